```python
import math
import jax, jax.numpy as jnp
from jax import lax
import numpy as np

D_MODEL = 1024
BATCH = 32
SEQ = 2048
DEPTH = 2

FOX_HEADS = 4
FOX_HEAD_DIM = 64
MOBA_HEADS = 4
MOBA_HEAD_DIM = 64
MOBA_BLOCK = 256
MOBA_TOPK = 3
MOBA_Q_CHUNK = 32
MLA_HEADS = 8
MLA_NOPE_DIM = 64
MLA_ROPE_DIM = 32
MLA_V_DIM = 64
MLA_Q_RANK = 256
MLA_KV_RANK = 128
ROPE_THETA = 10000.0
T5_BUCKETS = 32
T5_MAX_DISTANCE = 128
Q_BLOCK = 128
D_FF = -(-8 * D_MODEL // (3 * 256)) * 256
RMS_EPS = 1e-6

FOX_W = FOX_HEADS * FOX_HEAD_DIM
MOBA_W = MOBA_HEADS * MOBA_HEAD_DIM
MLA_W = MLA_HEADS * MLA_V_DIM
MIX_WIDTH = FOX_W + MOBA_W + MLA_W
IN_SIZES = (FOX_W, FOX_W, FOX_W, FOX_HEADS, MOBA_W, MOBA_W, MOBA_W, MLA_Q_RANK, MLA_KV_RANK, MLA_ROPE_DIM)
IN_WIDTH = 3 * FOX_W + FOX_HEADS + 3 * MOBA_W + MLA_Q_RANK + MLA_KV_RANK + MLA_ROPE_DIM

kernel_name = 'hybrid_fox_moba_mla_block'


def rmsnorm(x, g):
    x32 = x.astype(jnp.float32)
    y = x32 * lax.rsqrt(jnp.mean(x32 * x32, axis=-1, keepdims=True) + RMS_EPS)
    return (y * g.astype(jnp.float32)).astype(x.dtype)


def rope(x, pos):
    half = x.shape[-1] // 2
    inv_freq = 1.0 / (ROPE_THETA ** (jnp.arange(half, dtype=jnp.float32) / half))
    ang = pos.astype(jnp.float32)[:, None] * inv_freq[None, :]
    cos, sin = jnp.cos(ang), jnp.sin(ang)
    x32 = x.astype(jnp.float32)
    x1, x2 = x32[..., :half], x32[..., half:]
    return jnp.concatenate([x1 * cos - x2 * sin, x2 * cos + x1 * sin], axis=-1).astype(x.dtype)


def t5_bucket(dist):
    n = jnp.maximum(dist, 0)
    max_exact = T5_BUCKETS // 2
    nf = jnp.maximum(n, max_exact).astype(jnp.float32)
    large = max_exact + (jnp.log(nf / max_exact) / math.log(T5_MAX_DISTANCE / max_exact)
                         * (T5_BUCKETS - max_exact)).astype(jnp.int32)
    large = jnp.minimum(large, T5_BUCKETS - 1)
    return jnp.where(n < max_exact, n, large)


def dense_causal_attention(q, k, v, log_decay_cum=None):
    B, H, S, dk = q.shape
    dv = v.shape[-1]
    nqb = S // Q_BLOCK
    scale = dk ** -0.5
    kpos = jnp.arange(S)
    q_blocks = q.reshape(B, H, nqb, Q_BLOCK, dk).transpose(2, 0, 1, 3, 4)
    starts = jnp.arange(nqb, dtype=jnp.int32) * Q_BLOCK
    if log_decay_cum is None:
        xs = (q_blocks, starts)
    else:
        xs = (q_blocks, starts, log_decay_cum.reshape(B, H, nqb, Q_BLOCK).transpose(2, 0, 1, 3))

    def step(blk):
        q_i, t0 = blk[0], blk[1]
        s = jnp.einsum('bhqd,bhkd->bhqk', q_i, k).astype(jnp.float32) * scale
        if log_decay_cum is not None:
            s = s + blk[2][..., None] - log_decay_cum[:, :, None, :]
        qpos = t0 + jnp.arange(Q_BLOCK)
        s = jnp.where(kpos[None, :] <= qpos[:, None], s, -jnp.inf)
        p = jax.nn.softmax(s, axis=-1)
        return jnp.einsum('bhqk,bhkd->bhqd', p.astype(v.dtype), v)

    out = lax.map(step, xs)
    return out.transpose(1, 2, 0, 3, 4).reshape(B, H, S, dv)


def moba_attention(q, k, v, t5_table):
    B, H, S, d = q.shape
    nb = -(-S // MOBA_BLOCK)
    s_pad = nb * MOBA_BLOCK
    n_sel = min(MOBA_TOPK, nb)
    scale = d ** -0.5
    pad = ((0, 0), (0, 0), (0, s_pad - S), (0, 0))
    k_blocks = jnp.pad(k, pad).reshape(B, H, nb, MOBA_BLOCK, d)
    v_blocks = jnp.pad(v, pad).reshape(B, H, nb, MOBA_BLOCK, d)
    k_mean = jnp.mean(k_blocks.astype(jnp.float32), axis=3)
    qpos = jnp.arange(S)
    gate = jnp.einsum('bhsd,bhnd->bhsn', q.astype(jnp.float32), k_mean)
    past = jnp.arange(nb)[None, :] < (qpos // MOBA_BLOCK)[:, None]
    gate = jnp.where(past, gate, -jnp.inf)
    _, sel = lax.top_k(gate, n_sel)

    nc = S // MOBA_Q_CHUNK
    q_c = q.reshape(B, H, nc, MOBA_Q_CHUNK, d).transpose(2, 0, 1, 3, 4)
    sel_c = sel.reshape(B, H, nc, MOBA_Q_CHUNK, n_sel).transpose(2, 0, 1, 3, 4)
    starts = jnp.arange(nc, dtype=jnp.int32) * MOBA_Q_CHUNK
    table_t = t5_table.T
    b_ix = jnp.arange(B)[:, None, None, None]
    h_ix = jnp.arange(H)[None, :, None, None]
    offs = jnp.arange(MOBA_BLOCK)

    def step(blk):
        q_i, sel_i, t0 = blk
        own = t0 // MOBA_BLOCK
        qp = t0 + jnp.arange(MOBA_Q_CHUNK)
        k_g = k_blocks[b_ix, h_ix, sel_i]
        v_g = v_blocks[b_ix, h_ix, sel_i]
        kp_sel = sel_i[..., None] * MOBA_BLOCK + offs
        bias_sel = table_t[h_ix[..., None], t5_bucket(qp[:, None, None] - kp_sel)]
        s_sel = jnp.einsum('bhqd,bhqnkd->bhqnk', q_i, k_g).astype(jnp.float32) * scale + bias_sel
        s_sel = jnp.where((sel_i < own)[..., None], s_sel, -jnp.inf)
        k_own = lax.dynamic_index_in_dim(k_blocks, own, axis=2, keepdims=False)
        v_own = lax.dynamic_index_in_dim(v_blocks, own, axis=2, keepdims=False)
        dist_own = qp[:, None] - (own * MOBA_BLOCK + offs)[None, :]
        s_own = (jnp.einsum('bhqd,bhkd->bhqk', q_i, k_own).astype(jnp.float32) * scale
                 + table_t[:, t5_bucket(dist_own)])
        s_own = jnp.where(dist_own >= 0, s_own, -jnp.inf)
        n_flat = n_sel * MOBA_BLOCK
        s = jnp.concatenate([s_sel.reshape(B, H, MOBA_Q_CHUNK, n_flat), s_own], axis=-1)
        p = jax.nn.softmax(s, axis=-1).astype(v.dtype)
        p_sel = p[..., :n_flat].reshape(B, H, MOBA_Q_CHUNK, n_sel, MOBA_BLOCK)
        p_own = p[..., n_flat:]
        return (jnp.einsum('bhqnk,bhqnkd->bhqd', p_sel, v_g)
                + jnp.einsum('bhqk,bhkd->bhqd', p_own, v_own))

    out = lax.map(step, (q_c, sel_c, starts))
    return out.transpose(1, 2, 0, 3, 4).reshape(B, H, S, d)


def hybrid_mixer(h, w_in, b_forget, g_q_lat, w_uq, g_kv_lat, w_ukv, g_group, w_out, t5_table):
    B, S, _ = h.shape
    proj = h @ w_in
    split_points = np.cumsum(IN_SIZES)[:-1].tolist()
    q_f, k_f, v_f, f_logit, q_m, k_m, v_m, c_q, c_kv, k_r = jnp.split(proj, split_points, axis=-1)

    def heads(t, n):
        return t.reshape(B, S, n, -1).transpose(0, 2, 1, 3)

    def flat(o):
        return o.transpose(0, 2, 1, 3).reshape(B, S, -1)

    pos = jnp.arange(S)
    log_f = jax.nn.log_sigmoid((f_logit + b_forget).astype(jnp.float32))
    F = jnp.cumsum(log_f, axis=1).transpose(0, 2, 1)
    o_f = dense_causal_attention(heads(q_f, FOX_HEADS), heads(k_f, FOX_HEADS), heads(v_f, FOX_HEADS), F)
    o_m = moba_attention(heads(q_m, MOBA_HEADS), heads(k_m, MOBA_HEADS), heads(v_m, MOBA_HEADS), t5_table)
    q_lat = heads(rmsnorm(c_q, g_q_lat) @ w_uq, MLA_HEADS)
    q_nope, q_rot = q_lat[..., :MLA_NOPE_DIM], q_lat[..., MLA_NOPE_DIM:]
    kv = heads(rmsnorm(c_kv, g_kv_lat) @ w_ukv, MLA_HEADS)
    k_nope, v_c = kv[..., :MLA_NOPE_DIM], kv[..., MLA_NOPE_DIM:]
    k_rot = rope(k_r[:, None], pos)
    q_c = jnp.concatenate([q_nope, rope(q_rot, pos)], axis=-1)
    k_c = jnp.concatenate([k_nope, jnp.broadcast_to(k_rot, (B, MLA_HEADS, S, MLA_ROPE_DIM))], axis=-1)
    o_c = dense_causal_attention(q_c, k_c, v_c)
    g_a, g_b, g_c = jnp.split(g_group, [FOX_W, FOX_W + MOBA_W])
    o = jnp.concatenate([rmsnorm(flat(o_f), g_a), rmsnorm(flat(o_m), g_b), rmsnorm(flat(o_c), g_c)], axis=-1)
    return o @ w_out


def swiglu(h, w_gate_up, w_down):
    g, u = jnp.split(h @ w_gate_up, 2, axis=-1)
    return (jax.nn.silu(g) * u) @ w_down


def setup_inputs(seed: int = 0) -> dict:
    key = jax.random.key(seed)
    ks = jax.random.split(key, 20)
    f32 = jnp.float32

    def nrm(k, shape, scale):
        return jax.random.normal(k, shape, f32) * scale

    def gain(k, shape):
        return 1.0 + 0.1 * jax.random.normal(k, shape, f32)

    L = DEPTH
    return {
        'x': nrm(ks[0], (BATCH, SEQ, D_MODEL), 1.0),
        'c': nrm(ks[1], (BATCH, D_MODEL), 1.0),
        't5_table': nrm(ks[2], (T5_BUCKETS, MOBA_HEADS), 0.5),
        'w_ada': nrm(ks[3], (L, D_MODEL, 6 * D_MODEL), D_MODEL ** -0.5),
        'b_ada': nrm(ks[4], (L, 6 * D_MODEL), 0.02),
        'g_mix_pre': gain(ks[5], (L, D_MODEL)),
        'g_mix_post': gain(ks[6], (L, D_MODEL)),
        'w_in': nrm(ks[7], (L, D_MODEL, IN_WIDTH), D_MODEL ** -0.5),
        'b_forget': 2.0 + 0.5 * jax.random.normal(ks[8], (L, FOX_HEADS), f32),
        'g_q_lat': gain(ks[9], (L, MLA_Q_RANK)),
        'w_uq': nrm(ks[10], (L, MLA_Q_RANK, MLA_HEADS * (MLA_NOPE_DIM + MLA_ROPE_DIM)), MLA_Q_RANK ** -0.5),
        'g_kv_lat': gain(ks[11], (L, MLA_KV_RANK)),
        'w_ukv': nrm(ks[12], (L, MLA_KV_RANK, MLA_HEADS * (MLA_NOPE_DIM + MLA_V_DIM)), MLA_KV_RANK ** -0.5),
        'g_group': gain(ks[13], (L, MIX_WIDTH)),
        'w_out': nrm(ks[14], (L, MIX_WIDTH, D_MODEL), MIX_WIDTH ** -0.5),
        'g_ffn_pre': gain(ks[15], (L, D_MODEL)),
        'g_ffn_post': gain(ks[16], (L, D_MODEL)),
        'w_gate_up': nrm(ks[17], (L, D_MODEL, 2 * D_FF), D_MODEL ** -0.5),
        'w_down': nrm(ks[18], (L, D_FF, D_MODEL), D_FF ** -0.5),
    }


def reference(x, c, t5_table, w_ada, b_ada, g_mix_pre, g_mix_post, w_in, b_forget, g_q_lat, w_uq,
              g_kv_lat, w_ukv, g_group, w_out, g_ffn_pre, g_ffn_post, w_gate_up, w_down):
    c_act = jax.nn.silu(c)
    for l in range(DEPTH):
        mod = (c_act @ w_ada[l] + b_ada[l])[:, None, :]
        shift_a, scale_a, gate_a, shift_f, scale_f, gate_f = jnp.split(mod, 6, axis=-1)
        h = rmsnorm(x, g_mix_pre[l]) * (1.0 + scale_a) + shift_a
        y = hybrid_mixer(h, w_in[l], b_forget[l], g_q_lat[l], w_uq[l], g_kv_lat[l], w_ukv[l],
                         g_group[l], w_out[l], t5_table)
        x = x + gate_a * rmsnorm(y, g_mix_post[l])
        h = rmsnorm(x, g_ffn_pre[l]) * (1.0 + scale_f) + shift_f
        x = x + gate_f * rmsnorm(swiglu(h, w_gate_up[l], w_down[l]), g_ffn_post[l])
    return x
```

```python
import functools
import math

import jax
import jax.numpy as jnp
import numpy as np
from jax import lax
from jax.experimental import pallas as pl
from jax.experimental.pallas import tpu as pltpu

D_MODEL = 1024
DEPTH = 2
FOX_HEADS = 4
FOX_HEAD_DIM = 64
MOBA_HEADS = 4
MOBA_HEAD_DIM = 64
MOBA_BLOCK = 256
MOBA_TOPK = 3
MLA_HEADS = 8
MLA_NOPE_DIM = 64
MLA_ROPE_DIM = 32
MLA_V_DIM = 64
MLA_Q_RANK = 256
MLA_KV_RANK = 128
ROPE_THETA = 10000.0
T5_BUCKETS = 32
T5_MAX_DISTANCE = 128
D_FF = -(-8 * D_MODEL // (3 * 256)) * 256
RMS_EPS = 1e-6
FOX_W = FOX_HEADS * FOX_HEAD_DIM
MOBA_W = MOBA_HEADS * MOBA_HEAD_DIM
MLA_W = MLA_HEADS * MLA_V_DIM
MIX_WIDTH = FOX_W + MOBA_W + MLA_W
IN_SIZES = (FOX_W, FOX_W, FOX_W, FOX_HEADS, MOBA_W, MOBA_W, MOBA_W, MLA_Q_RANK, MLA_KV_RANK, MLA_ROPE_DIM)

V7X_LANES = 128
V7X_VMEM_LIMIT_BYTES = 56 * 1024 * 1024

TM = 512
TQ = 256
FFN_CHUNKS = ((0, 1024), (1024, 2048), (2048, D_FF))

HEAD_LANES = 64
MASK_NEG = -1e30
FGATE_SLOT = 8
N_SPLIT = 3

_C_QF, _C_KF, _C_VF = 0, 256, 512
_C_QM, _C_KM, _C_VM = 768, 1024, 1280
_C_CQ, _C_CKV = 1536, 1792
_C_FG, _C_KR, _C_KRS = 1920, 2048, 2176
IN_WIDTH_PADDED = 2304


def _in_proj_columns():
    off = np.cumsum((0,) + IN_SIZES)
    q_f, k_f, v_f, f_g, q_m, k_m, v_m, c_q, c_kv, k_r = (np.arange(off[i], off[i + 1]) for i in range(10))
    fg = np.full((V7X_LANES,), -1, np.int64)
    for h in range(FOX_HEADS):
        fg[FGATE_SLOT * h:FGATE_SLOT * h + 2 * N_SPLIT] = f_g[h]
    half = MLA_ROPE_DIM // 2
    kr4 = np.tile(k_r, V7X_LANES // MLA_ROPE_DIM)
    kr4s = np.tile(np.roll(k_r, -half), V7X_LANES // MLA_ROPE_DIM)
    cols = np.concatenate([q_f, k_f, v_f, q_m, k_m, v_m, c_q, c_kv, fg, kr4, kr4s])
    assert cols.shape == (IN_WIDTH_PADDED,)
    return cols


def _uq_columns():
    per = MLA_NOPE_DIM + MLA_ROPE_DIM
    half = MLA_ROPE_DIM // 2
    nope = np.concatenate([np.arange(h * per, h * per + MLA_NOPE_DIM) for h in range(MLA_HEADS)])
    rot = np.concatenate([np.arange(h * per + MLA_NOPE_DIM, (h + 1) * per) for h in range(MLA_HEADS)])
    rots = np.concatenate([np.roll(np.arange(h * per + MLA_NOPE_DIM, (h + 1) * per), -half) for h in range(MLA_HEADS)])
    return np.concatenate([nope, rot, rots])


def _ukv_columns():
    per = MLA_NOPE_DIM + MLA_V_DIM
    nope = np.concatenate([np.arange(h * per, h * per + MLA_NOPE_DIM) for h in range(MLA_HEADS)])
    val = np.concatenate([np.arange(h * per + MLA_NOPE_DIM, (h + 1) * per) for h in range(MLA_HEADS)])
    return np.concatenate([nope, val])


def _take_columns(w, cols):
    g = jnp.take(w, jnp.asarray(np.maximum(cols, 0), jnp.int32), axis=1)
    return jnp.where(jnp.asarray(cols >= 0)[None, :], g, 0.0)


def _t5_bucket_table(n):
    d = np.arange(n, dtype=np.int32)
    max_exact = T5_BUCKETS // 2
    nf = np.maximum(d, max_exact).astype(np.float32)
    ratio = np.log(nf / np.float32(max_exact)) / np.float32(math.log(T5_MAX_DISTANCE / max_exact))
    large = max_exact + (ratio.astype(np.float32) * np.float32(T5_BUCKETS - max_exact)).astype(np.int32)
    large = np.minimum(large, T5_BUCKETS - 1)
    return np.where(d < max_exact, d, large).astype(np.int32)


def _const_spec(shape):
    nd = len(shape)
    return pl.BlockSpec(shape, lambda *_: (0,) * nd, pipeline_mode=pl.Buffered(1))


def _params(*sem):
    return pltpu.CompilerParams(dimension_semantics=sem, vmem_limit_bytes=V7X_VMEM_LIMIT_BYTES)


def _rms(x, g):
    return x * lax.rsqrt(jnp.mean(x * x, axis=-1, keepdims=True) + RMS_EPS) * g


def _split3(v):
    hi = v.astype(jnp.bfloat16)
    r1 = v - hi.astype(jnp.float32)
    mid = r1.astype(jnp.bfloat16)
    lo = (r1 - mid.astype(jnp.float32)).astype(jnp.bfloat16)
    return hi, mid, lo


def _dot(a, b):
    return jnp.dot(a, b, preferred_element_type=jnp.float32)


def _dot_nt(a, b):
    return lax.dot_general(a, b, (((1,), (1,)), ((), ())), preferred_element_type=jnp.float32)


def _ada_kernel(c_ref, w_ref, b_ref, o_ref):
    c = c_ref[...]
    act = (c * jax.nn.sigmoid(c)).astype(jnp.bfloat16)
    o_ref[0] = _dot(act, w_ref[0].astype(jnp.bfloat16)) + b_ref[0]


def _ada_mod(c, w_ada, b_ada):
    depth, d, six_d = w_ada.shape
    batch = c.shape[0]
    n_col = six_d // d
    return pl.pallas_call(
        _ada_kernel,
        out_shape=jax.ShapeDtypeStruct((depth, batch, six_d), jnp.float32),
        grid=(depth, n_col),
        in_specs=[
            pl.BlockSpec((batch, d), lambda l, j: (0, 0)),
            pl.BlockSpec((1, d, d), lambda l, j: (l, 0, j)),
            pl.BlockSpec((1, 1, d), lambda l, j: (l, 0, j)),
        ],
        out_specs=pl.BlockSpec((1, batch, d), lambda l, j: (l, 0, j)),
        compiler_params=_params("arbitrary", "arbitrary"),
        name="ada_mod",
    )(c, w_ada, b_ada.reshape(depth, 1, six_d))


def _proj_kernel(x_ref, mod_ref, gpre_ref, win_ref, fb_ref, gq_ref, wuq_ref, gkv_ref, wukv_ref,
                 cos_ref, sin_ref, tril_ref,
                 qf_ref, eq_ref, kf_ref, ek_ref, vf_ref, qm_ref, km_ref, vm_ref, kmean_ref,
                 qn_ref, qr_ref, kn_ref, kr_ref, vc_ref, carry_ref):
    t = pl.program_id(1)
    bf16 = jnp.bfloat16
    x = x_ref[0]
    shift = mod_ref[0, :, 0:D_MODEL]
    scale = mod_ref[0, :, D_MODEL:2 * D_MODEL]
    h = (_rms(x, gpre_ref[...]) * (1.0 + scale) + shift).astype(bf16)

    def seg(c0, width):
        return _dot(h, win_ref[:, c0:c0 + width])

    qf_ref[0] = (seg(_C_QF, FOX_W) * FOX_HEAD_DIM ** -0.5).astype(bf16)
    kf_ref[0] = seg(_C_KF, FOX_W).astype(bf16)
    vf_ref[0] = seg(_C_VF, FOX_W).astype(bf16)
    qm_ref[0] = (seg(_C_QM, MOBA_W) * MOBA_HEAD_DIM ** -0.5).astype(bf16)
    km = seg(_C_KM, MOBA_W)
    km_ref[0] = km.astype(bf16)
    vm_ref[0] = seg(_C_VM, MOBA_W).astype(bf16)
    kmean_ref[0, 0] = jnp.mean(km.reshape(TM // MOBA_BLOCK, MOBA_BLOCK, MOBA_W), axis=1)

    lane = lax.broadcasted_iota(jnp.int32, (1, V7X_LANES), 1)
    slot = lane % FGATE_SLOT
    used = (lane < FGATE_SLOT * FOX_HEADS) & (slot < 2 * N_SPLIT)
    fl = seg(_C_FG, V7X_LANES) + fb_ref[...]
    logf = jnp.where(used, jnp.minimum(fl, 0.0) - jnp.log1p(jnp.exp(-jnp.abs(fl))), 0.0)

    @pl.when(t == 0)
    def _():
        carry_ref[...] = jnp.zeros_like(carry_ref)

    fcum = _dot(tril_ref[...], jnp.concatenate(_split3(logf), axis=0)) + carry_ref[0:1, :]
    carry_ref[0:1, :] = fcum[TM - 1:TM, :]
    hi, mid, lo = (p.astype(jnp.float32) for p in _split3(fcum))
    parts = jnp.where(slot % N_SPLIT == 0, hi, jnp.where(slot % N_SPLIT == 1, mid, lo))
    eq_ref[0] = jnp.where(used, jnp.where(slot < N_SPLIT, parts, 1.0), 0.0).astype(bf16)
    ek_ref[0] = jnp.where(used, jnp.where(slot < N_SPLIT, 1.0, -parts), 0.0).astype(bf16)

    cos = cos_ref[...]
    sin = sin_ref[...]
    mla_scale = (MLA_NOPE_DIM + MLA_ROPE_DIM) ** -0.5
    cq = _rms(seg(_C_CQ, MLA_Q_RANK), gq_ref[...]).astype(bf16)
    n_nope = MLA_HEADS * MLA_NOPE_DIM
    n_rot = MLA_HEADS * MLA_ROPE_DIM
    qn_ref[0] = (_dot(cq, wuq_ref[:, 0:n_nope]) * mla_scale).astype(bf16)
    q_rot = _dot(cq, wuq_ref[:, n_nope:n_nope + n_rot])
    q_rot_sw = _dot(cq, wuq_ref[:, n_nope + n_rot:n_nope + 2 * n_rot])
    cos2 = jnp.concatenate([cos] * (n_rot // V7X_LANES), axis=1)
    sin2 = jnp.concatenate([sin] * (n_rot // V7X_LANES), axis=1)
    qr_ref[0] = ((q_rot * cos2 + q_rot_sw * sin2) * mla_scale).astype(bf16)
    ckv = _rms(seg(_C_CKV, MLA_KV_RANK), gkv_ref[...]).astype(bf16)
    kn_ref[0] = _dot(ckv, wukv_ref[:, 0:n_nope]).astype(bf16)
    vc_ref[0] = _dot(ckv, wukv_ref[:, n_nope:n_nope + MLA_W]).astype(bf16)
    kr_ref[0] = (seg(_C_KR, V7X_LANES) * cos + seg(_C_KRS, V7X_LANES) * sin).astype(bf16)


def _proj(x, mod_l, gpre, win, fbias, gq, wuq, gkv, wukv, cos_t, sin_t, tril):
    batch, seq, d = x.shape
    nt = seq // TM
    bf16 = jnp.bfloat16

    def tok(width):
        return pl.BlockSpec((1, TM, width), lambda b, t: (b, t, 0))

    def out(width):
        return jax.ShapeDtypeStruct((batch, seq, width), bf16)

    widths = dict(qf=FOX_W, eq=V7X_LANES, kf=FOX_W, ek=V7X_LANES, vf=FOX_W, qm=MOBA_W, km=MOBA_W, vm=MOBA_W)
    mla_widths = dict(qn=MLA_HEADS * MLA_NOPE_DIM, qr=MLA_HEADS * MLA_ROPE_DIM, kn=MLA_HEADS * MLA_NOPE_DIM,
                      kr=V7X_LANES, vc=MLA_W)
    nb = TM // MOBA_BLOCK
    out_shape = ([out(w) for w in widths.values()]
                 + [jax.ShapeDtypeStruct((batch, nt, nb, MOBA_W), jnp.float32)]
                 + [out(w) for w in mla_widths.values()])
    out_specs = ([tok(w) for w in widths.values()]
                 + [pl.BlockSpec((1, 1, nb, MOBA_W), lambda b, t: (b, t, 0, 0))]
                 + [tok(w) for w in mla_widths.values()])
    res = pl.pallas_call(
        _proj_kernel,
        out_shape=out_shape,
        grid=(batch, nt),
        in_specs=[
            tok(d),
            pl.BlockSpec((1, 1, 6 * d), lambda b, t: (b, 0, 0)),
            _const_spec((1, d)),
            _const_spec(win.shape),
            _const_spec((1, V7X_LANES)),
            _const_spec((1, MLA_Q_RANK)),
            _const_spec(wuq.shape),
            _const_spec((1, MLA_KV_RANK)),
            _const_spec(wukv.shape),
            pl.BlockSpec((TM, V7X_LANES), lambda b, t: (t, 0)),
            pl.BlockSpec((TM, V7X_LANES), lambda b, t: (t, 0)),
            _const_spec(tril.shape),
        ],
        out_specs=out_specs,
        scratch_shapes=[pltpu.VMEM((8, V7X_LANES), jnp.float32)],
        compiler_params=_params("arbitrary", "arbitrary"),
        name="in_proj",
    )(x, mod_l, gpre, win, fbias, gq, wuq, gkv, wukv, cos_t, sin_t, tril)
    names = list(widths) + ["kmean"] + list(mla_widths)
    r = dict(zip(names, res))
    r["kmean"] = r["kmean"].reshape(batch, seq // MOBA_BLOCK, MOBA_W)
    return r


def _flash_head(lhs, i, rhs_tile, v_tile, diag_bias, off_bias):
    s = _dot_nt(lhs, rhs_tile(i))
    b = diag_bias()
    if b is not None:
        s = s + b
    m = jnp.max(s, axis=-1, keepdims=True)
    p = jnp.exp(s - m)
    l = jnp.sum(p, axis=-1, keepdims=True)
    acc = _dot(p.astype(jnp.bfloat16), v_tile(i))

    def step(jj, carry):
        m, l, acc = carry
        j = i - 1 - jj
        s = _dot_nt(lhs, rhs_tile(j))
        b = off_bias(j)
        if b is not None:
            s = s + b
        m_new = jnp.maximum(m, jnp.max(s, axis=-1, keepdims=True))
        alpha = jnp.exp(m - m_new)
        p = jnp.exp(s - m_new)
        l = alpha * l + jnp.sum(p, axis=-1, keepdims=True)
        acc = alpha * acc + _dot(p.astype(jnp.bfloat16), v_tile(j))
        return m_new, l, acc

    m, l, acc = lax.fori_loop(0, i, step, (m, l, acc))
    return acc / l


def _causal_bias():
    row = lax.broadcasted_iota(jnp.int32, (TQ, TQ), 0)
    col = lax.broadcasted_iota(jnp.int32, (TQ, TQ), 1)
    return jnp.where(row >= col, 0.0, MASK_NEG)


def _pair_merge(o_even, o_odd):
    lane = lax.broadcasted_iota(jnp.int32, (1, V7X_LANES), 1)
    return jnp.where(lane < HEAD_LANES, o_even, o_odd)


def _half_mask(x, e):
    lane = lax.broadcasted_iota(jnp.int32, (1, V7X_LANES), 1)
    keep = (lane >= HEAD_LANES) if e else (lane < HEAD_LANES)
    return jnp.where(keep, x, jnp.zeros_like(x))


def _group_norm_store(o_ref, pairs, g_ref):
    o = jnp.concatenate(pairs, axis=1)
    o_ref[0] = _rms(o, g_ref[...]).astype(o_ref.dtype)


def _fox_kernel(q_ref, eq_ref, k_ref, ek_ref, v_ref, g_ref, o_ref):
    i = pl.program_id(1)
    lane = lax.broadcasted_iota(jnp.int32, (1, V7X_LANES), 1)
    pairs = []
    for p in range(FOX_HEADS // 2):
        cs = slice(p * V7X_LANES, (p + 1) * V7X_LANES)
        lhs = jnp.concatenate([q_ref[0, :, cs], eq_ref[0]], axis=1)
        outs = []
        for e in range(2):
            h = 2 * p + e

            def rhs_tile(j, e=e, h=h, cs=cs):
                rows = pl.ds(pl.multiple_of(j * TQ, TQ), TQ)
                ek = ek_ref[0, rows, :]
                ekm = jnp.where(lane // FGATE_SLOT == h, ek, jnp.zeros_like(ek))
                return jnp.concatenate([_half_mask(k_ref[0, rows, cs], e), ekm], axis=1)

            def v_tile(j, cs=cs):
                return v_ref[0, pl.ds(pl.multiple_of(j * TQ, TQ), TQ), cs]

            outs.append(_flash_head(lhs, i, rhs_tile, v_tile, _causal_bias, lambda j: None))
        pairs.append(_pair_merge(*outs))
    _group_norm_store(o_ref, pairs, g_ref)


def _moba_kernel(q_ref, k_ref, v_ref, kmean_ref, bias_ref, g_ref, o_ref):
    i = pl.program_id(1)
    bf16 = jnp.bfloat16
    lane = lax.broadcasted_iota(jnp.int32, (1, V7X_LANES), 1)
    n_blocks = kmean_ref.shape[1]
    pairs = []
    for p in range(MOBA_HEADS // 2):
        cs = slice(p * V7X_LANES, (p + 1) * V7X_LANES)
        q2 = q_ref[0, :, cs]
        outs = []
        for e in range(2):
            h = 2 * p + e
            kmean = _half_mask(kmean_ref[0, :, cs], e)
            km_hi = kmean.astype(bf16)
            km_lo = (kmean - km_hi.astype(jnp.float32)).astype(bf16)
            pad = jnp.zeros((V7X_LANES - n_blocks, 2 * V7X_LANES), bf16)
            gate_rhs = jnp.concatenate([jnp.concatenate([km_hi, km_lo], axis=1), pad], axis=0)
            gate = _dot_nt(jnp.concatenate([q2, q2], axis=1), gate_rhs)
            beaten = jnp.zeros((TQ, V7X_LANES), jnp.float32)
            for mblk in range(n_blocks - 1):
                gm = gate[:, mblk:mblk + 1]
                wins = (gm > gate) | ((gm == gate) & (mblk < lane))
                beaten = beaten + jnp.where(wins, jnp.where(mblk < i, 1.0, 0.0), 0.0)
            keep = ((lane < i) & (beaten < MOBA_TOPK)) | (lane == i)
            sel = jnp.where(keep, 0.0, MASK_NEG).astype(bf16)
            lhs = jnp.concatenate([q2, sel], axis=1)

            def rhs_tile(j, e=e, cs=cs):
                rows = pl.ds(pl.multiple_of(j * TQ, TQ), TQ)
                onehot = jnp.broadcast_to(jnp.where(lane == j, 1.0, 0.0).astype(bf16), (TQ, V7X_LANES))
                return jnp.concatenate([_half_mask(k_ref[0, rows, cs], e), onehot], axis=1)

            def v_tile(j, cs=cs):
                return v_ref[0, pl.ds(pl.multiple_of(j * TQ, TQ), TQ), cs]

            def off_bias(j, h=h):
                return bias_ref[h, jnp.minimum(i - j, 2)]

            outs.append(_flash_head(lhs, i, rhs_tile, v_tile, lambda h=h: bias_ref[h, 0], off_bias))
        pairs.append(_pair_merge(*outs))
    _group_norm_store(o_ref, pairs, g_ref)


def _mla_kernel(qn_ref, qr_ref, kn_ref, kr_ref, v_ref, g_ref, o_ref):
    i = pl.program_id(1)
    lane = lax.broadcasted_iota(jnp.int32, (1, V7X_LANES), 1)
    heads_per_rot = V7X_LANES // MLA_ROPE_DIM
    pairs = []
    for p in range(MLA_HEADS // 2):
        cs = slice(p * V7X_LANES, (p + 1) * V7X_LANES)
        quad = (2 * p) // heads_per_rot
        lhs = jnp.concatenate([qn_ref[0, :, cs], qr_ref[0, :, quad * V7X_LANES:(quad + 1) * V7X_LANES]], axis=1)
        outs = []
        for e in range(2):
            h = 2 * p + e

            def rhs_tile(j, e=e, h=h, cs=cs):
                rows = pl.ds(pl.multiple_of(j * TQ, TQ), TQ)
                kr = kr_ref[0, rows, :]
                krm = jnp.where(lane // MLA_ROPE_DIM == h % heads_per_rot, kr, jnp.zeros_like(kr))
                return jnp.concatenate([_half_mask(kn_ref[0, rows, cs], e), krm], axis=1)

            def v_tile(j, cs=cs):
                return v_ref[0, pl.ds(pl.multiple_of(j * TQ, TQ), TQ), cs]

            outs.append(_flash_head(lhs, i, rhs_tile, v_tile, _causal_bias, lambda j: None))
        pairs.append(_pair_merge(*outs))
    _group_norm_store(o_ref, pairs, g_ref)


def _attn_call(kernel, name, q_arrays, kv_arrays, const_arrays, out_width):
    batch, seq, _ = q_arrays[0].shape
    nq = seq // TQ
    in_specs = ([pl.BlockSpec((1, TQ, a.shape[2]), lambda b, i: (b, i, 0)) for a in q_arrays]
                + [pl.BlockSpec((1,) + a.shape[1:], lambda b, i: (b, 0, 0)) for a in kv_arrays]
                + [_const_spec(a.shape) for a in const_arrays])
    return pl.pallas_call(
        kernel,
        out_shape=jax.ShapeDtypeStruct((batch, seq, out_width), jnp.bfloat16),
        grid=(batch, nq),
        in_specs=in_specs,
        out_specs=pl.BlockSpec((1, TQ, out_width), lambda b, i: (b, i, 0)),
        compiler_params=_params("arbitrary", "arbitrary"),
        name=name,
    )(*q_arrays, *kv_arrays, *const_arrays)


def _moba_bias_tiles(t5_table):
    bucket = _t5_bucket_table(3 * TQ)
    row = np.arange(TQ)[:, None]
    col = np.arange(TQ)[None, :]
    table_t = t5_table.T
    tiles = []
    for k in range(3):
        dist = row - col + k * TQ
        b = table_t[:, bucket[np.maximum(dist, 0)]]
        if k == 0:
            b = jnp.where(jnp.asarray(dist >= 0)[None], b, MASK_NEG)
        tiles.append(b)
    return jnp.stack(tiles, axis=1).astype(jnp.float32)


def _out_kernel(x_ref, of_ref, om_ref, oc_ref, mod_ref, gpost_ref, w_ref, o_ref):
    o = jnp.concatenate([of_ref[0], om_ref[0], oc_ref[0]], axis=1)
    y = _dot(o, w_ref[...])
    gate = mod_ref[0, :, 2 * D_MODEL:3 * D_MODEL]
    o_ref[0] = x_ref[0] + gate * _rms(y, gpost_ref[...])


def _out_proj(x, o_f, o_m, o_c, mod_l, gpost, w_out):
    batch, seq, d = x.shape

    def tok(width):
        return pl.BlockSpec((1, TM, width), lambda b, t: (b, t, 0))

    return pl.pallas_call(
        _out_kernel,
        out_shape=jax.ShapeDtypeStruct(x.shape, x.dtype),
        grid=(batch, seq // TM),
        in_specs=[tok(d), tok(FOX_W), tok(MOBA_W), tok(MLA_W),
                  pl.BlockSpec((1, 1, 6 * d), lambda b, t: (b, 0, 0)),
                  _const_spec((1, d)), _const_spec(w_out.shape)],
        out_specs=tok(d),
        compiler_params=_params("arbitrary", "arbitrary"),
        name="out_proj",
    )(x, o_f, o_m, o_c, mod_l, gpost, w_out)


def _ffn_kernel(x_ref, mod_ref, gpre_ref, gpost_ref, wgu_ref, wd_ref, o_ref):
    bf16 = jnp.bfloat16
    x = x_ref[0]
    shift = mod_ref[0, :, 3 * D_MODEL:4 * D_MODEL]
    scale = mod_ref[0, :, 4 * D_MODEL:5 * D_MODEL]
    gate = mod_ref[0, :, 5 * D_MODEL:6 * D_MODEL]
    h = (_rms(x, gpre_ref[...]) * (1.0 + scale) + shift).astype(bf16)
    acc = None
    for c0, c1 in FFN_CHUNKS:
        g = _dot(h, wgu_ref[:, c0:c1])
        u = _dot(h, wgu_ref[:, D_FF + c0:D_FF + c1])
        a = (g * jax.nn.sigmoid(g) * u).astype(bf16)
        part = _dot(a, wd_ref[c0:c1, :])
        acc = part if acc is None else acc + part
    o_ref[0] = x + gate * _rms(acc, gpost_ref[...])


def _ffn(x, mod_l, gpre, gpost, wgu, wd):
    batch, seq, d = x.shape
    tok = pl.BlockSpec((1, TM, d), lambda b, t: (b, t, 0))
    return pl.pallas_call(
        _ffn_kernel,
        out_shape=jax.ShapeDtypeStruct(x.shape, x.dtype),
        grid=(batch, seq // TM),
        in_specs=[tok, pl.BlockSpec((1, 1, 6 * d), lambda b, t: (b, 0, 0)),
                  _const_spec((1, d)), _const_spec((1, d)), _const_spec(wgu.shape), _const_spec(wd.shape)],
        out_specs=tok,
        compiler_params=_params("arbitrary", "arbitrary"),
        name="ffn",
    )(x, mod_l, gpre, gpost, wgu, wd)


def _rope_tables(seq):
    half = MLA_ROPE_DIM // 2
    inv_freq = 1.0 / (ROPE_THETA ** (jnp.arange(half, dtype=jnp.float32) / half))
    ang = jnp.arange(seq).astype(jnp.float32)[:, None] * inv_freq[None, :]
    reps = V7X_LANES // MLA_ROPE_DIM
    cos = jnp.tile(jnp.concatenate([jnp.cos(ang), jnp.cos(ang)], axis=1), (1, reps))
    sin = jnp.tile(jnp.concatenate([-jnp.sin(ang), jnp.sin(ang)], axis=1), (1, reps))
    return cos, sin


def kernel(x, c, t5_table, w_ada, b_ada, g_mix_pre, g_mix_post, w_in, b_forget, g_q_lat, w_uq, g_kv_lat, w_ukv,
           g_group, w_out, g_ffn_pre, g_ffn_post, w_gate_up, w_down):
    batch, seq, d = x.shape
    assert d == D_MODEL and seq % TM == 0 and TM % MOBA_BLOCK == 0 and TQ == MOBA_BLOCK
    bf16 = jnp.bfloat16
    in_cols, uq_cols, ukv_cols = _in_proj_columns(), _uq_columns(), _ukv_columns()
    cos_t, sin_t = _rope_tables(seq)
    tril = np.tril(np.ones((TM, TM), np.float32))
    tril = jnp.asarray(np.concatenate([tril] * N_SPLIT, axis=1), bf16)
    moba_bias = _moba_bias_tiles(t5_table)
    fg_lane = np.arange(V7X_LANES)
    fg_used = (fg_lane < FGATE_SLOT * FOX_HEADS) & (fg_lane % FGATE_SLOT < 2 * N_SPLIT)
    fg_head = np.minimum(fg_lane // FGATE_SLOT, FOX_HEADS - 1)

    mod = _ada_mod(c, w_ada, b_ada)
    for l in range(DEPTH):
        mod_l = mod[l].reshape(batch, 1, 6 * d)
        fbias = jnp.where(jnp.asarray(fg_used), b_forget[l][fg_head], 0.0).reshape(1, V7X_LANES)
        pr = _proj(x, mod_l, g_mix_pre[l].reshape(1, d),
                   _take_columns(w_in[l], in_cols).astype(bf16), fbias,
                   g_q_lat[l].reshape(1, -1), _take_columns(w_uq[l], uq_cols).astype(bf16),
                   g_kv_lat[l].reshape(1, -1), _take_columns(w_ukv[l], ukv_cols).astype(bf16),
                   cos_t, sin_t, tril)
        g_a = g_group[l, :FOX_W].reshape(1, -1)
        g_b = g_group[l, FOX_W:FOX_W + MOBA_W].reshape(1, -1)
        g_c = g_group[l, FOX_W + MOBA_W:].reshape(1, -1)
        o_f = _attn_call(_fox_kernel, "fox_attn", [pr["qf"], pr["eq"]], [pr["kf"], pr["ek"], pr["vf"]], [g_a], FOX_W)
        o_m = _attn_call(_moba_kernel, "moba_attn", [pr["qm"]], [pr["km"], pr["vm"], pr["kmean"]],
                         [moba_bias, g_b], MOBA_W)
        o_c = _attn_call(_mla_kernel, "mla_attn", [pr["qn"], pr["qr"]], [pr["kn"], pr["kr"], pr["vc"]], [g_c], MLA_W)
        x = _out_proj(x, o_f, o_m, o_c, mod_l, g_mix_post[l].reshape(1, d), w_out[l].astype(bf16))
        x = _ffn(x, mod_l, g_ffn_pre[l].reshape(1, d), g_ffn_post[l].reshape(1, d),
                 w_gate_up[l].astype(bf16), w_down[l].astype(bf16))
    return x
```

```python
import functools
import math

import jax
import jax.numpy as jnp
import numpy as np
from jax import lax
from jax.experimental import pallas as pl
from jax.experimental.pallas import tpu as pltpu

D_MODEL = 1024
DEPTH = 2
FOX_HEADS = 4
FOX_HEAD_DIM = 64
MOBA_HEADS = 4
MOBA_HEAD_DIM = 64
MOBA_BLOCK = 256
MOBA_TOPK = 3
MLA_HEADS = 8
MLA_NOPE_DIM = 64
MLA_ROPE_DIM = 32
MLA_V_DIM = 64
MLA_Q_RANK = 256
MLA_KV_RANK = 128
ROPE_THETA = 10000.0
T5_BUCKETS = 32
T5_MAX_DISTANCE = 128
D_FF = -(-8 * D_MODEL // (3 * 256)) * 256
RMS_EPS = 1e-6
FOX_W = FOX_HEADS * FOX_HEAD_DIM
MOBA_W = MOBA_HEADS * MOBA_HEAD_DIM
MLA_W = MLA_HEADS * MLA_V_DIM
MIX_WIDTH = FOX_W + MOBA_W + MLA_W
IN_SIZES = (FOX_W, FOX_W, FOX_W, FOX_HEADS, MOBA_W, MOBA_W, MOBA_W, MLA_Q_RANK, MLA_KV_RANK, MLA_ROPE_DIM)

V7X_LANES = 128
V7X_VMEM_LIMIT_BYTES = 56 * 1024 * 1024

TM = 512
TQ = 256
FFN_CHUNKS = ((0, 1024), (1024, 2048), (2048, D_FF))

HEAD_LANES = 64
MASK_NEG = -1e30
FGATE_SLOT = 8
N_SPLIT = 3

_C_QF, _C_KF, _C_VF = 0, 256, 512
_C_QM, _C_KM, _C_VM = 768, 1024, 1280
_C_CQ, _C_CKV = 1536, 1792
_C_FG, _C_KR, _C_KRS = 1920, 2048, 2176
IN_WIDTH_PADDED = 2304


def _in_proj_columns():
    off = np.cumsum((0,) + IN_SIZES)
    q_f, k_f, v_f, f_g, q_m, k_m, v_m, c_q, c_kv, k_r = (np.arange(off[i], off[i + 1]) for i in range(10))
    fg = np.full((V7X_LANES,), -1, np.int64)
    for h in range(FOX_HEADS):
        fg[FGATE_SLOT * h:FGATE_SLOT * h + 2 * N_SPLIT] = f_g[h]
    half = MLA_ROPE_DIM // 2
    kr4 = np.tile(k_r, V7X_LANES // MLA_ROPE_DIM)
    kr4s = np.tile(np.roll(k_r, -half), V7X_LANES // MLA_ROPE_DIM)
    cols = np.concatenate([q_f, k_f, v_f, q_m, k_m, v_m, c_q, c_kv, fg, kr4, kr4s])
    assert cols.shape == (IN_WIDTH_PADDED,)
    return cols


def _uq_columns():
    per = MLA_NOPE_DIM + MLA_ROPE_DIM
    half = MLA_ROPE_DIM // 2
    nope = np.concatenate([np.arange(h * per, h * per + MLA_NOPE_DIM) for h in range(MLA_HEADS)])
    rot = np.concatenate([np.arange(h * per + MLA_NOPE_DIM, (h + 1) * per) for h in range(MLA_HEADS)])
    rots = np.concatenate([np.roll(np.arange(h * per + MLA_NOPE_DIM, (h + 1) * per), -half) for h in range(MLA_HEADS)])
    return np.concatenate([nope, rot, rots])


def _ukv_columns():
    per = MLA_NOPE_DIM + MLA_V_DIM
    nope = np.concatenate([np.arange(h * per, h * per + MLA_NOPE_DIM) for h in range(MLA_HEADS)])
    val = np.concatenate([np.arange(h * per + MLA_NOPE_DIM, (h + 1) * per) for h in range(MLA_HEADS)])
    return np.concatenate([nope, val])


def _take_columns(w, cols):
    g = jnp.take(w, jnp.asarray(np.maximum(cols, 0), jnp.int32), axis=1)
    return jnp.where(jnp.asarray(cols >= 0)[None, :], g, 0.0)


def _t5_bucket_table(n):
    d = np.arange(n, dtype=np.int32)
    max_exact = T5_BUCKETS // 2
    nf = np.maximum(d, max_exact).astype(np.float32)
    ratio = np.log(nf / np.float32(max_exact)) / np.float32(math.log(T5_MAX_DISTANCE / max_exact))
    large = max_exact + (ratio.astype(np.float32) * np.float32(T5_BUCKETS - max_exact)).astype(np.int32)
    large = np.minimum(large, T5_BUCKETS - 1)
    return np.where(d < max_exact, d, large).astype(np.int32)


def _const_spec(shape):
    nd = len(shape)
    return pl.BlockSpec(shape, lambda *_: (0,) * nd, pipeline_mode=pl.Buffered(1))


def _params(*sem):
    return pltpu.CompilerParams(dimension_semantics=sem, vmem_limit_bytes=V7X_VMEM_LIMIT_BYTES)


def _rms(x, g):
    return x * lax.rsqrt(jnp.mean(x * x, axis=-1, keepdims=True) + RMS_EPS) * g


def _split3(v):
    hi = v.astype(jnp.bfloat16)
    r1 = v - hi.astype(jnp.float32)
    mid = r1.astype(jnp.bfloat16)
    lo = (r1 - mid.astype(jnp.float32)).astype(jnp.bfloat16)
    return hi, mid, lo


def _dot(a, b):
    return jnp.dot(a, b, preferred_element_type=jnp.float32)


def _ada_kernel(c_ref, w_ref, b_ref, o_ref):
    c = c_ref[...]
    act = (c * jax.nn.sigmoid(c)).astype(jnp.bfloat16)
    o_ref[0] = _dot(act, w_ref[0].astype(jnp.bfloat16)) + b_ref[0]


def _ada_mod(c, w_ada, b_ada):
    depth, d, six_d = w_ada.shape
    batch = c.shape[0]
    n_col = six_d // d
    return pl.pallas_call(
        _ada_kernel,
        out_shape=jax.ShapeDtypeStruct((depth, batch, six_d), jnp.float32),
        grid=(depth, n_col),
        in_specs=[
            pl.BlockSpec((batch, d), lambda l, j: (0, 0)),
            pl.BlockSpec((1, d, d), lambda l, j: (l, 0, j)),
            pl.BlockSpec((1, 1, d), lambda l, j: (l, 0, j)),
        ],
        out_specs=pl.BlockSpec((1, batch, d), lambda l, j: (l, 0, j)),
        compiler_params=_params("arbitrary", "arbitrary"),
        name="ada_mod",
    )(c, w_ada, b_ada.reshape(depth, 1, six_d))


def _proj_kernel(x_ref, mod_ref, gpre_ref, win_ref, fb_ref, gq_ref, wuq_ref, gkv_ref, wukv_ref,
                 cos_ref, sin_ref, tril_ref,
                 qf_ref, eq_ref, kf_ref, ek_ref, vf_ref, qm_ref, km_ref, vm_ref, kmean_ref,
                 qn_ref, qr_ref, kn_ref, kr_ref, vc_ref, carry_ref):
    t = pl.program_id(1)
    bf16 = jnp.bfloat16
    x = x_ref[0]
    shift = mod_ref[0, :, 0:D_MODEL]
    scale = mod_ref[0, :, D_MODEL:2 * D_MODEL]
    h = (_rms(x, gpre_ref[...]) * (1.0 + scale) + shift).astype(bf16)

    def seg(c0, width):
        return _dot(h, win_ref[:, c0:c0 + width])

    qf_ref[0] = (seg(_C_QF, FOX_W) * FOX_HEAD_DIM ** -0.5).astype(bf16)
    kf_ref[0] = seg(_C_KF, FOX_W).astype(bf16)
    vf_ref[0] = seg(_C_VF, FOX_W).astype(bf16)
    qm_ref[0] = (seg(_C_QM, MOBA_W) * MOBA_HEAD_DIM ** -0.5).astype(bf16)
    km = seg(_C_KM, MOBA_W)
    km_ref[0] = km.astype(bf16)
    vm_ref[0] = seg(_C_VM, MOBA_W).astype(bf16)
    kmean_ref[0, 0] = jnp.mean(km.reshape(TM // MOBA_BLOCK, MOBA_BLOCK, MOBA_W), axis=1)

    lane = lax.broadcasted_iota(jnp.int32, (1, V7X_LANES), 1)
    slot = lane % FGATE_SLOT
    used = (lane < FGATE_SLOT * FOX_HEADS) & (slot < 2 * N_SPLIT)
    fl = seg(_C_FG, V7X_LANES) + fb_ref[...]
    logf = jnp.where(used, jnp.minimum(fl, 0.0) - jnp.log1p(jnp.exp(-jnp.abs(fl))), 0.0)

    @pl.when(t == 0)
    def _():
        carry_ref[...] = jnp.zeros_like(carry_ref)

    fcum = _dot(tril_ref[...], jnp.concatenate(_split3(logf), axis=0)) + carry_ref[0:1, :]
    carry_ref[0:1, :] = fcum[TM - 1:TM, :]
    hi, mid, lo = (p.astype(jnp.float32) for p in _split3(fcum))
    parts = jnp.where(slot % N_SPLIT == 0, hi, jnp.where(slot % N_SPLIT == 1, mid, lo))
    eq_ref[0] = jnp.where(used, jnp.where(slot < N_SPLIT, parts, 1.0), 0.0).astype(bf16)
    ek_ref[0] = jnp.where(used, jnp.where(slot < N_SPLIT, 1.0, -parts), 0.0).astype(bf16)

    cos = cos_ref[...]
    sin = sin_ref[...]
    mla_scale = (MLA_NOPE_DIM + MLA_ROPE_DIM) ** -0.5
    cq = _rms(seg(_C_CQ, MLA_Q_RANK), gq_ref[...]).astype(bf16)
    n_nope = MLA_HEADS * MLA_NOPE_DIM
    n_rot = MLA_HEADS * MLA_ROPE_DIM
    qn_ref[0] = (_dot(cq, wuq_ref[:, 0:n_nope]) * mla_scale).astype(bf16)
    q_rot = _dot(cq, wuq_ref[:, n_nope:n_nope + n_rot])
    q_rot_sw = _dot(cq, wuq_ref[:, n_nope + n_rot:n_nope + 2 * n_rot])
    cos2 = jnp.concatenate([cos] * (n_rot // V7X_LANES), axis=1)
    sin2 = jnp.concatenate([sin] * (n_rot // V7X_LANES), axis=1)
    qr_ref[0] = ((q_rot * cos2 + q_rot_sw * sin2) * mla_scale).astype(bf16)
    ckv = _rms(seg(_C_CKV, MLA_KV_RANK), gkv_ref[...]).astype(bf16)
    kn_ref[0] = _dot(ckv, wukv_ref[:, 0:n_nope]).astype(bf16)
    vc_ref[0] = _dot(ckv, wukv_ref[:, n_nope:n_nope + MLA_W]).astype(bf16)
    kr_ref[0] = (seg(_C_KR, V7X_LANES) * cos + seg(_C_KRS, V7X_LANES) * sin).astype(bf16)


def _proj(x, mod_l, gpre, win, fbias, gq, wuq, gkv, wukv, cos_t, sin_t, tril):
    batch, seq, d = x.shape
    nt = seq // TM
    bf16 = jnp.bfloat16

    def tok(width):
        return pl.BlockSpec((1, TM, width), lambda b, t: (b, t, 0))

    def out(width):
        return jax.ShapeDtypeStruct((batch, seq, width), bf16)

    widths = dict(qf=FOX_W, eq=V7X_LANES, kf=FOX_W, ek=V7X_LANES, vf=FOX_W, qm=MOBA_W, km=MOBA_W, vm=MOBA_W)
    mla_widths = dict(qn=MLA_HEADS * MLA_NOPE_DIM, qr=MLA_HEADS * MLA_ROPE_DIM, kn=MLA_HEADS * MLA_NOPE_DIM,
                      kr=V7X_LANES, vc=MLA_W)
    nb = TM // MOBA_BLOCK
    out_shape = ([out(w) for w in widths.values()]
                 + [jax.ShapeDtypeStruct((batch, nt, nb, MOBA_W), jnp.float32)]
                 + [out(w) for w in mla_widths.values()])
    out_specs = ([tok(w) for w in widths.values()]
                 + [pl.BlockSpec((1, 1, nb, MOBA_W), lambda b, t: (b, t, 0, 0))]
                 + [tok(w) for w in mla_widths.values()])
    res = pl.pallas_call(
        _proj_kernel,
        out_shape=out_shape,
        grid=(batch, nt),
        in_specs=[
            tok(d),
            pl.BlockSpec((1, 1, 6 * d), lambda b, t: (b, 0, 0)),
            _const_spec((1, d)),
            _const_spec(win.shape),
            _const_spec((1, V7X_LANES)),
            _const_spec((1, MLA_Q_RANK)),
            _const_spec(wuq.shape),
            _const_spec((1, MLA_KV_RANK)),
            _const_spec(wukv.shape),
            pl.BlockSpec((TM, V7X_LANES), lambda b, t: (t, 0)),
            pl.BlockSpec((TM, V7X_LANES), lambda b, t: (t, 0)),
            _const_spec(tril.shape),
        ],
        out_specs=out_specs,
        scratch_shapes=[pltpu.VMEM((8, V7X_LANES), jnp.float32)],
        compiler_params=_params("arbitrary", "arbitrary"),
        name="in_proj",
    )(x, mod_l, gpre, win, fbias, gq, wuq, gkv, wukv, cos_t, sin_t, tril)
    names = list(widths) + ["kmean"] + list(mla_widths)
    r = dict(zip(names, res))
    r["kmean"] = r["kmean"].reshape(batch, seq // MOBA_BLOCK, MOBA_W)
    return r


def _transpose_to_bf16(x):
    return x.astype(jnp.float32).T.astype(jnp.bfloat16)


def _stage_values(i, v_ref, vt_ref):
    @pl.when(i == 0)
    def _():
        for c in range(vt_ref.shape[0]):
            vt_ref[c] = _transpose_to_bf16(v_ref[0, c * TQ:(c + 1) * TQ, :])


def _head_rows(x_pair, e):
    row = lax.broadcasted_iota(jnp.int32, x_pair.shape, 0)
    keep = (row >= HEAD_LANES) if e else (row < HEAD_LANES)
    return jnp.where(keep, x_pair, 0.0)


def _causal_bias_t():
    key = lax.broadcasted_iota(jnp.int32, (TQ, TQ), 0)
    qry = lax.broadcasted_iota(jnp.int32, (TQ, TQ), 1)
    return jnp.where(key <= qry, 0.0, MASK_NEG)


def _flash_tiles(i, n_heads, lhs_tile, bias_fn, rhs_ref, vt_ref, m_ref, l_ref, acc_ref):
    def tile(j, first):
        heads = range(n_heads)
        lhs = {h // 2: lhs_tile(j, h // 2) for h in heads if h % 2 == 0}
        scores = []
        for h in heads:
            s = _dot(lhs[h // 2], rhs_ref[h])
            b = bias_fn(j, h, first)
            scores.append(s if b is None else s + b)
        s_max = [jnp.max(s, axis=0, keepdims=True) for s in scores]
        if first:
            m_new, alpha = s_max, None
        else:
            m_old = [m_ref[h] for h in heads]
            m_new = [jnp.maximum(m_old[h], s_max[h]) for h in heads]
            alpha = [jnp.exp(m_old[h] - m_new[h]) for h in heads]
        probs = [jnp.exp(scores[h] - m_new[h]) for h in heads]
        p_sum = [jnp.sum(p, axis=0, keepdims=True) for p in probs]
        pv = [_dot(vt_ref[j, h * HEAD_LANES:(h + 1) * HEAD_LANES, :], probs[h].astype(jnp.bfloat16)) for h in heads]
        for h in heads:
            m_ref[h] = m_new[h]
            l_ref[h] = p_sum[h] if first else alpha[h] * l_ref[h] + p_sum[h]
            acc_ref[h] = pv[h] if first else alpha[h] * acc_ref[h] + pv[h]

    tile(i, True)

    def body(jj, carry):
        tile(i - 1 - jj, False)
        return carry

    lax.fori_loop(0, i, body, 0)


def _finish(n_heads, l_ref, acc_ref, g_ref, o_ref):
    o_t = jnp.concatenate([acc_ref[h] * (1.0 / l_ref[h]) for h in range(n_heads)], axis=0)
    o_ref[0] = _rms(o_t.T, g_ref[...]).astype(o_ref.dtype)


def _key_rows(j):
    return pl.ds(pl.multiple_of(j * TQ, TQ), TQ)


def _fox_kernel(q_ref, eq_ref, k_ref, ek_ref, v_ref, g_ref, o_ref, vt_ref, rhs_ref, m_ref, l_ref, acc_ref):
    i = pl.program_id(1)
    _stage_values(i, v_ref, vt_ref)
    q_t = q_ref[0].astype(jnp.float32).T
    eq_t = eq_ref[0].astype(jnp.float32).T
    slot_head = lax.broadcasted_iota(jnp.int32, eq_t.shape, 0) // FGATE_SLOT
    for h in range(FOX_HEADS):
        pair, e = divmod(h, 2)
        q_h = _head_rows(q_t[pair * V7X_LANES:(pair + 1) * V7X_LANES], e)
        e_h = jnp.where(slot_head == h, eq_t, 0.0)
        rhs_ref[h] = jnp.concatenate([q_h, e_h], axis=0).astype(jnp.bfloat16)

    def lhs_tile(j, pair):
        rows = _key_rows(j)
        return jnp.concatenate([k_ref[0, rows, pair * V7X_LANES:(pair + 1) * V7X_LANES], ek_ref[0, rows, :]], axis=1)

    def bias_fn(j, h, first):
        return _causal_bias_t() if first else None

    _flash_tiles(i, FOX_HEADS, lhs_tile, bias_fn, rhs_ref, vt_ref, m_ref, l_ref, acc_ref)
    _finish(FOX_HEADS, l_ref, acc_ref, g_ref, o_ref)


def _moba_kernel(q_ref, k_ref, v_ref, kmean_ref, bias_ref, g_ref, o_ref, vt_ref, rhs_ref, m_ref, l_ref, acc_ref):
    i = pl.program_id(1)
    f32, bf16 = jnp.float32, jnp.bfloat16
    _stage_values(i, v_ref, vt_ref)
    q_t = q_ref[0].astype(f32).T
    n_blocks = kmean_ref.shape[1]
    sel_rows = 16
    blk = lax.broadcasted_iota(jnp.int32, (sel_rows, TQ), 0)
    lane = lax.broadcasted_iota(jnp.int32, (1, V7X_LANES), 1)
    for h in range(MOBA_HEADS):
        pair, e = divmod(h, 2)
        q_h = _head_rows(q_t[pair * V7X_LANES:(pair + 1) * V7X_LANES], e)
        q_hb = q_h.astype(bf16)
        kmean = kmean_ref[0, :, pair * V7X_LANES:(pair + 1) * V7X_LANES]
        km_hi = kmean.astype(bf16).astype(f32)
        gate_lhs = jnp.concatenate([jnp.concatenate([km_hi, kmean - km_hi], axis=1),
                                    jnp.zeros((sel_rows - n_blocks, 2 * V7X_LANES), f32)], axis=0).astype(bf16)
        gate = _dot(gate_lhs, jnp.concatenate([q_hb, q_hb], axis=0))
        beaten = jnp.zeros((sel_rows, TQ), f32)
        for mblk in range(n_blocks - 1):
            gm = gate[mblk:mblk + 1, :]
            wins = (gm > gate) | ((gm == gate) & (mblk < blk))
            beaten = beaten + jnp.where(wins, jnp.where(mblk < i, 1.0, 0.0), 0.0)
        keep = ((blk < i) & (beaten < MOBA_TOPK)) | (blk == i)
        sel = jnp.where(keep, 0.0, MASK_NEG)
        pad = jnp.zeros((V7X_LANES - sel_rows, TQ), f32)
        rhs_ref[h] = jnp.concatenate([q_h, sel, pad], axis=0).astype(bf16)

    def lhs_tile(j, pair):
        onehot = jnp.broadcast_to(jnp.where(lane == j, 1.0, 0.0).astype(bf16), (TQ, V7X_LANES))
        return jnp.concatenate([k_ref[0, _key_rows(j), pair * V7X_LANES:(pair + 1) * V7X_LANES], onehot], axis=1)

    def bias_fn(j, h, first):
        return bias_ref[h, jnp.minimum(i - j, 2)]

    _flash_tiles(i, MOBA_HEADS, lhs_tile, bias_fn, rhs_ref, vt_ref, m_ref, l_ref, acc_ref)
    _finish(MOBA_HEADS, l_ref, acc_ref, g_ref, o_ref)


def _mla_kernel(qn_ref, qr_ref, kn_ref, kr_ref, v_ref, g_ref, o_ref, vt_ref, rhs_ref, m_ref, l_ref, acc_ref):
    i = pl.program_id(1)
    _stage_values(i, v_ref, vt_ref)
    qn_t = qn_ref[0].astype(jnp.float32).T
    qr_t = qr_ref[0].astype(jnp.float32).T
    heads_per_rot = V7X_LANES // MLA_ROPE_DIM
    rot_slot = lax.broadcasted_iota(jnp.int32, (V7X_LANES, TQ), 0) // MLA_ROPE_DIM
    for h in range(MLA_HEADS):
        pair, e = divmod(h, 2)
        quad, slot = divmod(h, heads_per_rot)
        q_h = _head_rows(qn_t[pair * V7X_LANES:(pair + 1) * V7X_LANES], e)
        r_h = jnp.where(rot_slot == slot, qr_t[quad * V7X_LANES:(quad + 1) * V7X_LANES], 0.0)
        rhs_ref[h] = jnp.concatenate([q_h, r_h], axis=0).astype(jnp.bfloat16)

    def lhs_tile(j, pair):
        rows = _key_rows(j)
        return jnp.concatenate([kn_ref[0, rows, pair * V7X_LANES:(pair + 1) * V7X_LANES], kr_ref[0, rows, :]], axis=1)

    def bias_fn(j, h, first):
        return _causal_bias_t() if first else None

    _flash_tiles(i, MLA_HEADS, lhs_tile, bias_fn, rhs_ref, vt_ref, m_ref, l_ref, acc_ref)
    _finish(MLA_HEADS, l_ref, acc_ref, g_ref, o_ref)


def _attn_call(kernel, name, n_heads, q_arrays, kv_arrays, const_arrays):
    batch, seq, _ = q_arrays[0].shape
    out_width = n_heads * HEAD_LANES
    in_specs = ([pl.BlockSpec((1, TQ, a.shape[2]), lambda b, i: (b, i, 0)) for a in q_arrays]
                + [pl.BlockSpec((1,) + a.shape[1:], lambda b, i: (b, 0, 0)) for a in kv_arrays]
                + [_const_spec(a.shape) for a in const_arrays])
    return pl.pallas_call(
        kernel,
        out_shape=jax.ShapeDtypeStruct((batch, seq, out_width), jnp.bfloat16),
        grid=(batch, seq // TQ),
        in_specs=in_specs,
        out_specs=pl.BlockSpec((1, TQ, out_width), lambda b, i: (b, i, 0)),
        scratch_shapes=[
            pltpu.VMEM((seq // TQ, out_width, TQ), jnp.bfloat16),
            pltpu.VMEM((n_heads, 2 * V7X_LANES, TQ), jnp.bfloat16),
            pltpu.VMEM((n_heads, 1, TQ), jnp.float32),
            pltpu.VMEM((n_heads, 1, TQ), jnp.float32),
            pltpu.VMEM((n_heads, HEAD_LANES, TQ), jnp.float32),
        ],
        compiler_params=_params("arbitrary", "arbitrary"),
        name=name,
    )(*q_arrays, *kv_arrays, *const_arrays)


def _bias_tile_kernel(table_ref, bucket_ref, o_ref):
    h = pl.program_id(0)
    for k in range(bucket_ref.shape[0]):
        bkt = bucket_ref[k]
        tile = jnp.full(bkt.shape, MASK_NEG, jnp.float32)
        for b in range(T5_BUCKETS):
            tile = jnp.where(bkt == b, table_ref[h, b], tile)
        o_ref[0, k] = tile


def _moba_bias_tiles(t5_table):
    bucket = _t5_bucket_table(3 * TQ)
    key = np.arange(TQ)[:, None]
    qry = np.arange(TQ)[None, :]
    kinds = []
    for k in range(3):
        dist = qry - key + k * TQ
        kinds.append(np.where(dist >= 0, bucket[np.maximum(dist, 0)], -1))
    buckets = jnp.asarray(np.stack(kinds), jnp.int32)
    return pl.pallas_call(
        _bias_tile_kernel,
        out_shape=jax.ShapeDtypeStruct((MOBA_HEADS, 3, TQ, TQ), jnp.float32),
        grid=(MOBA_HEADS,),
        in_specs=[pl.BlockSpec(memory_space=pltpu.SMEM), _const_spec(buckets.shape)],
        out_specs=pl.BlockSpec((1, 3, TQ, TQ), lambda h: (h, 0, 0, 0)),
        compiler_params=_params("arbitrary"),
        name="t5_bias_tiles",
    )(t5_table.T, buckets)


def _out_kernel(x_ref, of_ref, om_ref, oc_ref, mod_ref, gpost_ref, w_ref, o_ref):
    o = jnp.concatenate([of_ref[0], om_ref[0], oc_ref[0]], axis=1)
    y = _dot(o, w_ref[...])
    gate = mod_ref[0, :, 2 * D_MODEL:3 * D_MODEL]
    o_ref[0] = x_ref[0] + gate * _rms(y, gpost_ref[...])


def _out_proj(x, o_f, o_m, o_c, mod_l, gpost, w_out):
    batch, seq, d = x.shape

    def tok(width):
        return pl.BlockSpec((1, TM, width), lambda b, t: (b, t, 0))

    return pl.pallas_call(
        _out_kernel,
        out_shape=jax.ShapeDtypeStruct(x.shape, x.dtype),
        grid=(batch, seq // TM),
        in_specs=[tok(d), tok(FOX_W), tok(MOBA_W), tok(MLA_W),
                  pl.BlockSpec((1, 1, 6 * d), lambda b, t: (b, 0, 0)),
                  _const_spec((1, d)), _const_spec(w_out.shape)],
        out_specs=tok(d),
        compiler_params=_params("arbitrary", "arbitrary"),
        name="out_proj",
    )(x, o_f, o_m, o_c, mod_l, gpost, w_out)


def _ffn_kernel(x_ref, mod_ref, gpre_ref, gpost_ref, wgu_ref, wd_ref, o_ref):
    bf16 = jnp.bfloat16
    x = x_ref[0]
    shift = mod_ref[0, :, 3 * D_MODEL:4 * D_MODEL]
    scale = mod_ref[0, :, 4 * D_MODEL:5 * D_MODEL]
    gate = mod_ref[0, :, 5 * D_MODEL:6 * D_MODEL]
    h = (_rms(x, gpre_ref[...]) * (1.0 + scale) + shift).astype(bf16)
    acc = None
    for c0, c1 in FFN_CHUNKS:
        g = _dot(h, wgu_ref[:, c0:c1])
        u = _dot(h, wgu_ref[:, D_FF + c0:D_FF + c1])
        a = (g * jax.nn.sigmoid(g) * u).astype(bf16)
        part = _dot(a, wd_ref[c0:c1, :])
        acc = part if acc is None else acc + part
    o_ref[0] = x + gate * _rms(acc, gpost_ref[...])


def _ffn(x, mod_l, gpre, gpost, wgu, wd):
    batch, seq, d = x.shape
    tok = pl.BlockSpec((1, TM, d), lambda b, t: (b, t, 0))
    return pl.pallas_call(
        _ffn_kernel,
        out_shape=jax.ShapeDtypeStruct(x.shape, x.dtype),
        grid=(batch, seq // TM),
        in_specs=[tok, pl.BlockSpec((1, 1, 6 * d), lambda b, t: (b, 0, 0)),
                  _const_spec((1, d)), _const_spec((1, d)), _const_spec(wgu.shape), _const_spec(wd.shape)],
        out_specs=tok,
        compiler_params=_params("arbitrary", "arbitrary"),
        name="ffn",
    )(x, mod_l, gpre, gpost, wgu, wd)


def _rope_tables(seq):
    half = MLA_ROPE_DIM // 2
    inv_freq = 1.0 / (ROPE_THETA ** (jnp.arange(half, dtype=jnp.float32) / half))
    ang = jnp.arange(seq).astype(jnp.float32)[:, None] * inv_freq[None, :]
    reps = V7X_LANES // MLA_ROPE_DIM
    cos = jnp.tile(jnp.concatenate([jnp.cos(ang), jnp.cos(ang)], axis=1), (1, reps))
    sin = jnp.tile(jnp.concatenate([-jnp.sin(ang), jnp.sin(ang)], axis=1), (1, reps))
    return cos, sin


def kernel(x, c, t5_table, w_ada, b_ada, g_mix_pre, g_mix_post, w_in, b_forget, g_q_lat, w_uq, g_kv_lat, w_ukv,
           g_group, w_out, g_ffn_pre, g_ffn_post, w_gate_up, w_down):
    batch, seq, d = x.shape
    assert d == D_MODEL and seq % TM == 0 and TM % MOBA_BLOCK == 0 and TQ == MOBA_BLOCK
    bf16 = jnp.bfloat16
    in_cols, uq_cols, ukv_cols = _in_proj_columns(), _uq_columns(), _ukv_columns()
    cos_t, sin_t = _rope_tables(seq)
    tril = np.tril(np.ones((TM, TM), np.float32))
    tril = jnp.asarray(np.concatenate([tril] * N_SPLIT, axis=1), bf16)
    moba_bias = _moba_bias_tiles(t5_table)
    fg_lane = np.arange(V7X_LANES)
    fg_used = (fg_lane < FGATE_SLOT * FOX_HEADS) & (fg_lane % FGATE_SLOT < 2 * N_SPLIT)
    fg_head = np.minimum(fg_lane // FGATE_SLOT, FOX_HEADS - 1)

    mod = _ada_mod(c, w_ada, b_ada)
    for l in range(DEPTH):
        mod_l = mod[l].reshape(batch, 1, 6 * d)
        fbias = jnp.where(jnp.asarray(fg_used), b_forget[l][fg_head], 0.0).reshape(1, V7X_LANES)
        pr = _proj(x, mod_l, g_mix_pre[l].reshape(1, d),
                   _take_columns(w_in[l], in_cols).astype(bf16), fbias,
                   g_q_lat[l].reshape(1, -1), _take_columns(w_uq[l], uq_cols).astype(bf16),
                   g_kv_lat[l].reshape(1, -1), _take_columns(w_ukv[l], ukv_cols).astype(bf16),
                   cos_t, sin_t, tril)
        g_a = g_group[l, :FOX_W].reshape(1, -1)
        g_b = g_group[l, FOX_W:FOX_W + MOBA_W].reshape(1, -1)
        g_c = g_group[l, FOX_W + MOBA_W:].reshape(1, -1)
        o_f = _attn_call(_fox_kernel, "fox_attn", FOX_HEADS, [pr["qf"], pr["eq"]], [pr["kf"], pr["ek"], pr["vf"]], [g_a])
        o_m = _attn_call(_moba_kernel, "moba_attn", MOBA_HEADS, [pr["qm"]], [pr["km"], pr["vm"], pr["kmean"]],
                         [moba_bias, g_b])
        o_c = _attn_call(_mla_kernel, "mla_attn", MLA_HEADS, [pr["qn"], pr["qr"]], [pr["kn"], pr["kr"], pr["vc"]], [g_c])
        x = _out_proj(x, o_f, o_m, o_c, mod_l, g_mix_post[l].reshape(1, d), w_out[l].astype(bf16))
        x = _ffn(x, mod_l, g_ffn_pre[l].reshape(1, d), g_ffn_post[l].reshape(1, d),
                 w_gate_up[l].astype(bf16), w_down[l].astype(bf16))
    return x
```

```python
import functools
import math
from typing import NamedTuple

import jax
import jax.numpy as jnp
import numpy as np
from jax import lax
from jax.experimental import pallas as pl
from jax.experimental.pallas import tpu as pltpu

D_MODEL = 1024
DEPTH = 2
FOX_HEADS = 4
FOX_HEAD_DIM = 64
MOBA_HEADS = 4
MOBA_HEAD_DIM = 64
MOBA_BLOCK = 256
MOBA_TOPK = 3
MLA_HEADS = 8
MLA_NOPE_DIM = 64
MLA_ROPE_DIM = 32
MLA_V_DIM = 64
MLA_Q_RANK = 256
MLA_KV_RANK = 128
ROPE_THETA = 10000.0
T5_BUCKETS = 32
T5_MAX_DISTANCE = 128
D_FF = -(-8 * D_MODEL // (3 * 256)) * 256
RMS_EPS = 1e-6
FOX_W = FOX_HEADS * FOX_HEAD_DIM
MOBA_W = MOBA_HEADS * MOBA_HEAD_DIM
MLA_W = MLA_HEADS * MLA_V_DIM
MIX_WIDTH = FOX_W + MOBA_W + MLA_W
IN_SIZES = (FOX_W, FOX_W, FOX_W, FOX_HEADS, MOBA_W, MOBA_W, MOBA_W, MLA_Q_RANK, MLA_KV_RANK, MLA_ROPE_DIM)

V7X_LANES = 128
V7X_VMEM_LIMIT_BYTES = 56 * 1024 * 1024

TM = 512
TQ = 256
PIPE_HEADS = 4
FFN_CHUNKS = ((0, 1024), (1024, 2048), (2048, D_FF))

HEAD_LANES = 64
MASK_NEG = -1e30
FGATE_SLOT = 8
N_SPLIT = 3

_C_QF, _C_KF, _C_VF = 0, 256, 512
_C_QM, _C_KM, _C_VM = 768, 1024, 1280
_C_CQ, _C_CKV = 1536, 1792
_C_FG, _C_KR, _C_KRS = 1920, 2048, 2176
IN_WIDTH_PADDED = 2304


def _in_proj_columns():
    off = np.cumsum((0,) + IN_SIZES)
    q_f, k_f, v_f, f_g, q_m, k_m, v_m, c_q, c_kv, k_r = (np.arange(off[i], off[i + 1]) for i in range(10))
    fg = np.full((V7X_LANES,), -1, np.int64)
    for h in range(FOX_HEADS):
        fg[FGATE_SLOT * h:FGATE_SLOT * h + 2 * N_SPLIT] = f_g[h]
    half = MLA_ROPE_DIM // 2
    kr4 = np.tile(k_r, V7X_LANES // MLA_ROPE_DIM)
    kr4s = np.tile(np.roll(k_r, -half), V7X_LANES // MLA_ROPE_DIM)
    cols = np.concatenate([q_f, k_f, v_f, q_m, k_m, v_m, c_q, c_kv, fg, kr4, kr4s])
    assert cols.shape == (IN_WIDTH_PADDED,)
    return cols


def _uq_columns():
    per = MLA_NOPE_DIM + MLA_ROPE_DIM
    half = MLA_ROPE_DIM // 2
    nope = np.concatenate([np.arange(h * per, h * per + MLA_NOPE_DIM) for h in range(MLA_HEADS)])
    rot = np.concatenate([np.arange(h * per + MLA_NOPE_DIM, (h + 1) * per) for h in range(MLA_HEADS)])
    rots = np.concatenate([np.roll(np.arange(h * per + MLA_NOPE_DIM, (h + 1) * per), -half) for h in range(MLA_HEADS)])
    return np.concatenate([nope, rot, rots])


def _ukv_columns():
    per = MLA_NOPE_DIM + MLA_V_DIM
    nope = np.concatenate([np.arange(h * per, h * per + MLA_NOPE_DIM) for h in range(MLA_HEADS)])
    val = np.concatenate([np.arange(h * per + MLA_NOPE_DIM, (h + 1) * per) for h in range(MLA_HEADS)])
    return np.concatenate([nope, val])


def _take_columns(w, cols):
    g = jnp.take(w, jnp.asarray(np.maximum(cols, 0), jnp.int32), axis=1)
    return jnp.where(jnp.asarray(cols >= 0)[None, :], g, 0.0)


def _t5_bucket_table(n):
    d = np.arange(n, dtype=np.int32)
    max_exact = T5_BUCKETS // 2
    nf = np.maximum(d, max_exact).astype(np.float32)
    ratio = np.log(nf / np.float32(max_exact)) / np.float32(math.log(T5_MAX_DISTANCE / max_exact))
    large = max_exact + (ratio.astype(np.float32) * np.float32(T5_BUCKETS - max_exact)).astype(np.int32)
    large = np.minimum(large, T5_BUCKETS - 1)
    return np.where(d < max_exact, d, large).astype(np.int32)


def _const_spec(shape):
    nd = len(shape)
    return pl.BlockSpec(shape, lambda *_: (0,) * nd, pipeline_mode=pl.Buffered(1))


def _params(*sem):
    return pltpu.CompilerParams(dimension_semantics=sem, vmem_limit_bytes=V7X_VMEM_LIMIT_BYTES)


def _rms(x, g):
    return x * lax.rsqrt(jnp.mean(x * x, axis=-1, keepdims=True) + RMS_EPS) * g


def _split3(v):
    hi = v.astype(jnp.bfloat16)
    r1 = v - hi.astype(jnp.float32)
    mid = r1.astype(jnp.bfloat16)
    lo = (r1 - mid.astype(jnp.float32)).astype(jnp.bfloat16)
    return hi, mid, lo


def _dot(a, b):
    return jnp.dot(a, b, preferred_element_type=jnp.float32)


def _ada_kernel(c_ref, w_ref, b_ref, o_ref):
    c = c_ref[...]
    act = (c * jax.nn.sigmoid(c)).astype(jnp.bfloat16)
    o_ref[0] = _dot(act, w_ref[0].astype(jnp.bfloat16)) + b_ref[0]


def _ada_mod(c, w_ada, b_ada):
    depth, d, six_d = w_ada.shape
    batch = c.shape[0]
    n_col = six_d // d
    return pl.pallas_call(
        _ada_kernel,
        out_shape=jax.ShapeDtypeStruct((depth, batch, six_d), jnp.float32),
        grid=(depth, n_col),
        in_specs=[
            pl.BlockSpec((batch, d), lambda l, j: (0, 0)),
            pl.BlockSpec((1, d, d), lambda l, j: (l, 0, j)),
            pl.BlockSpec((1, 1, d), lambda l, j: (l, 0, j)),
        ],
        out_specs=pl.BlockSpec((1, batch, d), lambda l, j: (l, 0, j)),
        compiler_params=_params("arbitrary", "arbitrary"),
        name="ada_mod",
    )(c, w_ada, b_ada.reshape(depth, 1, six_d))


def _proj_kernel(x_ref, mod_ref, gpre_ref, win_ref, fb_ref, gq_ref, wuq_ref, gkv_ref, wukv_ref,
                 cos_ref, sin_ref, tril_ref,
                 qf_ref, eq_ref, kf_ref, ek_ref, vf_ref, qm_ref, km_ref, vm_ref, kmean_ref,
                 qn_ref, qr_ref, kn_ref, kr_ref, vc_ref, carry_ref):
    t = pl.program_id(1)
    bf16 = jnp.bfloat16
    x = x_ref[0]
    shift = mod_ref[0, :, 0:D_MODEL]
    scale = mod_ref[0, :, D_MODEL:2 * D_MODEL]
    h = (_rms(x, gpre_ref[...]) * (1.0 + scale) + shift).astype(bf16)

    def seg(c0, width):
        return _dot(h, win_ref[:, c0:c0 + width])

    qf_ref[0] = (seg(_C_QF, FOX_W) * FOX_HEAD_DIM ** -0.5).astype(bf16)
    kf_ref[0] = seg(_C_KF, FOX_W).astype(bf16)
    vf_ref[0] = seg(_C_VF, FOX_W).astype(bf16)
    qm_ref[0] = (seg(_C_QM, MOBA_W) * MOBA_HEAD_DIM ** -0.5).astype(bf16)
    km = seg(_C_KM, MOBA_W)
    km_ref[0] = km.astype(bf16)
    vm_ref[0] = seg(_C_VM, MOBA_W).astype(bf16)
    kmean_ref[0, 0] = jnp.mean(km.reshape(TM // MOBA_BLOCK, MOBA_BLOCK, MOBA_W), axis=1)

    lane = lax.broadcasted_iota(jnp.int32, (1, V7X_LANES), 1)
    slot = lane % FGATE_SLOT
    used = (lane < FGATE_SLOT * FOX_HEADS) & (slot < 2 * N_SPLIT)
    fl = seg(_C_FG, V7X_LANES) + fb_ref[...]
    logf = jnp.where(used, jnp.minimum(fl, 0.0) - jnp.log1p(jnp.exp(-jnp.abs(fl))), 0.0)

    @pl.when(t == 0)
    def _():
        carry_ref[...] = jnp.zeros_like(carry_ref)

    fcum = _dot(tril_ref[...], jnp.concatenate(_split3(logf), axis=0)) + carry_ref[0:1, :]
    carry_ref[0:1, :] = fcum[TM - 1:TM, :]
    hi, mid, lo = (p.astype(jnp.float32) for p in _split3(fcum))
    parts = jnp.where(slot % N_SPLIT == 0, hi, jnp.where(slot % N_SPLIT == 1, mid, lo))
    eq_ref[0] = jnp.where(used, jnp.where(slot < N_SPLIT, parts, 1.0), 0.0).astype(bf16)
    ek_ref[0] = jnp.where(used, jnp.where(slot < N_SPLIT, 1.0, -parts), 0.0).astype(bf16)

    cos = cos_ref[...]
    sin = sin_ref[...]
    mla_scale = (MLA_NOPE_DIM + MLA_ROPE_DIM) ** -0.5
    cq = _rms(seg(_C_CQ, MLA_Q_RANK), gq_ref[...]).astype(bf16)
    n_nope = MLA_HEADS * MLA_NOPE_DIM
    n_rot = MLA_HEADS * MLA_ROPE_DIM
    qn_ref[0] = (_dot(cq, wuq_ref[:, 0:n_nope]) * mla_scale).astype(bf16)
    q_rot = _dot(cq, wuq_ref[:, n_nope:n_nope + n_rot])
    q_rot_sw = _dot(cq, wuq_ref[:, n_nope + n_rot:n_nope + 2 * n_rot])
    cos2 = jnp.concatenate([cos] * (n_rot // V7X_LANES), axis=1)
    sin2 = jnp.concatenate([sin] * (n_rot // V7X_LANES), axis=1)
    qr_ref[0] = ((q_rot * cos2 + q_rot_sw * sin2) * mla_scale).astype(bf16)
    ckv = _rms(seg(_C_CKV, MLA_KV_RANK), gkv_ref[...]).astype(bf16)
    kn_ref[0] = _dot(ckv, wukv_ref[:, 0:n_nope]).astype(bf16)
    vc_ref[0] = _dot(ckv, wukv_ref[:, n_nope:n_nope + MLA_W]).astype(bf16)
    kr_ref[0] = (seg(_C_KR, V7X_LANES) * cos + seg(_C_KRS, V7X_LANES) * sin).astype(bf16)


def _proj(x, mod_l, gpre, win, fbias, gq, wuq, gkv, wukv, cos_t, sin_t, tril):
    batch, seq, d = x.shape
    nt = seq // TM
    bf16 = jnp.bfloat16

    def tok(width):
        return pl.BlockSpec((1, TM, width), lambda b, t: (b, t, 0))

    def out(width):
        return jax.ShapeDtypeStruct((batch, seq, width), bf16)

    widths = dict(qf=FOX_W, eq=V7X_LANES, kf=FOX_W, ek=V7X_LANES, vf=FOX_W, qm=MOBA_W, km=MOBA_W, vm=MOBA_W)
    mla_widths = dict(qn=MLA_HEADS * MLA_NOPE_DIM, qr=MLA_HEADS * MLA_ROPE_DIM, kn=MLA_HEADS * MLA_NOPE_DIM,
                      kr=V7X_LANES, vc=MLA_W)
    nb = TM // MOBA_BLOCK
    out_shape = ([out(w) for w in widths.values()]
                 + [jax.ShapeDtypeStruct((batch, nt, nb, MOBA_W), jnp.float32)]
                 + [out(w) for w in mla_widths.values()])
    out_specs = ([tok(w) for w in widths.values()]
                 + [pl.BlockSpec((1, 1, nb, MOBA_W), lambda b, t: (b, t, 0, 0))]
                 + [tok(w) for w in mla_widths.values()])
    res = pl.pallas_call(
        _proj_kernel,
        out_shape=out_shape,
        grid=(batch, nt),
        in_specs=[
            tok(d),
            pl.BlockSpec((1, 1, 6 * d), lambda b, t: (b, 0, 0)),
            _const_spec((1, d)),
            _const_spec(win.shape),
            _const_spec((1, V7X_LANES)),
            _const_spec((1, MLA_Q_RANK)),
            _const_spec(wuq.shape),
            _const_spec((1, MLA_KV_RANK)),
            _const_spec(wukv.shape),
            pl.BlockSpec((TM, V7X_LANES), lambda b, t: (t, 0)),
            pl.BlockSpec((TM, V7X_LANES), lambda b, t: (t, 0)),
            _const_spec(tril.shape),
        ],
        out_specs=out_specs,
        scratch_shapes=[pltpu.VMEM((8, V7X_LANES), jnp.float32)],
        compiler_params=_params("arbitrary", "arbitrary"),
        name="in_proj",
    )(x, mod_l, gpre, win, fbias, gq, wuq, gkv, wukv, cos_t, sin_t, tril)
    names = list(widths) + ["kmean"] + list(mla_widths)
    r = dict(zip(names, res))
    r["kmean"] = r["kmean"].reshape(batch, seq // MOBA_BLOCK, MOBA_W)
    return r


def _transpose_to_bf16(x):
    return x.astype(jnp.float32).T.astype(jnp.bfloat16)


def _stage_values(i, v_ref, vt_ref):
    @pl.when(i == 0)
    def _():
        for c in range(vt_ref.shape[0]):
            vt_ref[c] = _transpose_to_bf16(v_ref[0, c * TQ:(c + 1) * TQ, :])


def _head_rows(x_pair, e):
    row = lax.broadcasted_iota(jnp.int32, x_pair.shape, 0)
    keep = (row >= HEAD_LANES) if e else (row < HEAD_LANES)
    return jnp.where(keep, x_pair, 0.0)


def _causal_bias_t():
    key = lax.broadcasted_iota(jnp.int32, (TQ, TQ), 0)
    qry = lax.broadcasted_iota(jnp.int32, (TQ, TQ), 1)
    return jnp.where(key <= qry, 0.0, MASK_NEG)


class _FlashScratch(NamedTuple):
    vt: object
    rhs: object
    s: object
    p: object
    alpha: object
    m: object
    l: object
    acc: object


def _flash_tiles(i, heads, lhs_tile, bias_fn, sc):
    def scores(k, first=False):
        j = i - k
        lhs = {h // 2: lhs_tile(j, h // 2) for h in heads if h % 2 == 0}
        out = {}
        for h in heads:
            s = _dot(lhs[h // 2], sc.rhs[h])
            b = bias_fn(j, h, first)
            out[h] = s if b is None else s + b
        return out

    def store_scores(vals):
        for h in heads:
            sc.s[h] = vals[h]

    def softmax():
        s_val = {h: sc.s[h] for h in heads}
        m_old = {h: sc.m[h] for h in heads}
        m_new = {h: jnp.maximum(m_old[h], jnp.max(s_val[h], axis=0, keepdims=True)) for h in heads}
        alpha = {h: jnp.exp(m_old[h] - m_new[h]) for h in heads}
        probs = {h: jnp.exp(s_val[h] - m_new[h]) for h in heads}
        for h in heads:
            sc.l[h] = alpha[h] * sc.l[h] + jnp.sum(probs[h], axis=0, keepdims=True)
            sc.m[h] = m_new[h]
            sc.alpha[h] = alpha[h]
            sc.p[h] = probs[h].astype(jnp.bfloat16)

    def values(k):
        j = jnp.minimum(i - k, i)
        pv = {h: _dot(sc.vt[j, h * HEAD_LANES:(h + 1) * HEAD_LANES, :], sc.p[h]) for h in heads}
        for h in heads:
            sc.acc[h] = sc.alpha[h] * sc.acc[h] + pv[h]

    for h in heads:
        sc.m[h] = jnp.full(sc.m.shape[1:], MASK_NEG, jnp.float32)
        sc.l[h] = jnp.zeros(sc.l.shape[1:], jnp.float32)
        sc.alpha[h] = jnp.zeros(sc.alpha.shape[1:], jnp.float32)
        sc.acc[h] = jnp.zeros(sc.acc.shape[1:], jnp.float32)
        sc.p[h] = jnp.zeros(sc.p.shape[1:], jnp.bfloat16)
    store_scores(scores(0, first=True))

    def body(k, carry):
        nxt = scores(k + 1)
        values(k - 1)
        softmax()
        store_scores(nxt)
        return carry

    lax.fori_loop(0, i, body, 0)
    values(i - 1)
    softmax()
    values(i)


def _finish(n_heads, sc, g_ref, o_ref):
    o_t = jnp.concatenate([sc.acc[h] * (1.0 / sc.l[h]) for h in range(n_heads)], axis=0)
    o_ref[0] = _rms(o_t.T, g_ref[...]).astype(o_ref.dtype)


def _key_rows(j):
    return pl.ds(pl.multiple_of(j * TQ, TQ), TQ)


def _fox_kernel(q_ref, eq_ref, k_ref, ek_ref, v_ref, g_ref, o_ref, *scratch):
    sc = _FlashScratch(*scratch)
    i = pl.program_id(1)
    _stage_values(i, v_ref, sc.vt)
    q_t = q_ref[0].astype(jnp.float32).T
    eq_t = eq_ref[0].astype(jnp.float32).T
    slot_head = lax.broadcasted_iota(jnp.int32, eq_t.shape, 0) // FGATE_SLOT
    for h in range(FOX_HEADS):
        pair, e = divmod(h, 2)
        q_h = _head_rows(q_t[pair * V7X_LANES:(pair + 1) * V7X_LANES], e)
        e_h = jnp.where(slot_head == h, eq_t, 0.0)
        sc.rhs[h] =jnp.concatenate([q_h, e_h], axis=0).astype(jnp.bfloat16)

    def lhs_tile(j, pair):
        rows = _key_rows(j)
        return jnp.concatenate([k_ref[0, rows, pair * V7X_LANES:(pair + 1) * V7X_LANES], ek_ref[0, rows, :]], axis=1)

    def bias_fn(j, h, first):
        return _causal_bias_t() if first else None

    _flash_tiles(i, range(FOX_HEADS), lhs_tile, bias_fn, sc)
    _finish(FOX_HEADS, sc, g_ref, o_ref)


def _moba_kernel(q_ref, k_ref, v_ref, kmean_ref, bias_ref, g_ref, o_ref, *scratch):
    sc = _FlashScratch(*scratch)
    i = pl.program_id(1)
    f32, bf16 = jnp.float32, jnp.bfloat16
    _stage_values(i, v_ref, sc.vt)
    q_t = q_ref[0].astype(f32).T
    n_blocks = kmean_ref.shape[1]
    sel_rows = 16
    blk = lax.broadcasted_iota(jnp.int32, (sel_rows, TQ), 0)
    lane = lax.broadcasted_iota(jnp.int32, (1, V7X_LANES), 1)
    for h in range(MOBA_HEADS):
        pair, e = divmod(h, 2)
        q_h = _head_rows(q_t[pair * V7X_LANES:(pair + 1) * V7X_LANES], e)
        q_hb = q_h.astype(bf16)
        kmean = kmean_ref[0, :, pair * V7X_LANES:(pair + 1) * V7X_LANES]
        km_hi = kmean.astype(bf16).astype(f32)
        gate_lhs = jnp.concatenate([jnp.concatenate([km_hi, kmean - km_hi], axis=1),
                                    jnp.zeros((sel_rows - n_blocks, 2 * V7X_LANES), f32)], axis=0).astype(bf16)
        gate = _dot(gate_lhs, jnp.concatenate([q_hb, q_hb], axis=0))
        beaten = jnp.zeros((sel_rows, TQ), f32)
        for mblk in range(n_blocks - 1):
            gm = gate[mblk:mblk + 1, :]
            wins = (gm > gate) | ((gm == gate) & (mblk < blk))
            beaten = beaten + jnp.where(wins, jnp.where(mblk < i, 1.0, 0.0), 0.0)
        keep = ((blk < i) & (beaten < MOBA_TOPK)) | (blk == i)
        sel = jnp.where(keep, 0.0, MASK_NEG)
        pad = jnp.zeros((V7X_LANES - sel_rows, TQ), f32)
        sc.rhs[h] =jnp.concatenate([q_h, sel, pad], axis=0).astype(bf16)

    def lhs_tile(j, pair):
        onehot = jnp.broadcast_to(jnp.where(lane == j, 1.0, 0.0).astype(bf16), (TQ, V7X_LANES))
        return jnp.concatenate([k_ref[0, _key_rows(j), pair * V7X_LANES:(pair + 1) * V7X_LANES], onehot], axis=1)

    def bias_fn(j, h, first):
        return bias_ref[h, jnp.minimum(i - j, 2)]

    _flash_tiles(i, range(MOBA_HEADS), lhs_tile, bias_fn, sc)
    _finish(MOBA_HEADS, sc, g_ref, o_ref)


def _mla_kernel(qn_ref, qr_ref, kn_ref, kr_ref, v_ref, g_ref, o_ref, *scratch):
    sc = _FlashScratch(*scratch)
    i = pl.program_id(1)
    _stage_values(i, v_ref, sc.vt)
    qn_t = qn_ref[0].astype(jnp.float32).T
    qr_t = qr_ref[0].astype(jnp.float32).T
    heads_per_rot = V7X_LANES // MLA_ROPE_DIM
    rot_slot = lax.broadcasted_iota(jnp.int32, (V7X_LANES, TQ), 0) // MLA_ROPE_DIM
    for h in range(MLA_HEADS):
        pair, e = divmod(h, 2)
        quad, slot = divmod(h, heads_per_rot)
        q_h = _head_rows(qn_t[pair * V7X_LANES:(pair + 1) * V7X_LANES], e)
        r_h = jnp.where(rot_slot == slot, qr_t[quad * V7X_LANES:(quad + 1) * V7X_LANES], 0.0)
        sc.rhs[h] =jnp.concatenate([q_h, r_h], axis=0).astype(jnp.bfloat16)

    def lhs_tile(j, pair):
        rows = _key_rows(j)
        return jnp.concatenate([kn_ref[0, rows, pair * V7X_LANES:(pair + 1) * V7X_LANES], kr_ref[0, rows, :]], axis=1)

    def bias_fn(j, h, first):
        return _causal_bias_t() if first else None

    for h0 in range(0, MLA_HEADS, PIPE_HEADS):
        _flash_tiles(i, range(h0, h0 + PIPE_HEADS), lhs_tile, bias_fn, sc)
    _finish(MLA_HEADS, sc, g_ref, o_ref)


def _attn_call(kernel, name, n_heads, q_arrays, kv_arrays, const_arrays):
    batch, seq, _ = q_arrays[0].shape
    out_width = n_heads * HEAD_LANES
    in_specs = ([pl.BlockSpec((1, TQ, a.shape[2]), lambda b, i: (b, i, 0)) for a in q_arrays]
                + [pl.BlockSpec((1,) + a.shape[1:], lambda b, i: (b, 0, 0)) for a in kv_arrays]
                + [_const_spec(a.shape) for a in const_arrays])
    return pl.pallas_call(
        kernel,
        out_shape=jax.ShapeDtypeStruct((batch, seq, out_width), jnp.bfloat16),
        grid=(batch, seq // TQ),
        in_specs=in_specs,
        out_specs=pl.BlockSpec((1, TQ, out_width), lambda b, i: (b, i, 0)),
        scratch_shapes=[
            pltpu.VMEM((seq // TQ, out_width, TQ), jnp.bfloat16),
            pltpu.VMEM((n_heads, 2 * V7X_LANES, TQ), jnp.bfloat16),
            pltpu.VMEM((n_heads, TQ, TQ), jnp.float32),
            pltpu.VMEM((n_heads, TQ, TQ), jnp.bfloat16),
            pltpu.VMEM((n_heads, 1, TQ), jnp.float32),
            pltpu.VMEM((n_heads, 1, TQ), jnp.float32),
            pltpu.VMEM((n_heads, 1, TQ), jnp.float32),
            pltpu.VMEM((n_heads, HEAD_LANES, TQ), jnp.float32),
        ],
        compiler_params=_params("arbitrary", "arbitrary"),
        name=name,
    )(*q_arrays, *kv_arrays, *const_arrays)


def _bias_tile_kernel(table_ref, bucket_ref, o_ref):
    h = pl.program_id(0)
    for k in range(bucket_ref.shape[0]):
        bkt = bucket_ref[k]
        tile = jnp.full(bkt.shape, MASK_NEG, jnp.float32)
        for b in range(T5_BUCKETS):
            tile = jnp.where(bkt == b, table_ref[h, b], tile)
        o_ref[0, k] = tile


def _moba_bias_tiles(t5_table):
    bucket = _t5_bucket_table(3 * TQ)
    key = np.arange(TQ)[:, None]
    qry = np.arange(TQ)[None, :]
    kinds = []
    for k in range(3):
        dist = qry - key + k * TQ
        kinds.append(np.where(dist >= 0, bucket[np.maximum(dist, 0)], -1))
    buckets = jnp.asarray(np.stack(kinds), jnp.int32)
    return pl.pallas_call(
        _bias_tile_kernel,
        out_shape=jax.ShapeDtypeStruct((MOBA_HEADS, 3, TQ, TQ), jnp.float32),
        grid=(MOBA_HEADS,),
        in_specs=[pl.BlockSpec(memory_space=pltpu.SMEM), _const_spec(buckets.shape)],
        out_specs=pl.BlockSpec((1, 3, TQ, TQ), lambda h: (h, 0, 0, 0)),
        compiler_params=_params("arbitrary"),
        name="t5_bias_tiles",
    )(t5_table.T, buckets)


def _out_kernel(x_ref, of_ref, om_ref, oc_ref, mod_ref, gpost_ref, w_ref, o_ref):
    o = jnp.concatenate([of_ref[0], om_ref[0], oc_ref[0]], axis=1)
    y = _dot(o, w_ref[...])
    gate = mod_ref[0, :, 2 * D_MODEL:3 * D_MODEL]
    o_ref[0] = x_ref[0] + gate * _rms(y, gpost_ref[...])


def _out_proj(x, o_f, o_m, o_c, mod_l, gpost, w_out):
    batch, seq, d = x.shape

    def tok(width):
        return pl.BlockSpec((1, TM, width), lambda b, t: (b, t, 0))

    return pl.pallas_call(
        _out_kernel,
        out_shape=jax.ShapeDtypeStruct(x.shape, x.dtype),
        grid=(batch, seq // TM),
        in_specs=[tok(d), tok(FOX_W), tok(MOBA_W), tok(MLA_W),
                  pl.BlockSpec((1, 1, 6 * d), lambda b, t: (b, 0, 0)),
                  _const_spec((1, d)), _const_spec(w_out.shape)],
        out_specs=tok(d),
        compiler_params=_params("arbitrary", "arbitrary"),
        name="out_proj",
    )(x, o_f, o_m, o_c, mod_l, gpost, w_out)


def _ffn_kernel(x_ref, mod_ref, gpre_ref, gpost_ref, wgu_ref, wd_ref, o_ref):
    bf16 = jnp.bfloat16
    x = x_ref[0]
    shift = mod_ref[0, :, 3 * D_MODEL:4 * D_MODEL]
    scale = mod_ref[0, :, 4 * D_MODEL:5 * D_MODEL]
    gate = mod_ref[0, :, 5 * D_MODEL:6 * D_MODEL]
    h = (_rms(x, gpre_ref[...]) * (1.0 + scale) + shift).astype(bf16)
    acc = None
    for c0, c1 in FFN_CHUNKS:
        g = _dot(h, wgu_ref[:, c0:c1])
        u = _dot(h, wgu_ref[:, D_FF + c0:D_FF + c1])
        a = (g * jax.nn.sigmoid(g) * u).astype(bf16)
        part = _dot(a, wd_ref[c0:c1, :])
        acc = part if acc is None else acc + part
    o_ref[0] = x + gate * _rms(acc, gpost_ref[...])


def _ffn(x, mod_l, gpre, gpost, wgu, wd):
    batch, seq, d = x.shape
    tok = pl.BlockSpec((1, TM, d), lambda b, t: (b, t, 0))
    return pl.pallas_call(
        _ffn_kernel,
        out_shape=jax.ShapeDtypeStruct(x.shape, x.dtype),
        grid=(batch, seq // TM),
        in_specs=[tok, pl.BlockSpec((1, 1, 6 * d), lambda b, t: (b, 0, 0)),
                  _const_spec((1, d)), _const_spec((1, d)), _const_spec(wgu.shape), _const_spec(wd.shape)],
        out_specs=tok,
        compiler_params=_params("arbitrary", "arbitrary"),
        name="ffn",
    )(x, mod_l, gpre, gpost, wgu, wd)


def _rope_tables(seq):
    half = MLA_ROPE_DIM // 2
    inv_freq = 1.0 / (ROPE_THETA ** (jnp.arange(half, dtype=jnp.float32) / half))
    ang = jnp.arange(seq).astype(jnp.float32)[:, None] * inv_freq[None, :]
    reps = V7X_LANES // MLA_ROPE_DIM
    cos = jnp.tile(jnp.concatenate([jnp.cos(ang), jnp.cos(ang)], axis=1), (1, reps))
    sin = jnp.tile(jnp.concatenate([-jnp.sin(ang), jnp.sin(ang)], axis=1), (1, reps))
    return cos, sin


def kernel(x, c, t5_table, w_ada, b_ada, g_mix_pre, g_mix_post, w_in, b_forget, g_q_lat, w_uq, g_kv_lat, w_ukv,
           g_group, w_out, g_ffn_pre, g_ffn_post, w_gate_up, w_down):
    batch, seq, d = x.shape
    assert d == D_MODEL and seq % TM == 0 and TM % MOBA_BLOCK == 0 and TQ == MOBA_BLOCK
    bf16 = jnp.bfloat16
    in_cols, uq_cols, ukv_cols = _in_proj_columns(), _uq_columns(), _ukv_columns()
    cos_t, sin_t = _rope_tables(seq)
    tril = np.tril(np.ones((TM, TM), np.float32))
    tril = jnp.asarray(np.concatenate([tril] * N_SPLIT, axis=1), bf16)
    moba_bias = _moba_bias_tiles(t5_table)
    fg_lane = np.arange(V7X_LANES)
    fg_used = (fg_lane < FGATE_SLOT * FOX_HEADS) & (fg_lane % FGATE_SLOT < 2 * N_SPLIT)
    fg_head = np.minimum(fg_lane // FGATE_SLOT, FOX_HEADS - 1)

    mod = _ada_mod(c, w_ada, b_ada)
    for l in range(DEPTH):
        mod_l = mod[l].reshape(batch, 1, 6 * d)
        fbias = jnp.where(jnp.asarray(fg_used), b_forget[l][fg_head], 0.0).reshape(1, V7X_LANES)
        pr = _proj(x, mod_l, g_mix_pre[l].reshape(1, d),
                   _take_columns(w_in[l], in_cols).astype(bf16), fbias,
                   g_q_lat[l].reshape(1, -1), _take_columns(w_uq[l], uq_cols).astype(bf16),
                   g_kv_lat[l].reshape(1, -1), _take_columns(w_ukv[l], ukv_cols).astype(bf16),
                   cos_t, sin_t, tril)
        g_a = g_group[l, :FOX_W].reshape(1, -1)
        g_b = g_group[l, FOX_W:FOX_W + MOBA_W].reshape(1, -1)
        g_c = g_group[l, FOX_W + MOBA_W:].reshape(1, -1)
        o_f = _attn_call(_fox_kernel, "fox_attn", FOX_HEADS, [pr["qf"], pr["eq"]], [pr["kf"], pr["ek"], pr["vf"]], [g_a])
        o_m = _attn_call(_moba_kernel, "moba_attn", MOBA_HEADS, [pr["qm"]], [pr["km"], pr["vm"], pr["kmean"]],
                         [moba_bias, g_b])
        o_c = _attn_call(_mla_kernel, "mla_attn", MLA_HEADS, [pr["qn"], pr["qr"]], [pr["kn"], pr["kr"], pr["vc"]], [g_c])
        x = _out_proj(x, o_f, o_m, o_c, mod_l, g_mix_post[l].reshape(1, d), w_out[l].astype(bf16))
        x = _ffn(x, mod_l, g_ffn_pre[l].reshape(1, d), g_ffn_post[l].reshape(1, d),
                 w_gate_up[l].astype(bf16), w_down[l].astype(bf16))
    return x
```

```python
import functools
import math
from typing import NamedTuple

import jax
import jax.numpy as jnp
import numpy as np
from jax import lax
from jax.experimental import pallas as pl
from jax.experimental.pallas import tpu as pltpu

D_MODEL = 1024
DEPTH = 2
FOX_HEADS = 4
FOX_HEAD_DIM = 64
MOBA_HEADS = 4
MOBA_HEAD_DIM = 64
MOBA_BLOCK = 256
MOBA_TOPK = 3
MLA_HEADS = 8
MLA_NOPE_DIM = 64
MLA_ROPE_DIM = 32
MLA_V_DIM = 64
MLA_Q_RANK = 256
MLA_KV_RANK = 128
ROPE_THETA = 10000.0
T5_BUCKETS = 32
T5_MAX_DISTANCE = 128
D_FF = -(-8 * D_MODEL // (3 * 256)) * 256
RMS_EPS = 1e-6
FOX_W = FOX_HEADS * FOX_HEAD_DIM
MOBA_W = MOBA_HEADS * MOBA_HEAD_DIM
MLA_W = MLA_HEADS * MLA_V_DIM
MIX_WIDTH = FOX_W + MOBA_W + MLA_W
IN_SIZES = (FOX_W, FOX_W, FOX_W, FOX_HEADS, MOBA_W, MOBA_W, MOBA_W, MLA_Q_RANK, MLA_KV_RANK, MLA_ROPE_DIM)

V7X_LANES = 128
V7X_VMEM_LIMIT_BYTES = 56 * 1024 * 1024

TM = 512
TQ = 256
PIPE_HEADS = 4
FFN_CHUNKS = ((0, 1024), (1024, 2048), (2048, D_FF))

HEAD_LANES = 64
MASK_NEG = -1e30
FGATE_SLOT = 8
N_SPLIT = 3

_C_QF, _C_KF, _C_VF = 0, 256, 512
_C_QM, _C_KM, _C_VM = 768, 1024, 1280
_C_CQ, _C_CKV = 1536, 1792
_C_FG, _C_KR, _C_KRS = 1920, 2048, 2176
IN_WIDTH_PADDED = 2304


def _in_proj_columns():
    off = np.cumsum((0,) + IN_SIZES)
    q_f, k_f, v_f, f_g, q_m, k_m, v_m, c_q, c_kv, k_r = (np.arange(off[i], off[i + 1]) for i in range(10))
    fg = np.full((V7X_LANES,), -1, np.int64)
    for h in range(FOX_HEADS):
        fg[FGATE_SLOT * h:FGATE_SLOT * h + 2 * N_SPLIT] = f_g[h]
    half = MLA_ROPE_DIM // 2
    kr4 = np.tile(k_r, V7X_LANES // MLA_ROPE_DIM)
    kr4s = np.tile(np.roll(k_r, -half), V7X_LANES // MLA_ROPE_DIM)
    cols = np.concatenate([q_f, k_f, v_f, q_m, k_m, v_m, c_q, c_kv, fg, kr4, kr4s])
    assert cols.shape == (IN_WIDTH_PADDED,)
    return cols


def _uq_columns():
    per = MLA_NOPE_DIM + MLA_ROPE_DIM
    half = MLA_ROPE_DIM // 2
    nope = np.concatenate([np.arange(h * per, h * per + MLA_NOPE_DIM) for h in range(MLA_HEADS)])
    rot = np.concatenate([np.arange(h * per + MLA_NOPE_DIM, (h + 1) * per) for h in range(MLA_HEADS)])
    rots = np.concatenate([np.roll(np.arange(h * per + MLA_NOPE_DIM, (h + 1) * per), -half) for h in range(MLA_HEADS)])
    return np.concatenate([nope, rot, rots])


def _ukv_columns():
    per = MLA_NOPE_DIM + MLA_V_DIM
    nope = np.concatenate([np.arange(h * per, h * per + MLA_NOPE_DIM) for h in range(MLA_HEADS)])
    val = np.concatenate([np.arange(h * per + MLA_NOPE_DIM, (h + 1) * per) for h in range(MLA_HEADS)])
    return np.concatenate([nope, val])


def _take_columns(w, cols):
    g = jnp.take(w, jnp.asarray(np.maximum(cols, 0), jnp.int32), axis=1)
    return jnp.where(jnp.asarray(cols >= 0)[None, :], g, 0.0)


def _t5_bucket_table(n):
    d = np.arange(n, dtype=np.int32)
    max_exact = T5_BUCKETS // 2
    nf = np.maximum(d, max_exact).astype(np.float32)
    ratio = np.log(nf / np.float32(max_exact)) / np.float32(math.log(T5_MAX_DISTANCE / max_exact))
    large = max_exact + (ratio.astype(np.float32) * np.float32(T5_BUCKETS - max_exact)).astype(np.int32)
    large = np.minimum(large, T5_BUCKETS - 1)
    return np.where(d < max_exact, d, large).astype(np.int32)


def _const_spec(shape):
    nd = len(shape)
    return pl.BlockSpec(shape, lambda *_: (0,) * nd, pipeline_mode=pl.Buffered(1))


def _params(*sem):
    return pltpu.CompilerParams(dimension_semantics=sem, vmem_limit_bytes=V7X_VMEM_LIMIT_BYTES)


def _rms(x, g):
    return x * lax.rsqrt(jnp.mean(x * x, axis=-1, keepdims=True) + RMS_EPS) * g


def _split3(v):
    hi = v.astype(jnp.bfloat16)
    r1 = v - hi.astype(jnp.float32)
    mid = r1.astype(jnp.bfloat16)
    lo = (r1 - mid.astype(jnp.float32)).astype(jnp.bfloat16)
    return hi, mid, lo


def _dot(a, b):
    return jnp.dot(a, b, preferred_element_type=jnp.float32)


def _ada_kernel(c_ref, w_ref, b_ref, o_ref):
    c = c_ref[...]
    act = (c * jax.nn.sigmoid(c)).astype(jnp.bfloat16)
    o_ref[0] = _dot(act, w_ref[0].astype(jnp.bfloat16)) + b_ref[0]


def _ada_mod(c, w_ada, b_ada):
    depth, d, six_d = w_ada.shape
    batch = c.shape[0]
    n_col = six_d // d
    return pl.pallas_call(
        _ada_kernel,
        out_shape=jax.ShapeDtypeStruct((depth, batch, six_d), jnp.float32),
        grid=(depth, n_col),
        in_specs=[
            pl.BlockSpec((batch, d), lambda l, j: (0, 0)),
            pl.BlockSpec((1, d, d), lambda l, j: (l, 0, j)),
            pl.BlockSpec((1, 1, d), lambda l, j: (l, 0, j)),
        ],
        out_specs=pl.BlockSpec((1, batch, d), lambda l, j: (l, 0, j)),
        compiler_params=_params("arbitrary", "arbitrary"),
        name="ada_mod",
    )(c, w_ada, b_ada.reshape(depth, 1, six_d))


def _proj_kernel(x_ref, mod_ref, gpre_ref, win_ref, fb_ref, gq_ref, wuq_ref, gkv_ref, wukv_ref,
                 cos_ref, sin_ref, tril_ref,
                 qf_ref, eq_ref, kf_ref, ek_ref, vf_ref, qm_ref, km_ref, vm_ref, kmean_ref,
                 qn_ref, qr_ref, kn_ref, kr_ref, vc_ref, carry_ref):
    t = pl.program_id(1)
    bf16 = jnp.bfloat16
    x = x_ref[0]
    shift = mod_ref[0, :, 0:D_MODEL]
    scale = mod_ref[0, :, D_MODEL:2 * D_MODEL]
    h = (_rms(x, gpre_ref[...]) * (1.0 + scale) + shift).astype(bf16)

    def seg(c0, width):
        return _dot(h, win_ref[:, c0:c0 + width])

    qf_ref[0] = (seg(_C_QF, FOX_W) * FOX_HEAD_DIM ** -0.5).astype(bf16)
    kf_ref[0] = seg(_C_KF, FOX_W).astype(bf16)
    vf_ref[0] = seg(_C_VF, FOX_W).astype(bf16)
    qm_ref[0] = (seg(_C_QM, MOBA_W) * MOBA_HEAD_DIM ** -0.5).astype(bf16)
    km = seg(_C_KM, MOBA_W)
    km_ref[0] = km.astype(bf16)
    vm_ref[0] = seg(_C_VM, MOBA_W).astype(bf16)
    kmean_ref[0, 0] = jnp.mean(km.reshape(TM // MOBA_BLOCK, MOBA_BLOCK, MOBA_W), axis=1)

    lane = lax.broadcasted_iota(jnp.int32, (1, V7X_LANES), 1)
    slot = lane % FGATE_SLOT
    used = (lane < FGATE_SLOT * FOX_HEADS) & (slot < 2 * N_SPLIT)
    fl = seg(_C_FG, V7X_LANES) + fb_ref[...]
    logf = jnp.where(used, jnp.minimum(fl, 0.0) - jnp.log1p(jnp.exp(-jnp.abs(fl))), 0.0)

    @pl.when(t == 0)
    def _():
        carry_ref[...] = jnp.zeros_like(carry_ref)

    fcum = _dot(tril_ref[...], jnp.concatenate(_split3(logf), axis=0)) + carry_ref[0:1, :]
    carry_ref[0:1, :] = fcum[TM - 1:TM, :]
    hi, mid, lo = (p.astype(jnp.float32) for p in _split3(fcum))
    parts = jnp.where(slot % N_SPLIT == 0, hi, jnp.where(slot % N_SPLIT == 1, mid, lo))
    eq_ref[0] = jnp.where(used, jnp.where(slot < N_SPLIT, parts, 1.0), 0.0).astype(bf16)
    ek_ref[0] = jnp.where(used, jnp.where(slot < N_SPLIT, 1.0, -parts), 0.0).astype(bf16)

    cos = cos_ref[...]
    sin = sin_ref[...]
    mla_scale = (MLA_NOPE_DIM + MLA_ROPE_DIM) ** -0.5
    cq = _rms(seg(_C_CQ, MLA_Q_RANK), gq_ref[...]).astype(bf16)
    n_nope = MLA_HEADS * MLA_NOPE_DIM
    n_rot = MLA_HEADS * MLA_ROPE_DIM
    qn_ref[0] = (_dot(cq, wuq_ref[:, 0:n_nope]) * mla_scale).astype(bf16)
    q_rot = _dot(cq, wuq_ref[:, n_nope:n_nope + n_rot])
    q_rot_sw = _dot(cq, wuq_ref[:, n_nope + n_rot:n_nope + 2 * n_rot])
    cos2 = jnp.concatenate([cos] * (n_rot // V7X_LANES), axis=1)
    sin2 = jnp.concatenate([sin] * (n_rot // V7X_LANES), axis=1)
    qr_ref[0] = ((q_rot * cos2 + q_rot_sw * sin2) * mla_scale).astype(bf16)
    ckv = _rms(seg(_C_CKV, MLA_KV_RANK), gkv_ref[...]).astype(bf16)
    kn_ref[0] = _dot(ckv, wukv_ref[:, 0:n_nope]).astype(bf16)
    vc_ref[0] = _dot(ckv, wukv_ref[:, n_nope:n_nope + MLA_W]).astype(bf16)
    kr_ref[0] = (seg(_C_KR, V7X_LANES) * cos + seg(_C_KRS, V7X_LANES) * sin).astype(bf16)


def _proj(x, mod_l, gpre, win, fbias, gq, wuq, gkv, wukv, cos_t, sin_t, tril):
    batch, seq, d = x.shape
    nt = seq // TM
    bf16 = jnp.bfloat16

    def tok(width):
        return pl.BlockSpec((1, TM, width), lambda b, t: (b, t, 0))

    def out(width):
        return jax.ShapeDtypeStruct((batch, seq, width), bf16)

    widths = dict(qf=FOX_W, eq=V7X_LANES, kf=FOX_W, ek=V7X_LANES, vf=FOX_W, qm=MOBA_W, km=MOBA_W, vm=MOBA_W)
    mla_widths = dict(qn=MLA_HEADS * MLA_NOPE_DIM, qr=MLA_HEADS * MLA_ROPE_DIM, kn=MLA_HEADS * MLA_NOPE_DIM,
                      kr=V7X_LANES, vc=MLA_W)
    nb = TM // MOBA_BLOCK
    out_shape = ([out(w) for w in widths.values()]
                 + [jax.ShapeDtypeStruct((batch, nt, nb, MOBA_W), jnp.float32)]
                 + [out(w) for w in mla_widths.values()])
    out_specs = ([tok(w) for w in widths.values()]
                 + [pl.BlockSpec((1, 1, nb, MOBA_W), lambda b, t: (b, t, 0, 0))]
                 + [tok(w) for w in mla_widths.values()])
    res = pl.pallas_call(
        _proj_kernel,
        out_shape=out_shape,
        grid=(batch, nt),
        in_specs=[
            tok(d),
            pl.BlockSpec((1, 1, 6 * d), lambda b, t: (b, 0, 0)),
            _const_spec((1, d)),
            _const_spec(win.shape),
            _const_spec((1, V7X_LANES)),
            _const_spec((1, MLA_Q_RANK)),
            _const_spec(wuq.shape),
            _const_spec((1, MLA_KV_RANK)),
            _const_spec(wukv.shape),
            pl.BlockSpec((TM, V7X_LANES), lambda b, t: (t, 0)),
            pl.BlockSpec((TM, V7X_LANES), lambda b, t: (t, 0)),
            _const_spec(tril.shape),
        ],
        out_specs=out_specs,
        scratch_shapes=[pltpu.VMEM((8, V7X_LANES), jnp.float32)],
        compiler_params=_params("arbitrary", "arbitrary"),
        name="in_proj",
    )(x, mod_l, gpre, win, fbias, gq, wuq, gkv, wukv, cos_t, sin_t, tril)
    names = list(widths) + ["kmean"] + list(mla_widths)
    r = dict(zip(names, res))
    r["kmean"] = r["kmean"].reshape(batch, seq // MOBA_BLOCK, MOBA_W)
    return r


def _transpose_to_bf16(x):
    return x.astype(jnp.float32).T.astype(jnp.bfloat16)


def _head_rows(x_pair, e):
    row = lax.broadcasted_iota(jnp.int32, x_pair.shape, 0)
    keep = (row >= HEAD_LANES) if e else (row < HEAD_LANES)
    return jnp.where(keep, x_pair, 0.0)


def _causal_bias_t():
    key = lax.broadcasted_iota(jnp.int32, (TQ, TQ), 0)
    qry = lax.broadcasted_iota(jnp.int32, (TQ, TQ), 1)
    return jnp.where(key <= qry, 0.0, MASK_NEG)


def _tile_rows(t):
    return pl.ds(pl.multiple_of(t * TQ, TQ), TQ)


class _FlashScratch(NamedTuple):
    vt: object
    rhs: object
    s: object
    p: object
    alpha: object
    m: object
    l: object
    acc: object


def _flash_begin(v_ref, sc):
    for c in range(sc.vt.shape[0]):
        sc.vt[c] = _transpose_to_bf16(v_ref[0, c * TQ:(c + 1) * TQ, :])
    sc.m[...] = jnp.full(sc.m.shape, MASK_NEG, jnp.float32)
    sc.l[...] = jnp.zeros(sc.l.shape, jnp.float32)
    sc.acc[...] = jnp.zeros(sc.acc.shape, jnp.float32)


def _flash_pipeline(tab_ref, segments, heads, n_heads, lhs_tile, sc):
    slot = {h: h - heads[0] for h in heads}
    n_steps = sum(count for count, _ in segments)

    def ij(t):
        t = jnp.clip(t, 0, n_steps - 1)
        return tab_ref[0, t], tab_ref[1, t]

    def scores(t, bias_fn):
        i, j = ij(t)
        lhs = {h // 2: lhs_tile(j, h // 2) for h in heads if h % 2 == 0}
        out = {}
        for h in heads:
            s = _dot(lhs[h // 2], sc.rhs[i * n_heads + h])
            b = bias_fn(i, j, h)
            out[h] = s if b is None else s + b
        return out

    def store_scores(vals):
        for h in heads:
            sc.s[slot[h]] = vals[h]

    def softmax(t):
        i, _ = ij(t)
        s_val = {h: sc.s[slot[h]] for h in heads}
        m_old = {h: sc.m[i * n_heads + h] for h in heads}
        m_new = {h: jnp.maximum(m_old[h], jnp.max(s_val[h], axis=0, keepdims=True)) for h in heads}
        alpha = {h: jnp.exp(m_old[h] - m_new[h]) for h in heads}
        probs = {h: jnp.exp(s_val[h] - m_new[h]) for h in heads}
        for h in heads:
            st = i * n_heads + h
            sc.l[st] = alpha[h] * sc.l[st] + jnp.sum(probs[h], axis=0, keepdims=True)
            sc.m[st] = m_new[h]
            sc.alpha[slot[h]] = alpha[h]
            sc.p[slot[h]] = probs[h].astype(jnp.bfloat16)

    def values(t):
        i, j = ij(t)
        pv = {h: _dot(sc.vt[j, h * HEAD_LANES:(h + 1) * HEAD_LANES, :], sc.p[slot[h]]) for h in heads}
        for h in heads:
            st = i * n_heads + h
            sc.acc[st] = sc.alpha[slot[h]] * sc.acc[st] + pv[h]

    def prologue(_, carry):
        for h in heads:
            sc.alpha[slot[h]] = jnp.ones(sc.alpha.shape[1:], jnp.float32)
            sc.p[slot[h]] = jnp.zeros(sc.p.shape[1:], jnp.bfloat16)
        store_scores(scores(0, segments[0][1]))
        return carry

    lax.fori_loop(0, tab_ref[0, n_steps], prologue, 0)

    first = 0
    for count, bias_fn in segments:

        def body(t, carry, bias_fn=bias_fn):
            nxt = scores(t + 1, bias_fn)
            values(t - 1)
            softmax(t)
            store_scores(nxt)
            return carry

        lax.fori_loop(max(first - 1, 0), first + count - 1, body, 0)
        first += count
    values(n_steps - 2)
    softmax(n_steps - 1)
    values(n_steps - 1)


def _flash_finish(n_heads, sc, g_ref, o_ref):
    def body(iq, carry):
        o_t = jnp.concatenate([sc.acc[iq * n_heads + h] * (1.0 / sc.l[iq * n_heads + h]) for h in range(n_heads)],
                              axis=0)
        o_ref[0, _tile_rows(iq), :] = _rms(o_t.T, g_ref[...]).astype(o_ref.dtype)
        return carry

    lax.fori_loop(0, o_ref.shape[1] // TQ, body, 0)


def _causal_segments(n_tiles):
    return [(n_tiles * (n_tiles - 1) // 2, lambda i, j, h: None), (n_tiles, lambda i, j, h: _causal_bias_t())]


def _fox_kernel(tab_ref, q_ref, eq_ref, k_ref, ek_ref, v_ref, g_ref, o_ref, *scratch):
    sc = _FlashScratch(*scratch)
    n_tiles = q_ref.shape[1] // TQ
    _flash_begin(v_ref, sc)

    def build_rhs(iq, carry):
        rows = _tile_rows(iq)
        q_t = q_ref[0, rows, :].astype(jnp.float32).T
        eq_t = eq_ref[0, rows, :].astype(jnp.float32).T
        slot_head = lax.broadcasted_iota(jnp.int32, eq_t.shape, 0) // FGATE_SLOT
        for h in range(FOX_HEADS):
            pair, e = divmod(h, 2)
            q_h = _head_rows(q_t[pair * V7X_LANES:(pair + 1) * V7X_LANES], e)
            e_h = jnp.where(slot_head == h, eq_t, 0.0)
            sc.rhs[iq * FOX_HEADS + h] = jnp.concatenate([q_h, e_h], axis=0).astype(jnp.bfloat16)
        return carry

    lax.fori_loop(0, n_tiles, build_rhs, 0)

    def lhs_tile(j, pair):
        rows = _tile_rows(j)
        return jnp.concatenate([k_ref[0, rows, pair * V7X_LANES:(pair + 1) * V7X_LANES], ek_ref[0, rows, :]], axis=1)

    _flash_pipeline(tab_ref, _causal_segments(n_tiles), range(FOX_HEADS), FOX_HEADS, lhs_tile, sc)
    _flash_finish(FOX_HEADS, sc, g_ref, o_ref)


def _moba_kernel(tab_ref, q_ref, k_ref, v_ref, kmean_ref, bias_ref, g_ref, o_ref, *scratch):
    sc = _FlashScratch(*scratch)
    f32, bf16 = jnp.float32, jnp.bfloat16
    n_tiles = q_ref.shape[1] // TQ
    n_blocks = kmean_ref.shape[1]
    _flash_begin(v_ref, sc)
    sel_rows = 16
    blk = lax.broadcasted_iota(jnp.int32, (sel_rows, TQ), 0)
    lane = lax.broadcasted_iota(jnp.int32, (1, V7X_LANES), 1)

    def build_rhs(iq, carry):
        q_t = q_ref[0, _tile_rows(iq), :].astype(f32).T
        for h in range(MOBA_HEADS):
            pair, e = divmod(h, 2)
            q_h = _head_rows(q_t[pair * V7X_LANES:(pair + 1) * V7X_LANES], e)
            q_hb = q_h.astype(bf16)
            kmean = kmean_ref[0, :, pair * V7X_LANES:(pair + 1) * V7X_LANES]
            km_hi = kmean.astype(bf16).astype(f32)
            gate_lhs = jnp.concatenate([jnp.concatenate([km_hi, kmean - km_hi], axis=1),
                                        jnp.zeros((sel_rows - n_blocks, 2 * V7X_LANES), f32)], axis=0).astype(bf16)
            gate = _dot(gate_lhs, jnp.concatenate([q_hb, q_hb], axis=0))
            beaten = jnp.zeros((sel_rows, TQ), f32)
            for mblk in range(n_blocks - 1):
                gm = gate[mblk:mblk + 1, :]
                wins = (gm > gate) | ((gm == gate) & (mblk < blk))
                beaten = beaten + jnp.where(wins, jnp.where(mblk < iq, 1.0, 0.0), 0.0)
            keep = ((blk < iq) & (beaten < MOBA_TOPK)) | (blk == iq)
            sel = jnp.where(keep, 0.0, MASK_NEG)
            pad = jnp.zeros((V7X_LANES - sel_rows, TQ), f32)
            sc.rhs[iq * MOBA_HEADS + h] = jnp.concatenate([q_h, sel, pad], axis=0).astype(bf16)
        return carry

    lax.fori_loop(0, n_tiles, build_rhs, 0)

    def lhs_tile(j, pair):
        onehot = jnp.broadcast_to(jnp.where(lane == j, 1.0, 0.0).astype(bf16), (TQ, V7X_LANES))
        return jnp.concatenate([k_ref[0, _tile_rows(j), pair * V7X_LANES:(pair + 1) * V7X_LANES], onehot], axis=1)

    def bias_fn(i, j, h):
        return bias_ref[h, jnp.minimum(i - j, 2)]

    segments = [(n_tiles * (n_tiles + 1) // 2, bias_fn)]
    _flash_pipeline(tab_ref, segments, range(MOBA_HEADS), MOBA_HEADS, lhs_tile, sc)
    _flash_finish(MOBA_HEADS, sc, g_ref, o_ref)


def _mla_kernel(tab_ref, qn_ref, qr_ref, kn_ref, kr_ref, v_ref, g_ref, o_ref, *scratch):
    sc = _FlashScratch(*scratch)
    n_tiles = qn_ref.shape[1] // TQ
    _flash_begin(v_ref, sc)
    heads_per_rot = V7X_LANES // MLA_ROPE_DIM
    rot_slot = lax.broadcasted_iota(jnp.int32, (V7X_LANES, TQ), 0) // MLA_ROPE_DIM

    def build_rhs(iq, carry):
        rows = _tile_rows(iq)
        qn_t = qn_ref[0, rows, :].astype(jnp.float32).T
        qr_t = qr_ref[0, rows, :].astype(jnp.float32).T
        for h in range(MLA_HEADS):
            pair, e = divmod(h, 2)
            quad, slot = divmod(h, heads_per_rot)
            q_h = _head_rows(qn_t[pair * V7X_LANES:(pair + 1) * V7X_LANES], e)
            r_h = jnp.where(rot_slot == slot, qr_t[quad * V7X_LANES:(quad + 1) * V7X_LANES], 0.0)
            sc.rhs[iq * MLA_HEADS + h] = jnp.concatenate([q_h, r_h], axis=0).astype(jnp.bfloat16)
        return carry

    lax.fori_loop(0, n_tiles, build_rhs, 0)

    def lhs_tile(j, pair):
        rows = _tile_rows(j)
        return jnp.concatenate([kn_ref[0, rows, pair * V7X_LANES:(pair + 1) * V7X_LANES], kr_ref[0, rows, :]], axis=1)

    for h0 in range(0, MLA_HEADS, PIPE_HEADS):
        _flash_pipeline(tab_ref, _causal_segments(n_tiles), range(h0, h0 + PIPE_HEADS), MLA_HEADS, lhs_tile, sc)
    _flash_finish(MLA_HEADS, sc, g_ref, o_ref)


def _tile_pairs(n_tiles, diagonal_last):
    if diagonal_last:
        pairs = [(i, j) for i in range(n_tiles) for j in range(i)] + [(i, i) for i in range(n_tiles)]
    else:
        pairs = [(i, j) for i in range(n_tiles) for j in range(i + 1)]
    return jnp.asarray(np.array(pairs + [(1, 1)], np.int32).T)


def _attn_call(kernel, name, n_heads, diagonal_last, arrays, const_arrays):
    batch, seq, _ = arrays[0].shape
    n_tiles = seq // TQ
    out_width = n_heads * HEAD_LANES
    row = lambda a: pl.BlockSpec((1,) + a.shape[1:], lambda b: (b, 0, 0))
    return pl.pallas_call(
        kernel,
        out_shape=jax.ShapeDtypeStruct((batch, seq, out_width), jnp.bfloat16),
        grid=(batch,),
        in_specs=([pl.BlockSpec(memory_space=pltpu.SMEM)] + [row(a) for a in arrays]
                  + [_const_spec(a.shape) for a in const_arrays]),
        out_specs=pl.BlockSpec((1, seq, out_width), lambda b: (b, 0, 0)),
        scratch_shapes=[
            pltpu.VMEM((n_tiles, out_width, TQ), jnp.bfloat16),
            pltpu.VMEM((n_tiles * n_heads, 2 * V7X_LANES, TQ), jnp.bfloat16),
            pltpu.VMEM((PIPE_HEADS, TQ, TQ), jnp.float32),
            pltpu.VMEM((PIPE_HEADS, TQ, TQ), jnp.bfloat16),
            pltpu.VMEM((PIPE_HEADS, 1, TQ), jnp.float32),
            pltpu.VMEM((n_tiles * n_heads, 1, TQ), jnp.float32),
            pltpu.VMEM((n_tiles * n_heads, 1, TQ), jnp.float32),
            pltpu.VMEM((n_tiles * n_heads, HEAD_LANES, TQ), jnp.float32),
        ],
        compiler_params=_params("arbitrary"),
        name=name,
    )(_tile_pairs(n_tiles, diagonal_last), *arrays, *const_arrays)


def _bias_tile_kernel(table_ref, bucket_ref, o_ref):
    h = pl.program_id(0)
    for k in range(bucket_ref.shape[0]):
        bkt = bucket_ref[k]
        tile = jnp.full(bkt.shape, MASK_NEG, jnp.float32)
        for b in range(T5_BUCKETS):
            tile = jnp.where(bkt == b, table_ref[h, b], tile)
        o_ref[0, k] = tile


def _moba_bias_tiles(t5_table):
    bucket = _t5_bucket_table(3 * TQ)
    key = np.arange(TQ)[:, None]
    qry = np.arange(TQ)[None, :]
    kinds = []
    for k in range(3):
        dist = qry - key + k * TQ
        kinds.append(np.where(dist >= 0, bucket[np.maximum(dist, 0)], -1))
    buckets = jnp.asarray(np.stack(kinds), jnp.int32)
    return pl.pallas_call(
        _bias_tile_kernel,
        out_shape=jax.ShapeDtypeStruct((MOBA_HEADS, 3, TQ, TQ), jnp.float32),
        grid=(MOBA_HEADS,),
        in_specs=[pl.BlockSpec(memory_space=pltpu.SMEM), _const_spec(buckets.shape)],
        out_specs=pl.BlockSpec((1, 3, TQ, TQ), lambda h: (h, 0, 0, 0)),
        compiler_params=_params("arbitrary"),
        name="t5_bias_tiles",
    )(t5_table.T, buckets)


def _out_kernel(x_ref, of_ref, om_ref, oc_ref, mod_ref, gpost_ref, w_ref, o_ref):
    o = jnp.concatenate([of_ref[0], om_ref[0], oc_ref[0]], axis=1)
    y = _dot(o, w_ref[...])
    gate = mod_ref[0, :, 2 * D_MODEL:3 * D_MODEL]
    o_ref[0] = x_ref[0] + gate * _rms(y, gpost_ref[...])


def _out_proj(x, o_f, o_m, o_c, mod_l, gpost, w_out):
    batch, seq, d = x.shape

    def tok(width):
        return pl.BlockSpec((1, TM, width), lambda b, t: (b, t, 0))

    return pl.pallas_call(
        _out_kernel,
        out_shape=jax.ShapeDtypeStruct(x.shape, x.dtype),
        grid=(batch, seq // TM),
        in_specs=[tok(d), tok(FOX_W), tok(MOBA_W), tok(MLA_W),
                  pl.BlockSpec((1, 1, 6 * d), lambda b, t: (b, 0, 0)),
                  _const_spec((1, d)), _const_spec(w_out.shape)],
        out_specs=tok(d),
        compiler_params=_params("arbitrary", "arbitrary"),
        name="out_proj",
    )(x, o_f, o_m, o_c, mod_l, gpost, w_out)


def _ffn_kernel(x_ref, mod_ref, gpre_ref, gpost_ref, wgu_ref, wd_ref, o_ref):
    bf16 = jnp.bfloat16
    x = x_ref[0]
    shift = mod_ref[0, :, 3 * D_MODEL:4 * D_MODEL]
    scale = mod_ref[0, :, 4 * D_MODEL:5 * D_MODEL]
    gate = mod_ref[0, :, 5 * D_MODEL:6 * D_MODEL]
    h = (_rms(x, gpre_ref[...]) * (1.0 + scale) + shift).astype(bf16)
    acc = None
    for c0, c1 in FFN_CHUNKS:
        g = _dot(h, wgu_ref[:, c0:c1])
        u = _dot(h, wgu_ref[:, D_FF + c0:D_FF + c1])
        a = (g * jax.nn.sigmoid(g) * u).astype(bf16)
        part = _dot(a, wd_ref[c0:c1, :])
        acc = part if acc is None else acc + part
    o_ref[0] = x + gate * _rms(acc, gpost_ref[...])


def _ffn(x, mod_l, gpre, gpost, wgu, wd):
    batch, seq, d = x.shape
    tok = pl.BlockSpec((1, TM, d), lambda b, t: (b, t, 0))
    return pl.pallas_call(
        _ffn_kernel,
        out_shape=jax.ShapeDtypeStruct(x.shape, x.dtype),
        grid=(batch, seq // TM),
        in_specs=[tok, pl.BlockSpec((1, 1, 6 * d), lambda b, t: (b, 0, 0)),
                  _const_spec((1, d)), _const_spec((1, d)), _const_spec(wgu.shape), _const_spec(wd.shape)],
        out_specs=tok,
        compiler_params=_params("arbitrary", "arbitrary"),
        name="ffn",
    )(x, mod_l, gpre, gpost, wgu, wd)


def _rope_tables(seq):
    half = MLA_ROPE_DIM // 2
    inv_freq = 1.0 / (ROPE_THETA ** (jnp.arange(half, dtype=jnp.float32) / half))
    ang = jnp.arange(seq).astype(jnp.float32)[:, None] * inv_freq[None, :]
    reps = V7X_LANES // MLA_ROPE_DIM
    cos = jnp.tile(jnp.concatenate([jnp.cos(ang), jnp.cos(ang)], axis=1), (1, reps))
    sin = jnp.tile(jnp.concatenate([-jnp.sin(ang), jnp.sin(ang)], axis=1), (1, reps))
    return cos, sin


def kernel(x, c, t5_table, w_ada, b_ada, g_mix_pre, g_mix_post, w_in, b_forget, g_q_lat, w_uq, g_kv_lat, w_ukv,
           g_group, w_out, g_ffn_pre, g_ffn_post, w_gate_up, w_down):
    batch, seq, d = x.shape
    assert d == D_MODEL and seq % TM == 0 and TM % MOBA_BLOCK == 0 and TQ == MOBA_BLOCK
    bf16 = jnp.bfloat16
    in_cols, uq_cols, ukv_cols = _in_proj_columns(), _uq_columns(), _ukv_columns()
    cos_t, sin_t = _rope_tables(seq)
    tril = np.tril(np.ones((TM, TM), np.float32))
    tril = jnp.asarray(np.concatenate([tril] * N_SPLIT, axis=1), bf16)
    moba_bias = _moba_bias_tiles(t5_table)
    fg_lane = np.arange(V7X_LANES)
    fg_used = (fg_lane < FGATE_SLOT * FOX_HEADS) & (fg_lane % FGATE_SLOT < 2 * N_SPLIT)
    fg_head = np.minimum(fg_lane // FGATE_SLOT, FOX_HEADS - 1)

    mod = _ada_mod(c, w_ada, b_ada)
    for l in range(DEPTH):
        mod_l = mod[l].reshape(batch, 1, 6 * d)
        fbias = jnp.where(jnp.asarray(fg_used), b_forget[l][fg_head], 0.0).reshape(1, V7X_LANES)
        pr = _proj(x, mod_l, g_mix_pre[l].reshape(1, d),
                   _take_columns(w_in[l], in_cols).astype(bf16), fbias,
                   g_q_lat[l].reshape(1, -1), _take_columns(w_uq[l], uq_cols).astype(bf16),
                   g_kv_lat[l].reshape(1, -1), _take_columns(w_ukv[l], ukv_cols).astype(bf16),
                   cos_t, sin_t, tril)
        g_a = g_group[l, :FOX_W].reshape(1, -1)
        g_b = g_group[l, FOX_W:FOX_W + MOBA_W].reshape(1, -1)
        g_c = g_group[l, FOX_W + MOBA_W:].reshape(1, -1)
        o_f = _attn_call(_fox_kernel, "fox_attn", FOX_HEADS, True,
                         [pr["qf"], pr["eq"], pr["kf"], pr["ek"], pr["vf"]], [g_a])
        o_m = _attn_call(_moba_kernel, "moba_attn", MOBA_HEADS, False,
                         [pr["qm"], pr["km"], pr["vm"], pr["kmean"]], [moba_bias, g_b])
        o_c = _attn_call(_mla_kernel, "mla_attn", MLA_HEADS, True,
                         [pr["qn"], pr["qr"], pr["kn"], pr["kr"], pr["vc"]], [g_c])
        x = _out_proj(x, o_f, o_m, o_c, mod_l, g_mix_post[l].reshape(1, d), w_out[l].astype(bf16))
        x = _ffn(x, mod_l, g_ffn_pre[l].reshape(1, d), g_ffn_post[l].reshape(1, d),
                 w_gate_up[l].astype(bf16), w_down[l].astype(bf16))
    return x
```

```python
import functools
import math
from typing import NamedTuple

import jax
import jax.numpy as jnp
import numpy as np
from jax import lax
from jax.experimental import pallas as pl
from jax.experimental.pallas import tpu as pltpu

D_MODEL = 1024
DEPTH = 2
FOX_HEADS = 4
FOX_HEAD_DIM = 64
MOBA_HEADS = 4
MOBA_HEAD_DIM = 64
MOBA_BLOCK = 256
MOBA_TOPK = 3
MLA_HEADS = 8
MLA_NOPE_DIM = 64
MLA_ROPE_DIM = 32
MLA_V_DIM = 64
MLA_Q_RANK = 256
MLA_KV_RANK = 128
ROPE_THETA = 10000.0
T5_BUCKETS = 32
T5_MAX_DISTANCE = 128
D_FF = -(-8 * D_MODEL // (3 * 256)) * 256
RMS_EPS = 1e-6
FOX_W = FOX_HEADS * FOX_HEAD_DIM
MOBA_W = MOBA_HEADS * MOBA_HEAD_DIM
MLA_W = MLA_HEADS * MLA_V_DIM
MIX_WIDTH = FOX_W + MOBA_W + MLA_W
IN_SIZES = (FOX_W, FOX_W, FOX_W, FOX_HEADS, MOBA_W, MOBA_W, MOBA_W, MLA_Q_RANK, MLA_KV_RANK, MLA_ROPE_DIM)

V7X_LANES = 128
V7X_VMEM_LIMIT_BYTES = 56 * 1024 * 1024

TM = 512
TQ = 256
PIPE_HEADS = 4
FFN_CHUNKS = ((0, 1024), (1024, 2048), (2048, D_FF))

HEAD_LANES = 64
MASK_NEG = -1e30
LOG2E = math.log2(math.e)
V_ROWS = 80
FGATE_SLOT = 8
N_SPLIT = 3

_C_QF, _C_KF, _C_VF = 0, 256, 512
_C_QM, _C_KM, _C_VM = 768, 1024, 1280
_C_CQ, _C_CKV = 1536, 1792
_C_FG, _C_KR, _C_KRS = 1920, 2048, 2176
IN_WIDTH_PADDED = 2304


def _in_proj_columns():
    off = np.cumsum((0,) + IN_SIZES)
    q_f, k_f, v_f, f_g, q_m, k_m, v_m, c_q, c_kv, k_r = (np.arange(off[i], off[i + 1]) for i in range(10))
    fg = np.full((V7X_LANES,), -1, np.int64)
    for h in range(FOX_HEADS):
        fg[FGATE_SLOT * h:FGATE_SLOT * h + 2 * N_SPLIT] = f_g[h]
    half = MLA_ROPE_DIM // 2
    kr4 = np.tile(k_r, V7X_LANES // MLA_ROPE_DIM)
    kr4s = np.tile(np.roll(k_r, -half), V7X_LANES // MLA_ROPE_DIM)
    cols = np.concatenate([q_f, k_f, v_f, q_m, k_m, v_m, c_q, c_kv, fg, kr4, kr4s])
    assert cols.shape == (IN_WIDTH_PADDED,)
    return cols


def _uq_columns():
    per = MLA_NOPE_DIM + MLA_ROPE_DIM
    half = MLA_ROPE_DIM // 2
    nope = np.concatenate([np.arange(h * per, h * per + MLA_NOPE_DIM) for h in range(MLA_HEADS)])
    rot = np.concatenate([np.arange(h * per + MLA_NOPE_DIM, (h + 1) * per) for h in range(MLA_HEADS)])
    rots = np.concatenate([np.roll(np.arange(h * per + MLA_NOPE_DIM, (h + 1) * per), -half) for h in range(MLA_HEADS)])
    return np.concatenate([nope, rot, rots])


def _ukv_columns():
    per = MLA_NOPE_DIM + MLA_V_DIM
    nope = np.concatenate([np.arange(h * per, h * per + MLA_NOPE_DIM) for h in range(MLA_HEADS)])
    val = np.concatenate([np.arange(h * per + MLA_NOPE_DIM, (h + 1) * per) for h in range(MLA_HEADS)])
    return np.concatenate([nope, val])


def _take_columns(w, cols):
    g = jnp.take(w, jnp.asarray(np.maximum(cols, 0), jnp.int32), axis=1)
    return jnp.where(jnp.asarray(cols >= 0)[None, :], g, 0.0)


def _t5_bucket_table(n):
    d = np.arange(n, dtype=np.int32)
    max_exact = T5_BUCKETS // 2
    nf = np.maximum(d, max_exact).astype(np.float32)
    ratio = np.log(nf / np.float32(max_exact)) / np.float32(math.log(T5_MAX_DISTANCE / max_exact))
    large = max_exact + (ratio.astype(np.float32) * np.float32(T5_BUCKETS - max_exact)).astype(np.int32)
    large = np.minimum(large, T5_BUCKETS - 1)
    return np.where(d < max_exact, d, large).astype(np.int32)


def _const_spec(shape):
    nd = len(shape)
    return pl.BlockSpec(shape, lambda *_: (0,) * nd, pipeline_mode=pl.Buffered(1))


def _params(*sem):
    return pltpu.CompilerParams(dimension_semantics=sem, vmem_limit_bytes=V7X_VMEM_LIMIT_BYTES)


def _rms(x, g):
    return x * lax.rsqrt(jnp.mean(x * x, axis=-1, keepdims=True) + RMS_EPS) * g


def _split3(v):
    hi = v.astype(jnp.bfloat16)
    r1 = v - hi.astype(jnp.float32)
    mid = r1.astype(jnp.bfloat16)
    lo = (r1 - mid.astype(jnp.float32)).astype(jnp.bfloat16)
    return hi, mid, lo


def _dot(a, b):
    return jnp.dot(a, b, preferred_element_type=jnp.float32)


def _ada_kernel(c_ref, w_ref, b_ref, o_ref):
    c = c_ref[...]
    act = (c * jax.nn.sigmoid(c)).astype(jnp.bfloat16)
    o_ref[0] = _dot(act, w_ref[0].astype(jnp.bfloat16)) + b_ref[0]


def _ada_mod(c, w_ada, b_ada):
    depth, d, six_d = w_ada.shape
    batch = c.shape[0]
    n_col = six_d // d
    return pl.pallas_call(
        _ada_kernel,
        out_shape=jax.ShapeDtypeStruct((depth, batch, six_d), jnp.float32),
        grid=(depth, n_col),
        in_specs=[
            pl.BlockSpec((batch, d), lambda l, j: (0, 0)),
            pl.BlockSpec((1, d, d), lambda l, j: (l, 0, j)),
            pl.BlockSpec((1, 1, d), lambda l, j: (l, 0, j)),
        ],
        out_specs=pl.BlockSpec((1, batch, d), lambda l, j: (l, 0, j)),
        compiler_params=_params("arbitrary", "arbitrary"),
        name="ada_mod",
    )(c, w_ada, b_ada.reshape(depth, 1, six_d))


def _proj_kernel(x_ref, mod_ref, gpre_ref, win_ref, fb_ref, gq_ref, wuq_ref, gkv_ref, wukv_ref,
                 cos_ref, sin_ref, tril_ref,
                 qf_ref, eq_ref, kf_ref, ek_ref, vf_ref, qm_ref, km_ref, vm_ref, kmean_ref,
                 qn_ref, qr_ref, kn_ref, kr_ref, vc_ref, carry_ref):
    t = pl.program_id(1)
    bf16 = jnp.bfloat16
    x = x_ref[0]
    shift = mod_ref[0, :, 0:D_MODEL]
    scale = mod_ref[0, :, D_MODEL:2 * D_MODEL]
    h = (_rms(x, gpre_ref[...]) * (1.0 + scale) + shift).astype(bf16)

    def seg(c0, width):
        return _dot(h, win_ref[:, c0:c0 + width])

    qf_ref[0] = (seg(_C_QF, FOX_W) * (FOX_HEAD_DIM ** -0.5 * LOG2E)).astype(bf16)
    kf_ref[0] = seg(_C_KF, FOX_W).astype(bf16)
    vf_ref[0] = seg(_C_VF, FOX_W).astype(bf16)
    qm_ref[0] = (seg(_C_QM, MOBA_W) * (MOBA_HEAD_DIM ** -0.5 * LOG2E)).astype(bf16)
    km = seg(_C_KM, MOBA_W)
    km_ref[0] = km.astype(bf16)
    vm_ref[0] = seg(_C_VM, MOBA_W).astype(bf16)
    kmean_ref[0, 0] = jnp.mean(km.reshape(TM // MOBA_BLOCK, MOBA_BLOCK, MOBA_W), axis=1)

    lane = lax.broadcasted_iota(jnp.int32, (1, V7X_LANES), 1)
    slot = lane % FGATE_SLOT
    used = (lane < FGATE_SLOT * FOX_HEADS) & (slot < 2 * N_SPLIT)
    fl = seg(_C_FG, V7X_LANES) + fb_ref[...]
    logf = jnp.where(used, jnp.minimum(fl, 0.0) - jnp.log1p(jnp.exp(-jnp.abs(fl))), 0.0)

    @pl.when(t == 0)
    def _():
        carry_ref[...] = jnp.zeros_like(carry_ref)

    fcum = _dot(tril_ref[...], jnp.concatenate(_split3(logf), axis=0)) + carry_ref[0:1, :]
    carry_ref[0:1, :] = fcum[TM - 1:TM, :]
    hi, mid, lo = (p.astype(jnp.float32) for p in _split3(fcum * LOG2E))
    parts = jnp.where(slot % N_SPLIT == 0, hi, jnp.where(slot % N_SPLIT == 1, mid, lo))
    eq_ref[0] = jnp.where(used, jnp.where(slot < N_SPLIT, parts, 1.0), 0.0).astype(bf16)
    ek_ref[0] = jnp.where(used, jnp.where(slot < N_SPLIT, 1.0, -parts), 0.0).astype(bf16)

    cos = cos_ref[...]
    sin = sin_ref[...]
    mla_scale = (MLA_NOPE_DIM + MLA_ROPE_DIM) ** -0.5 * LOG2E
    cq = _rms(seg(_C_CQ, MLA_Q_RANK), gq_ref[...]).astype(bf16)
    n_nope = MLA_HEADS * MLA_NOPE_DIM
    n_rot = MLA_HEADS * MLA_ROPE_DIM
    qn_ref[0] = (_dot(cq, wuq_ref[:, 0:n_nope]) * mla_scale).astype(bf16)
    q_rot = _dot(cq, wuq_ref[:, n_nope:n_nope + n_rot])
    q_rot_sw = _dot(cq, wuq_ref[:, n_nope + n_rot:n_nope + 2 * n_rot])
    cos2 = jnp.concatenate([cos] * (n_rot // V7X_LANES), axis=1)
    sin2 = jnp.concatenate([sin] * (n_rot // V7X_LANES), axis=1)
    qr_ref[0] = ((q_rot * cos2 + q_rot_sw * sin2) * mla_scale).astype(bf16)
    ckv = _rms(seg(_C_CKV, MLA_KV_RANK), gkv_ref[...]).astype(bf16)
    kn_ref[0] = _dot(ckv, wukv_ref[:, 0:n_nope]).astype(bf16)
    vc_ref[0] = _dot(ckv, wukv_ref[:, n_nope:n_nope + MLA_W]).astype(bf16)
    kr_ref[0] = (seg(_C_KR, V7X_LANES) * cos + seg(_C_KRS, V7X_LANES) * sin).astype(bf16)


def _proj(x, mod_l, gpre, win, fbias, gq, wuq, gkv, wukv, cos_t, sin_t, tril):
    batch, seq, d = x.shape
    nt = seq // TM
    bf16 = jnp.bfloat16

    def tok(width):
        return pl.BlockSpec((1, TM, width), lambda b, t: (b, t, 0))

    def out(width):
        return jax.ShapeDtypeStruct((batch, seq, width), bf16)

    widths = dict(qf=FOX_W, eq=V7X_LANES, kf=FOX_W, ek=V7X_LANES, vf=FOX_W, qm=MOBA_W, km=MOBA_W, vm=MOBA_W)
    mla_widths = dict(qn=MLA_HEADS * MLA_NOPE_DIM, qr=MLA_HEADS * MLA_ROPE_DIM, kn=MLA_HEADS * MLA_NOPE_DIM,
                      kr=V7X_LANES, vc=MLA_W)
    nb = TM // MOBA_BLOCK
    out_shape = ([out(w) for w in widths.values()]
                 + [jax.ShapeDtypeStruct((batch, nt, nb, MOBA_W), jnp.float32)]
                 + [out(w) for w in mla_widths.values()])
    out_specs = ([tok(w) for w in widths.values()]
                 + [pl.BlockSpec((1, 1, nb, MOBA_W), lambda b, t: (b, t, 0, 0))]
                 + [tok(w) for w in mla_widths.values()])
    res = pl.pallas_call(
        _proj_kernel,
        out_shape=out_shape,
        grid=(batch, nt),
        in_specs=[
            tok(d),
            pl.BlockSpec((1, 1, 6 * d), lambda b, t: (b, 0, 0)),
            _const_spec((1, d)),
            _const_spec(win.shape),
            _const_spec((1, V7X_LANES)),
            _const_spec((1, MLA_Q_RANK)),
            _const_spec(wuq.shape),
            _const_spec((1, MLA_KV_RANK)),
            _const_spec(wukv.shape),
            pl.BlockSpec((TM, V7X_LANES), lambda b, t: (t, 0)),
            pl.BlockSpec((TM, V7X_LANES), lambda b, t: (t, 0)),
            _const_spec(tril.shape),
        ],
        out_specs=out_specs,
        scratch_shapes=[pltpu.VMEM((8, V7X_LANES), jnp.float32)],
        compiler_params=_params("arbitrary", "arbitrary"),
        name="in_proj",
    )(x, mod_l, gpre, win, fbias, gq, wuq, gkv, wukv, cos_t, sin_t, tril)
    names = list(widths) + ["kmean"] + list(mla_widths)
    r = dict(zip(names, res))
    r["kmean"] = r["kmean"].reshape(batch, seq // MOBA_BLOCK, MOBA_W)
    return r


def _transpose_to_bf16(x):
    return x.astype(jnp.float32).T.astype(jnp.bfloat16)


def _head_rows(x_pair, e):
    row = lax.broadcasted_iota(jnp.int32, x_pair.shape, 0)
    keep = (row >= HEAD_LANES) if e else (row < HEAD_LANES)
    return jnp.where(keep, x_pair, 0.0)


def _causal_bias_t():
    key = lax.broadcasted_iota(jnp.int32, (TQ, TQ), 0)
    qry = lax.broadcasted_iota(jnp.int32, (TQ, TQ), 1)
    return jnp.where(key <= qry, 0.0, MASK_NEG)


def _tile_rows(t):
    return pl.ds(pl.multiple_of(t * TQ, TQ), TQ)


class _FlashScratch(NamedTuple):
    vt: object
    rhs: object
    s: object
    p: object
    alpha: object
    m: object
    acc: object


def _flash_begin(v_ref, sc):
    n_heads = sc.vt.shape[1] // V_ROWS
    extra = lax.broadcasted_iota(jnp.int32, (V_ROWS - HEAD_LANES, TQ), 0)
    ones_row = jnp.where(extra == 0, 1.0, 0.0).astype(jnp.bfloat16)
    for c in range(sc.vt.shape[0]):
        v_t = _transpose_to_bf16(v_ref[0, c * TQ:(c + 1) * TQ, :])
        for h in range(n_heads):
            sc.vt[c, h * V_ROWS:h * V_ROWS + HEAD_LANES, :] = v_t[h * HEAD_LANES:(h + 1) * HEAD_LANES]
            sc.vt[c, h * V_ROWS + HEAD_LANES:(h + 1) * V_ROWS, :] = ones_row
    sc.m[...] = jnp.full(sc.m.shape, MASK_NEG, jnp.float32)
    sc.acc[...] = jnp.zeros(sc.acc.shape, jnp.float32)


def _flash_pipeline(tab_ref, segments, heads, n_heads, lhs_tile, sc):
    slot = {h: h - heads[0] for h in heads}
    n_steps = sum(count for count, _ in segments)

    def ij(t):
        t = jnp.clip(t, 0, n_steps - 1)
        return tab_ref[0, t], tab_ref[1, t]

    def scores(t, bias_fn):
        i, j = ij(t)
        lhs = {h // 2: lhs_tile(j, h // 2) for h in heads if h % 2 == 0}
        out = {}
        for h in heads:
            s = _dot(lhs[h // 2], sc.rhs[i * n_heads + h])
            b = bias_fn(i, j, h)
            out[h] = s if b is None else s + b
        return out

    def store_scores(vals):
        for h in heads:
            sc.s[slot[h]] = vals[h]

    def softmax(t):
        i, _ = ij(t)
        s_val = {h: sc.s[slot[h]] for h in heads}
        m_old = {h: sc.m[i * n_heads + h] for h in heads}
        m_new = {h: jnp.maximum(m_old[h], jnp.max(s_val[h], axis=0, keepdims=True)) for h in heads}
        alpha = {h: jnp.exp2(m_old[h] - m_new[h]) for h in heads}
        probs = {h: jnp.exp2(s_val[h] - m_new[h]) for h in heads}
        for h in heads:
            st = i * n_heads + h
            sc.m[st] = m_new[h]
            sc.alpha[slot[h]] = alpha[h]
            sc.p[slot[h]] = probs[h].astype(jnp.bfloat16)

    def values(t):
        i, j = ij(t)
        pv = {h: _dot(sc.vt[j, h * V_ROWS:(h + 1) * V_ROWS, :], sc.p[slot[h]]) for h in heads}
        for h in heads:
            st = i * n_heads + h
            sc.acc[st] = sc.alpha[slot[h]] * sc.acc[st] + pv[h]

    def prologue(_, carry):
        for h in heads:
            sc.alpha[slot[h]] = jnp.ones(sc.alpha.shape[1:], jnp.float32)
            sc.p[slot[h]] = jnp.zeros(sc.p.shape[1:], jnp.bfloat16)
        store_scores(scores(0, segments[0][1]))
        return carry

    lax.fori_loop(0, tab_ref[0, n_steps], prologue, 0)

    first = 0
    for count, bias_fn in segments:

        def body(t, carry, bias_fn=bias_fn):
            nxt = scores(t + 1, bias_fn)
            values(t - 1)
            softmax(t)
            store_scores(nxt)
            return carry

        lax.fori_loop(max(first - 1, 0), first + count - 1, body, 0)
        first += count
    values(n_steps - 2)
    softmax(n_steps - 1)
    values(n_steps - 1)


def _flash_finish(n_heads, sc, g_ref, o_ref):
    def body(iq, carry):
        heads = []
        for h in range(n_heads):
            acc = sc.acc[iq * n_heads + h]
            heads.append(acc[0:HEAD_LANES] * (1.0 / acc[HEAD_LANES:HEAD_LANES + 1]))
        o_t = jnp.concatenate(heads, axis=0)
        o_ref[0, _tile_rows(iq), :] = _rms(o_t.T, g_ref[...]).astype(o_ref.dtype)
        return carry

    lax.fori_loop(0, o_ref.shape[1] // TQ, body, 0)


def _causal_segments(n_tiles):
    return [(n_tiles * (n_tiles - 1) // 2, lambda i, j, h: None), (n_tiles, lambda i, j, h: _causal_bias_t())]


def _fox_kernel(tab_ref, q_ref, eq_ref, k_ref, ek_ref, v_ref, g_ref, o_ref, *scratch):
    sc = _FlashScratch(*scratch)
    n_tiles = q_ref.shape[1] // TQ
    _flash_begin(v_ref, sc)

    def build_rhs(iq, carry):
        rows = _tile_rows(iq)
        q_t = q_ref[0, rows, :].astype(jnp.float32).T
        eq_t = eq_ref[0, rows, :].astype(jnp.float32).T
        slot_head = lax.broadcasted_iota(jnp.int32, eq_t.shape, 0) // FGATE_SLOT
        for h in range(FOX_HEADS):
            pair, e = divmod(h, 2)
            q_h = _head_rows(q_t[pair * V7X_LANES:(pair + 1) * V7X_LANES], e)
            e_h = jnp.where(slot_head == h, eq_t, 0.0)
            sc.rhs[iq * FOX_HEADS + h] = jnp.concatenate([q_h, e_h], axis=0).astype(jnp.bfloat16)
        return carry

    lax.fori_loop(0, n_tiles, build_rhs, 0)

    def lhs_tile(j, pair):
        rows = _tile_rows(j)
        return jnp.concatenate([k_ref[0, rows, pair * V7X_LANES:(pair + 1) * V7X_LANES], ek_ref[0, rows, :]], axis=1)

    _flash_pipeline(tab_ref, _causal_segments(n_tiles), range(FOX_HEADS), FOX_HEADS, lhs_tile, sc)
    _flash_finish(FOX_HEADS, sc, g_ref, o_ref)


def _moba_kernel(tab_ref, q_ref, k_ref, v_ref, kmean_ref, bias_ref, g_ref, o_ref, *scratch):
    sc = _FlashScratch(*scratch)
    f32, bf16 = jnp.float32, jnp.bfloat16
    n_tiles = q_ref.shape[1] // TQ
    n_blocks = kmean_ref.shape[1]
    _flash_begin(v_ref, sc)
    sel_rows = 16
    blk = lax.broadcasted_iota(jnp.int32, (sel_rows, TQ), 0)
    lane = lax.broadcasted_iota(jnp.int32, (1, V7X_LANES), 1)

    def build_rhs(iq, carry):
        q_t = q_ref[0, _tile_rows(iq), :].astype(f32).T
        for h in range(MOBA_HEADS):
            pair, e = divmod(h, 2)
            q_h = _head_rows(q_t[pair * V7X_LANES:(pair + 1) * V7X_LANES], e)
            q_hb = q_h.astype(bf16)
            kmean = kmean_ref[0, :, pair * V7X_LANES:(pair + 1) * V7X_LANES]
            km_hi = kmean.astype(bf16).astype(f32)
            gate_lhs = jnp.concatenate([jnp.concatenate([km_hi, kmean - km_hi], axis=1),
                                        jnp.zeros((sel_rows - n_blocks, 2 * V7X_LANES), f32)], axis=0).astype(bf16)
            gate = _dot(gate_lhs, jnp.concatenate([q_hb, q_hb], axis=0))
            beaten = jnp.zeros((sel_rows, TQ), f32)
            for mblk in range(n_blocks - 1):
                gm = gate[mblk:mblk + 1, :]
                wins = (gm > gate) | ((gm == gate) & (mblk < blk))
                beaten = beaten + jnp.where(wins, jnp.where(mblk < iq, 1.0, 0.0), 0.0)
            keep = ((blk < iq) & (beaten < MOBA_TOPK)) | (blk == iq)
            sel = jnp.where(keep, 0.0, MASK_NEG)
            pad = jnp.zeros((V7X_LANES - sel_rows, TQ), f32)
            sc.rhs[iq * MOBA_HEADS + h] = jnp.concatenate([q_h, sel, pad], axis=0).astype(bf16)
        return carry

    lax.fori_loop(0, n_tiles, build_rhs, 0)

    def lhs_tile(j, pair):
        onehot = jnp.broadcast_to(jnp.where(lane == j, 1.0, 0.0).astype(bf16), (TQ, V7X_LANES))
        return jnp.concatenate([k_ref[0, _tile_rows(j), pair * V7X_LANES:(pair + 1) * V7X_LANES], onehot], axis=1)

    def bias_fn(i, j, h):
        return bias_ref[h, jnp.minimum(i - j, 2)]

    segments = [(n_tiles * (n_tiles + 1) // 2, bias_fn)]
    _flash_pipeline(tab_ref, segments, range(MOBA_HEADS), MOBA_HEADS, lhs_tile, sc)
    _flash_finish(MOBA_HEADS, sc, g_ref, o_ref)


def _mla_kernel(tab_ref, qn_ref, qr_ref, kn_ref, kr_ref, v_ref, g_ref, o_ref, *scratch):
    sc = _FlashScratch(*scratch)
    n_tiles = qn_ref.shape[1] // TQ
    _flash_begin(v_ref, sc)
    heads_per_rot = V7X_LANES // MLA_ROPE_DIM
    rot_slot = lax.broadcasted_iota(jnp.int32, (V7X_LANES, TQ), 0) // MLA_ROPE_DIM

    def build_rhs(iq, carry):
        rows = _tile_rows(iq)
        qn_t = qn_ref[0, rows, :].astype(jnp.float32).T
        qr_t = qr_ref[0, rows, :].astype(jnp.float32).T
        for h in range(MLA_HEADS):
            pair, e = divmod(h, 2)
            quad, slot = divmod(h, heads_per_rot)
            q_h = _head_rows(qn_t[pair * V7X_LANES:(pair + 1) * V7X_LANES], e)
            r_h = jnp.where(rot_slot == slot, qr_t[quad * V7X_LANES:(quad + 1) * V7X_LANES], 0.0)
            sc.rhs[iq * MLA_HEADS + h] = jnp.concatenate([q_h, r_h], axis=0).astype(jnp.bfloat16)
        return carry

    lax.fori_loop(0, n_tiles, build_rhs, 0)

    def lhs_tile(j, pair):
        rows = _tile_rows(j)
        return jnp.concatenate([kn_ref[0, rows, pair * V7X_LANES:(pair + 1) * V7X_LANES], kr_ref[0, rows, :]], axis=1)

    for h0 in range(0, MLA_HEADS, PIPE_HEADS):
        _flash_pipeline(tab_ref, _causal_segments(n_tiles), range(h0, h0 + PIPE_HEADS), MLA_HEADS, lhs_tile, sc)
    _flash_finish(MLA_HEADS, sc, g_ref, o_ref)


def _tile_pairs(n_tiles, diagonal_last):
    if diagonal_last:
        pairs = [(i, j) for i in range(n_tiles) for j in range(i)] + [(i, i) for i in range(n_tiles)]
    else:
        pairs = [(i, j) for i in range(n_tiles) for j in range(i + 1)]
    return jnp.asarray(np.array(pairs + [(1, 1)], np.int32).T)


def _attn_call(kernel, name, n_heads, diagonal_last, arrays, const_arrays):
    batch, seq, _ = arrays[0].shape
    n_tiles = seq // TQ
    out_width = n_heads * HEAD_LANES
    row = lambda a: pl.BlockSpec((1,) + a.shape[1:], lambda b: (b, 0, 0))
    return pl.pallas_call(
        kernel,
        out_shape=jax.ShapeDtypeStruct((batch, seq, out_width), jnp.bfloat16),
        grid=(batch,),
        in_specs=([pl.BlockSpec(memory_space=pltpu.SMEM)] + [row(a) for a in arrays]
                  + [_const_spec(a.shape) for a in const_arrays]),
        out_specs=pl.BlockSpec((1, seq, out_width), lambda b: (b, 0, 0)),
        scratch_shapes=[
            pltpu.VMEM((n_tiles, n_heads * V_ROWS, TQ), jnp.bfloat16),
            pltpu.VMEM((n_tiles * n_heads, 2 * V7X_LANES, TQ), jnp.bfloat16),
            pltpu.VMEM((PIPE_HEADS, TQ, TQ), jnp.float32),
            pltpu.VMEM((PIPE_HEADS, TQ, TQ), jnp.bfloat16),
            pltpu.VMEM((PIPE_HEADS, 1, TQ), jnp.float32),
            pltpu.VMEM((n_tiles * n_heads, 1, TQ), jnp.float32),
            pltpu.VMEM((n_tiles * n_heads, V_ROWS, TQ), jnp.float32),
        ],
        compiler_params=_params("arbitrary"),
        name=name,
    )(_tile_pairs(n_tiles, diagonal_last), *arrays, *const_arrays)


def _bias_tile_kernel(table_ref, bucket_ref, o_ref):
    h = pl.program_id(0)
    for k in range(bucket_ref.shape[0]):
        bkt = bucket_ref[k]
        tile = jnp.full(bkt.shape, MASK_NEG, jnp.float32)
        for b in range(T5_BUCKETS):
            tile = jnp.where(bkt == b, table_ref[h, b] * LOG2E, tile)
        o_ref[0, k] = tile


def _moba_bias_tiles(t5_table):
    bucket = _t5_bucket_table(3 * TQ)
    key = np.arange(TQ)[:, None]
    qry = np.arange(TQ)[None, :]
    kinds = []
    for k in range(3):
        dist = qry - key + k * TQ
        kinds.append(np.where(dist >= 0, bucket[np.maximum(dist, 0)], -1))
    buckets = jnp.asarray(np.stack(kinds), jnp.int32)
    return pl.pallas_call(
        _bias_tile_kernel,
        out_shape=jax.ShapeDtypeStruct((MOBA_HEADS, 3, TQ, TQ), jnp.float32),
        grid=(MOBA_HEADS,),
        in_specs=[pl.BlockSpec(memory_space=pltpu.SMEM), _const_spec(buckets.shape)],
        out_specs=pl.BlockSpec((1, 3, TQ, TQ), lambda h: (h, 0, 0, 0)),
        compiler_params=_params("arbitrary"),
        name="t5_bias_tiles",
    )(t5_table.T, buckets)


def _out_kernel(x_ref, of_ref, om_ref, oc_ref, mod_ref, gpost_ref, w_ref, o_ref):
    o = jnp.concatenate([of_ref[0], om_ref[0], oc_ref[0]], axis=1)
    y = _dot(o, w_ref[...])
    gate = mod_ref[0, :, 2 * D_MODEL:3 * D_MODEL]
    o_ref[0] = x_ref[0] + gate * _rms(y, gpost_ref[...])


def _out_proj(x, o_f, o_m, o_c, mod_l, gpost, w_out):
    batch, seq, d = x.shape

    def tok(width):
        return pl.BlockSpec((1, TM, width), lambda b, t: (b, t, 0))

    return pl.pallas_call(
        _out_kernel,
        out_shape=jax.ShapeDtypeStruct(x.shape, x.dtype),
        grid=(batch, seq // TM),
        in_specs=[tok(d), tok(FOX_W), tok(MOBA_W), tok(MLA_W),
                  pl.BlockSpec((1, 1, 6 * d), lambda b, t: (b, 0, 0)),
                  _const_spec((1, d)), _const_spec(w_out.shape)],
        out_specs=tok(d),
        compiler_params=_params("arbitrary", "arbitrary"),
        name="out_proj",
    )(x, o_f, o_m, o_c, mod_l, gpost, w_out)


def _ffn_kernel(x_ref, mod_ref, gpre_ref, gpost_ref, wgu_ref, wd_ref, o_ref):
    bf16 = jnp.bfloat16
    x = x_ref[0]
    shift = mod_ref[0, :, 3 * D_MODEL:4 * D_MODEL]
    scale = mod_ref[0, :, 4 * D_MODEL:5 * D_MODEL]
    gate = mod_ref[0, :, 5 * D_MODEL:6 * D_MODEL]
    h = (_rms(x, gpre_ref[...]) * (1.0 + scale) + shift).astype(bf16)
    acc = None
    for c0, c1 in FFN_CHUNKS:
        g = _dot(h, wgu_ref[:, c0:c1])
        u = _dot(h, wgu_ref[:, D_FF + c0:D_FF + c1])
        a = (g * jax.nn.sigmoid(g) * u).astype(bf16)
        part = _dot(a, wd_ref[c0:c1, :])
        acc = part if acc is None else acc + part
    o_ref[0] = x + gate * _rms(acc, gpost_ref[...])


def _ffn(x, mod_l, gpre, gpost, wgu, wd):
    batch, seq, d = x.shape
    tok = pl.BlockSpec((1, TM, d), lambda b, t: (b, t, 0))
    return pl.pallas_call(
        _ffn_kernel,
        out_shape=jax.ShapeDtypeStruct(x.shape, x.dtype),
        grid=(batch, seq // TM),
        in_specs=[tok, pl.BlockSpec((1, 1, 6 * d), lambda b, t: (b, 0, 0)),
                  _const_spec((1, d)), _const_spec((1, d)), _const_spec(wgu.shape), _const_spec(wd.shape)],
        out_specs=tok,
        compiler_params=_params("arbitrary", "arbitrary"),
        name="ffn",
    )(x, mod_l, gpre, gpost, wgu, wd)


def _rope_tables(seq):
    half = MLA_ROPE_DIM // 2
    inv_freq = 1.0 / (ROPE_THETA ** (jnp.arange(half, dtype=jnp.float32) / half))
    ang = jnp.arange(seq).astype(jnp.float32)[:, None] * inv_freq[None, :]
    reps = V7X_LANES // MLA_ROPE_DIM
    cos = jnp.tile(jnp.concatenate([jnp.cos(ang), jnp.cos(ang)], axis=1), (1, reps))
    sin = jnp.tile(jnp.concatenate([-jnp.sin(ang), jnp.sin(ang)], axis=1), (1, reps))
    return cos, sin


def kernel(x, c, t5_table, w_ada, b_ada, g_mix_pre, g_mix_post, w_in, b_forget, g_q_lat, w_uq, g_kv_lat, w_ukv,
           g_group, w_out, g_ffn_pre, g_ffn_post, w_gate_up, w_down):
    batch, seq, d = x.shape
    assert d == D_MODEL and seq % TM == 0 and TM % MOBA_BLOCK == 0 and TQ == MOBA_BLOCK
    bf16 = jnp.bfloat16
    in_cols, uq_cols, ukv_cols = _in_proj_columns(), _uq_columns(), _ukv_columns()
    cos_t, sin_t = _rope_tables(seq)
    tril = np.tril(np.ones((TM, TM), np.float32))
    tril = jnp.asarray(np.concatenate([tril] * N_SPLIT, axis=1), bf16)
    moba_bias = _moba_bias_tiles(t5_table)
    fg_lane = np.arange(V7X_LANES)
    fg_used = (fg_lane < FGATE_SLOT * FOX_HEADS) & (fg_lane % FGATE_SLOT < 2 * N_SPLIT)
    fg_head = np.minimum(fg_lane // FGATE_SLOT, FOX_HEADS - 1)

    mod = _ada_mod(c, w_ada, b_ada)
    for l in range(DEPTH):
        mod_l = mod[l].reshape(batch, 1, 6 * d)
        fbias = jnp.where(jnp.asarray(fg_used), b_forget[l][fg_head], 0.0).reshape(1, V7X_LANES)
        pr = _proj(x, mod_l, g_mix_pre[l].reshape(1, d),
                   _take_columns(w_in[l], in_cols).astype(bf16), fbias,
                   g_q_lat[l].reshape(1, -1), _take_columns(w_uq[l], uq_cols).astype(bf16),
                   g_kv_lat[l].reshape(1, -1), _take_columns(w_ukv[l], ukv_cols).astype(bf16),
                   cos_t, sin_t, tril)
        g_a = g_group[l, :FOX_W].reshape(1, -1)
        g_b = g_group[l, FOX_W:FOX_W + MOBA_W].reshape(1, -1)
        g_c = g_group[l, FOX_W + MOBA_W:].reshape(1, -1)
        o_f = _attn_call(_fox_kernel, "fox_attn", FOX_HEADS, True,
                         [pr["qf"], pr["eq"], pr["kf"], pr["ek"], pr["vf"]], [g_a])
        o_m = _attn_call(_moba_kernel, "moba_attn", MOBA_HEADS, False,
                         [pr["qm"], pr["km"], pr["vm"], pr["kmean"]], [moba_bias, g_b])
        o_c = _attn_call(_mla_kernel, "mla_attn", MLA_HEADS, True,
                         [pr["qn"], pr["qr"], pr["kn"], pr["kr"], pr["vc"]], [g_c])
        x = _out_proj(x, o_f, o_m, o_c, mod_l, g_mix_post[l].reshape(1, d), w_out[l].astype(bf16))
        x = _ffn(x, mod_l, g_ffn_pre[l].reshape(1, d), g_ffn_post[l].reshape(1, d),
                 w_gate_up[l].astype(bf16), w_down[l].astype(bf16))
    return x
```

```python
import math
from typing import NamedTuple

import jax
import jax.numpy as jnp
import numpy as np
from jax import lax
from jax.experimental import pallas as pl
from jax.experimental.pallas import tpu as pltpu

D_MODEL = 1024
DEPTH = 2
FOX_HEADS = 4
FOX_HEAD_DIM = 64
MOBA_HEADS = 4
MOBA_HEAD_DIM = 64
MOBA_BLOCK = 256
MOBA_TOPK = 3
MLA_HEADS = 8
MLA_NOPE_DIM = 64
MLA_ROPE_DIM = 32
MLA_V_DIM = 64
MLA_Q_RANK = 256
MLA_KV_RANK = 128
ROPE_THETA = 10000.0
T5_BUCKETS = 32
T5_MAX_DISTANCE = 128
D_FF = -(-8 * D_MODEL // (3 * 256)) * 256
RMS_EPS = 1e-6
FOX_W = FOX_HEADS * FOX_HEAD_DIM
MOBA_W = MOBA_HEADS * MOBA_HEAD_DIM
MLA_W = MLA_HEADS * MLA_V_DIM
MIX_WIDTH = FOX_W + MOBA_W + MLA_W
IN_SIZES = (FOX_W, FOX_W, FOX_W, FOX_HEADS, MOBA_W, MOBA_W, MOBA_W, MLA_Q_RANK, MLA_KV_RANK, MLA_ROPE_DIM)

V7X_LANES = 128
V7X_VMEM_LIMIT_BYTES = 56 * 1024 * 1024

TM = 512
TQ = 256
PIPE_HEADS = 4
FFN_CHUNKS = ((0, 1024), (1024, 2048), (2048, D_FF))

HEAD_LANES = 64
MASK_NEG = -1e30
LOG2E = math.log2(math.e)
V_ROWS = 80
FGATE_SLOT = 8
N_SPLIT = 3

_C_QF, _C_KF, _C_VF = 0, 256, 512
_C_QM, _C_KM, _C_VM = 768, 1024, 1280
_C_CQ, _C_CKV = 1536, 1792
_C_FG, _C_KR, _C_KRS = 1920, 2048, 2176
IN_WIDTH_PADDED = 2304


def _in_proj_columns():
    off = np.cumsum((0,) + IN_SIZES)
    q_f, k_f, v_f, f_g, q_m, k_m, v_m, c_q, c_kv, k_r = (np.arange(off[i], off[i + 1]) for i in range(10))
    fg = np.full((V7X_LANES,), -1, np.int64)
    for h in range(FOX_HEADS):
        fg[FGATE_SLOT * h:FGATE_SLOT * h + 2 * N_SPLIT] = f_g[h]
    half = MLA_ROPE_DIM // 2
    kr4 = np.tile(k_r, V7X_LANES // MLA_ROPE_DIM)
    kr4s = np.tile(np.roll(k_r, -half), V7X_LANES // MLA_ROPE_DIM)
    cols = np.concatenate([q_f, k_f, v_f, q_m, k_m, v_m, c_q, c_kv, fg, kr4, kr4s])
    assert cols.shape == (IN_WIDTH_PADDED,)
    return cols


def _uq_columns():
    per = MLA_NOPE_DIM + MLA_ROPE_DIM
    half = MLA_ROPE_DIM // 2
    nope = np.concatenate([np.arange(h * per, h * per + MLA_NOPE_DIM) for h in range(MLA_HEADS)])
    rot = np.concatenate([np.arange(h * per + MLA_NOPE_DIM, (h + 1) * per) for h in range(MLA_HEADS)])
    rots = np.concatenate([np.roll(np.arange(h * per + MLA_NOPE_DIM, (h + 1) * per), -half) for h in range(MLA_HEADS)])
    return np.concatenate([nope, rot, rots])


def _ukv_columns():
    per = MLA_NOPE_DIM + MLA_V_DIM
    nope = np.concatenate([np.arange(h * per, h * per + MLA_NOPE_DIM) for h in range(MLA_HEADS)])
    val = np.concatenate([np.arange(h * per + MLA_NOPE_DIM, (h + 1) * per) for h in range(MLA_HEADS)])
    return np.concatenate([nope, val])


def _take_columns(w, cols):
    g = jnp.take(w, jnp.asarray(np.maximum(cols, 0), jnp.int32), axis=1)
    return jnp.where(jnp.asarray(cols >= 0)[None, :], g, 0.0)


def _t5_bucket_table(n):
    d = np.arange(n, dtype=np.int32)
    max_exact = T5_BUCKETS // 2
    nf = np.maximum(d, max_exact).astype(np.float32)
    ratio = np.log(nf / np.float32(max_exact)) / np.float32(math.log(T5_MAX_DISTANCE / max_exact))
    large = max_exact + (ratio.astype(np.float32) * np.float32(T5_BUCKETS - max_exact)).astype(np.int32)
    large = np.minimum(large, T5_BUCKETS - 1)
    return np.where(d < max_exact, d, large).astype(np.int32)


def _const_spec(shape):
    nd = len(shape)
    return pl.BlockSpec(shape, lambda *_: (0,) * nd, pipeline_mode=pl.Buffered(1))


def _params(*sem):
    return pltpu.CompilerParams(dimension_semantics=sem, vmem_limit_bytes=V7X_VMEM_LIMIT_BYTES)


def _rms(x, g):
    return x * lax.rsqrt(jnp.mean(x * x, axis=-1, keepdims=True) + RMS_EPS) * g


def _split3(v):
    hi = v.astype(jnp.bfloat16)
    r1 = v - hi.astype(jnp.float32)
    mid = r1.astype(jnp.bfloat16)
    lo = (r1 - mid.astype(jnp.float32)).astype(jnp.bfloat16)
    return hi, mid, lo


def _dot(a, b):
    return jnp.dot(a, b, preferred_element_type=jnp.float32)


def _ada_kernel(c_ref, w_ref, b_ref, o_ref):
    c = c_ref[...]
    act = (c * jax.nn.sigmoid(c)).astype(jnp.bfloat16)
    o_ref[0] = _dot(act, w_ref[0].astype(jnp.bfloat16)) + b_ref[0]


def _ada_mod(c, w_ada, b_ada):
    depth, d, six_d = w_ada.shape
    batch = c.shape[0]
    n_col = six_d // d
    return pl.pallas_call(
        _ada_kernel,
        out_shape=jax.ShapeDtypeStruct((depth, batch, six_d), jnp.float32),
        grid=(depth, n_col),
        in_specs=[
            pl.BlockSpec((batch, d), lambda l, j: (0, 0)),
            pl.BlockSpec((1, d, d), lambda l, j: (l, 0, j)),
            pl.BlockSpec((1, 1, d), lambda l, j: (l, 0, j)),
        ],
        out_specs=pl.BlockSpec((1, batch, d), lambda l, j: (l, 0, j)),
        compiler_params=_params("arbitrary", "arbitrary"),
        name="ada_mod",
    )(c, w_ada, b_ada.reshape(depth, 1, six_d))


def _proj_kernel(x_ref, mod_ref, gpre_ref, win_ref, fb_ref, gq_ref, wuq_ref, gkv_ref, wukv_ref,
                 cos_ref, sin_ref, tril_ref,
                 qf_ref, eq_ref, kf_ref, ek_ref, vf_ref, qm_ref, km_ref, vm_ref, kmean_ref,
                 qn_ref, qr_ref, kn_ref, kr_ref, vc_ref, carry_ref):
    t = pl.program_id(1)
    bf16 = jnp.bfloat16
    x = x_ref[0]
    shift = mod_ref[0, :, 0:D_MODEL]
    scale = mod_ref[0, :, D_MODEL:2 * D_MODEL]
    h = (_rms(x, gpre_ref[...]) * (1.0 + scale) + shift).astype(bf16)

    def seg(c0, width):
        return _dot(h, win_ref[:, c0:c0 + width])

    qf_ref[0] = (seg(_C_QF, FOX_W) * (FOX_HEAD_DIM ** -0.5 * LOG2E)).astype(bf16)
    kf_ref[0] = seg(_C_KF, FOX_W).astype(bf16)
    vf_ref[0] = seg(_C_VF, FOX_W).astype(bf16)
    qm_ref[0] = (seg(_C_QM, MOBA_W) * (MOBA_HEAD_DIM ** -0.5 * LOG2E)).astype(bf16)
    km = seg(_C_KM, MOBA_W)
    km_ref[0] = km.astype(bf16)
    vm_ref[0] = seg(_C_VM, MOBA_W).astype(bf16)
    kmean_ref[0, 0] = jnp.mean(km.reshape(TM // MOBA_BLOCK, MOBA_BLOCK, MOBA_W), axis=1)

    lane = lax.broadcasted_iota(jnp.int32, (1, V7X_LANES), 1)
    slot = lane % FGATE_SLOT
    used = (lane < FGATE_SLOT * FOX_HEADS) & (slot < 2 * N_SPLIT)
    fl = seg(_C_FG, V7X_LANES) + fb_ref[...]
    logf = jnp.where(used, jnp.minimum(fl, 0.0) - jnp.log1p(jnp.exp(-jnp.abs(fl))), 0.0)

    @pl.when(t == 0)
    def _():
        carry_ref[...] = jnp.zeros_like(carry_ref)

    fcum = _dot(tril_ref[...], jnp.concatenate(_split3(logf), axis=0)) + carry_ref[0:1, :]
    carry_ref[0:1, :] = fcum[TM - 1:TM, :]
    hi, mid, lo = (p.astype(jnp.float32) for p in _split3(fcum * LOG2E))
    parts = jnp.where(slot % N_SPLIT == 0, hi, jnp.where(slot % N_SPLIT == 1, mid, lo))
    eq_ref[0] = jnp.where(used, jnp.where(slot < N_SPLIT, parts, 1.0), 0.0).astype(bf16)
    ek_ref[0] = jnp.where(used, jnp.where(slot < N_SPLIT, 1.0, -parts), 0.0).astype(bf16)

    cos = cos_ref[...]
    sin = sin_ref[...]
    mla_scale = (MLA_NOPE_DIM + MLA_ROPE_DIM) ** -0.5 * LOG2E
    cq = _rms(seg(_C_CQ, MLA_Q_RANK), gq_ref[...]).astype(bf16)
    n_nope = MLA_HEADS * MLA_NOPE_DIM
    n_rot = MLA_HEADS * MLA_ROPE_DIM
    qn_ref[0] = (_dot(cq, wuq_ref[:, 0:n_nope]) * mla_scale).astype(bf16)
    q_rot = _dot(cq, wuq_ref[:, n_nope:n_nope + n_rot])
    q_rot_sw = _dot(cq, wuq_ref[:, n_nope + n_rot:n_nope + 2 * n_rot])
    cos2 = jnp.concatenate([cos] * (n_rot // V7X_LANES), axis=1)
    sin2 = jnp.concatenate([sin] * (n_rot // V7X_LANES), axis=1)
    qr_ref[0] = ((q_rot * cos2 + q_rot_sw * sin2) * mla_scale).astype(bf16)
    ckv = _rms(seg(_C_CKV, MLA_KV_RANK), gkv_ref[...]).astype(bf16)
    kn_ref[0] = _dot(ckv, wukv_ref[:, 0:n_nope]).astype(bf16)
    vc_ref[0] = _dot(ckv, wukv_ref[:, n_nope:n_nope + MLA_W]).astype(bf16)
    kr_ref[0] = (seg(_C_KR, V7X_LANES) * cos + seg(_C_KRS, V7X_LANES) * sin).astype(bf16)


def _proj(x, mod_l, gpre, win, fbias, gq, wuq, gkv, wukv, cos_t, sin_t, tril):
    batch, seq, d = x.shape
    nt = seq // TM
    bf16 = jnp.bfloat16

    def tok(width):
        return pl.BlockSpec((1, TM, width), lambda b, t: (b, t, 0))

    def out(width):
        return jax.ShapeDtypeStruct((batch, seq, width), bf16)

    widths = dict(qf=FOX_W, eq=V7X_LANES, kf=FOX_W, ek=V7X_LANES, vf=FOX_W, qm=MOBA_W, km=MOBA_W, vm=MOBA_W)
    mla_widths = dict(qn=MLA_HEADS * MLA_NOPE_DIM, qr=MLA_HEADS * MLA_ROPE_DIM, kn=MLA_HEADS * MLA_NOPE_DIM,
                      kr=V7X_LANES, vc=MLA_W)
    nb = TM // MOBA_BLOCK
    out_shape = ([out(w) for w in widths.values()]
                 + [jax.ShapeDtypeStruct((batch, nt, nb, MOBA_W), jnp.float32)]
                 + [out(w) for w in mla_widths.values()])
    out_specs = ([tok(w) for w in widths.values()]
                 + [pl.BlockSpec((1, 1, nb, MOBA_W), lambda b, t: (b, t, 0, 0))]
                 + [tok(w) for w in mla_widths.values()])
    res = pl.pallas_call(
        _proj_kernel,
        out_shape=out_shape,
        grid=(batch, nt),
        in_specs=[
            tok(d),
            pl.BlockSpec((1, 1, 6 * d), lambda b, t: (b, 0, 0)),
            _const_spec((1, d)),
            _const_spec(win.shape),
            _const_spec((1, V7X_LANES)),
            _const_spec((1, MLA_Q_RANK)),
            _const_spec(wuq.shape),
            _const_spec((1, MLA_KV_RANK)),
            _const_spec(wukv.shape),
            pl.BlockSpec((TM, V7X_LANES), lambda b, t: (t, 0)),
            pl.BlockSpec((TM, V7X_LANES), lambda b, t: (t, 0)),
            _const_spec(tril.shape),
        ],
        out_specs=out_specs,
        scratch_shapes=[pltpu.VMEM((8, V7X_LANES), jnp.float32)],
        compiler_params=_params("arbitrary", "arbitrary"),
        name="in_proj",
    )(x, mod_l, gpre, win, fbias, gq, wuq, gkv, wukv, cos_t, sin_t, tril)
    names = list(widths) + ["kmean"] + list(mla_widths)
    r = dict(zip(names, res))
    r["kmean"] = r["kmean"].reshape(batch, seq // MOBA_BLOCK, MOBA_W)
    return r


def _transpose_to_bf16(x):
    return x.astype(jnp.float32).T.astype(jnp.bfloat16)


def _head_rows(x_pair, e):
    row = lax.broadcasted_iota(jnp.int32, x_pair.shape, 0)
    keep = (row >= HEAD_LANES) if e else (row < HEAD_LANES)
    return jnp.where(keep, x_pair, 0.0)


def _causal_bias_t():
    key = lax.broadcasted_iota(jnp.int32, (TQ, TQ), 0)
    qry = lax.broadcasted_iota(jnp.int32, (TQ, TQ), 1)
    return jnp.where(key <= qry, 0.0, MASK_NEG)


def _tile_rows(t):
    return pl.ds(pl.multiple_of(t * TQ, TQ), TQ)


class _FlashScratch(NamedTuple):
    vt: object
    rhs: object
    s: object
    p: object
    alpha: object
    m: object
    acc: object


def _flash_begin(v_ref, sc):
    n_heads = sc.vt.shape[1] // V_ROWS
    extra = lax.broadcasted_iota(jnp.int32, (V_ROWS - HEAD_LANES, TQ), 0)
    ones_row = jnp.where(extra == 0, 1.0, 0.0).astype(jnp.bfloat16)
    for c in range(sc.vt.shape[0]):
        v_t = _transpose_to_bf16(v_ref[0, c * TQ:(c + 1) * TQ, :])
        for h in range(n_heads):
            sc.vt[c, h * V_ROWS:h * V_ROWS + HEAD_LANES, :] = v_t[h * HEAD_LANES:(h + 1) * HEAD_LANES]
            sc.vt[c, h * V_ROWS + HEAD_LANES:(h + 1) * V_ROWS, :] = ones_row
    sc.m[...] = jnp.full(sc.m.shape, MASK_NEG, jnp.float32)
    sc.acc[...] = jnp.zeros(sc.acc.shape, jnp.float32)


def _flash_pipeline(tab_ref, segments, heads, n_heads, lhs_tile, sc):
    slot = {h: h - heads[0] for h in heads}
    n_steps = sum(count for count, _ in segments)

    def ij(t):
        t = jnp.clip(t, 0, n_steps - 1)
        return tab_ref[0, t], tab_ref[1, t]

    def scores(t, bias_fn):
        i, j = ij(t)
        lhs = {h // 2: lhs_tile(j, h // 2) for h in heads if h % 2 == 0}
        out = {}
        for h in heads:
            s = _dot(lhs[h // 2], sc.rhs[i * n_heads + h])
            b = bias_fn(i, j, h)
            out[h] = s if b is None else s + b
        return out

    def store_scores(vals):
        for h in heads:
            sc.s[slot[h]] = vals[h]

    def softmax(t):
        i, _ = ij(t)
        s_val = {h: sc.s[slot[h]] for h in heads}
        m_old = {h: sc.m[i * n_heads + h] for h in heads}
        m_new = {h: jnp.maximum(m_old[h], jnp.max(s_val[h], axis=0, keepdims=True)) for h in heads}
        alpha = {h: jnp.exp2(m_old[h] - m_new[h]) for h in heads}
        probs = {h: jnp.exp2(s_val[h] - m_new[h]) for h in heads}
        for h in heads:
            st = i * n_heads + h
            sc.m[st] = m_new[h]
            sc.alpha[slot[h]] = alpha[h]
            sc.p[slot[h]] = probs[h].astype(jnp.bfloat16)

    def values(t):
        i, j = ij(t)
        pv = {h: _dot(sc.vt[j, h * V_ROWS:(h + 1) * V_ROWS, :], sc.p[slot[h]]) for h in heads}
        for h in heads:
            st = i * n_heads + h
            sc.acc[st] = sc.alpha[slot[h]] * sc.acc[st] + pv[h]

    def prologue(_, carry):
        for h in heads:
            sc.alpha[slot[h]] = jnp.ones(sc.alpha.shape[1:], jnp.float32)
            sc.p[slot[h]] = jnp.zeros(sc.p.shape[1:], jnp.bfloat16)
        store_scores(scores(0, segments[0][1]))
        return carry

    lax.fori_loop(0, tab_ref[0, n_steps], prologue, 0)

    first = 0
    for count, bias_fn in segments:

        def body(t, carry, bias_fn=bias_fn):
            nxt = scores(t + 1, bias_fn)
            values(t - 1)
            softmax(t)
            store_scores(nxt)
            return carry

        lax.fori_loop(max(first - 1, 0), first + count - 1, body, 0)
        first += count
    values(n_steps - 2)
    softmax(n_steps - 1)
    values(n_steps - 1)


def _flash_finish(n_heads, sc, g_ref, o_ref):
    def body(iq, carry):
        heads = []
        for h in range(n_heads):
            acc = sc.acc[iq * n_heads + h]
            heads.append(acc[0:HEAD_LANES] * (1.0 / acc[HEAD_LANES:HEAD_LANES + 1]))
        o_t = jnp.concatenate(heads, axis=0)
        o_ref[0, _tile_rows(iq), :] = _rms(o_t.T, g_ref[...]).astype(o_ref.dtype)
        return carry

    lax.fori_loop(0, o_ref.shape[1] // TQ, body, 0)


def _causal_segments(n_tiles):
    return [(n_tiles * (n_tiles - 1) // 2, lambda i, j, h: None), (n_tiles, lambda i, j, h: _causal_bias_t())]


def _fox_kernel(tab_ref, q_ref, eq_ref, k_ref, ek_ref, v_ref, g_ref, o_ref, *scratch):
    sc = _FlashScratch(*scratch)
    n_tiles = q_ref.shape[1] // TQ
    _flash_begin(v_ref, sc)

    def build_rhs(iq, carry):
        rows = _tile_rows(iq)
        q_t = q_ref[0, rows, :].astype(jnp.float32).T
        eq_t = eq_ref[0, rows, :].astype(jnp.float32).T
        slot_head = lax.broadcasted_iota(jnp.int32, eq_t.shape, 0) // FGATE_SLOT
        for h in range(FOX_HEADS):
            pair, e = divmod(h, 2)
            q_h = _head_rows(q_t[pair * V7X_LANES:(pair + 1) * V7X_LANES], e)
            e_h = jnp.where(slot_head == h, eq_t, 0.0)
            sc.rhs[iq * FOX_HEADS + h] = jnp.concatenate([q_h, e_h], axis=0).astype(jnp.bfloat16)
        return carry

    lax.fori_loop(0, n_tiles, build_rhs, 0)

    def lhs_tile(j, pair):
        rows = _tile_rows(j)
        return jnp.concatenate([k_ref[0, rows, pair * V7X_LANES:(pair + 1) * V7X_LANES], ek_ref[0, rows, :]], axis=1)

    _flash_pipeline(tab_ref, _causal_segments(n_tiles), range(FOX_HEADS), FOX_HEADS, lhs_tile, sc)
    _flash_finish(FOX_HEADS, sc, g_ref, o_ref)


def _moba_kernel(tab_ref, q_ref, k_ref, v_ref, kmean_ref, bias_ref, g_ref, o_ref, *scratch):
    sc = _FlashScratch(*scratch)
    f32, bf16 = jnp.float32, jnp.bfloat16
    n_tiles = q_ref.shape[1] // TQ
    n_blocks = kmean_ref.shape[1]
    _flash_begin(v_ref, sc)
    sel_rows = 16
    blk = lax.broadcasted_iota(jnp.int32, (sel_rows, TQ), 0)
    lane = lax.broadcasted_iota(jnp.int32, (1, V7X_LANES), 1)

    def build_rhs(iq, carry):
        q_t = q_ref[0, _tile_rows(iq), :].astype(f32).T
        for h in range(MOBA_HEADS):
            pair, e = divmod(h, 2)
            q_h = _head_rows(q_t[pair * V7X_LANES:(pair + 1) * V7X_LANES], e)
            q_hb = q_h.astype(bf16)
            kmean = kmean_ref[0, :, pair * V7X_LANES:(pair + 1) * V7X_LANES]
            km_hi = kmean.astype(bf16).astype(f32)
            gate_lhs = jnp.concatenate([jnp.concatenate([km_hi, kmean - km_hi], axis=1),
                                        jnp.zeros((sel_rows - n_blocks, 2 * V7X_LANES), f32)], axis=0).astype(bf16)
            gate = _dot(gate_lhs, jnp.concatenate([q_hb, q_hb], axis=0))
            beaten = jnp.zeros((sel_rows, TQ), f32)
            for mblk in range(n_blocks - 1):
                gm = gate[mblk:mblk + 1, :]
                wins = (gm > gate) | ((gm == gate) & (mblk < blk))
                beaten = beaten + jnp.where(wins, jnp.where(mblk < iq, 1.0, 0.0), 0.0)
            keep = ((blk < iq) & (beaten < MOBA_TOPK)) | (blk == iq)
            sel = jnp.where(keep, 0.0, MASK_NEG)
            pad = jnp.zeros((V7X_LANES - sel_rows, TQ), f32)
            sc.rhs[iq * MOBA_HEADS + h] = jnp.concatenate([q_h, sel, pad], axis=0).astype(bf16)
        return carry

    lax.fori_loop(0, n_tiles, build_rhs, 0)

    def lhs_tile(j, pair):
        onehot = jnp.broadcast_to(jnp.where(lane == j, 1.0, 0.0).astype(bf16), (TQ, V7X_LANES))
        return jnp.concatenate([k_ref[0, _tile_rows(j), pair * V7X_LANES:(pair + 1) * V7X_LANES], onehot], axis=1)

    def bias_fn(i, j, h):
        return bias_ref[h, jnp.minimum(i - j, 2)]

    segments = [(n_tiles * (n_tiles + 1) // 2, bias_fn)]
    _flash_pipeline(tab_ref, segments, range(MOBA_HEADS), MOBA_HEADS, lhs_tile, sc)
    _flash_finish(MOBA_HEADS, sc, g_ref, o_ref)


def _mla_kernel(tab_ref, qn_ref, qr_ref, kn_ref, kr_ref, v_ref, g_ref, o_ref, *scratch):
    sc = _FlashScratch(*scratch)
    n_tiles = qn_ref.shape[1] // TQ
    _flash_begin(v_ref, sc)
    heads_per_rot = V7X_LANES // MLA_ROPE_DIM
    rot_slot = lax.broadcasted_iota(jnp.int32, (V7X_LANES, TQ), 0) // MLA_ROPE_DIM

    def build_rhs(iq, carry):
        rows = _tile_rows(iq)
        qn_t = qn_ref[0, rows, :].astype(jnp.float32).T
        qr_t = qr_ref[0, rows, :].astype(jnp.float32).T
        for h in range(MLA_HEADS):
            pair, e = divmod(h, 2)
            quad, slot = divmod(h, heads_per_rot)
            q_h = _head_rows(qn_t[pair * V7X_LANES:(pair + 1) * V7X_LANES], e)
            r_h = jnp.where(rot_slot == slot, qr_t[quad * V7X_LANES:(quad + 1) * V7X_LANES], 0.0)
            sc.rhs[iq * MLA_HEADS + h] = jnp.concatenate([q_h, r_h], axis=0).astype(jnp.bfloat16)
        return carry

    lax.fori_loop(0, n_tiles, build_rhs, 0)

    def lhs_tile(j, pair):
        rows = _tile_rows(j)
        return jnp.concatenate([kn_ref[0, rows, pair * V7X_LANES:(pair + 1) * V7X_LANES], kr_ref[0, rows, :]], axis=1)

    for h0 in range(0, MLA_HEADS, PIPE_HEADS):
        _flash_pipeline(tab_ref, _causal_segments(n_tiles), range(h0, h0 + PIPE_HEADS), MLA_HEADS, lhs_tile, sc)
    _flash_finish(MLA_HEADS, sc, g_ref, o_ref)


def _tile_pairs(n_tiles, diagonal_last):
    if diagonal_last:
        pairs = [(i, j) for i in range(n_tiles) for j in range(i)] + [(i, i) for i in range(n_tiles)]
    else:
        pairs = [(i, j) for i in range(n_tiles) for j in range(i + 1)]
    return jnp.asarray(np.array(pairs + [(1, 1)], np.int32).T)


def _attn_call(kernel, name, n_heads, diagonal_last, arrays, const_arrays):
    batch, seq, _ = arrays[0].shape
    n_tiles = seq // TQ
    out_width = n_heads * HEAD_LANES
    row = lambda a: pl.BlockSpec((1,) + a.shape[1:], lambda b: (b, 0, 0))
    return pl.pallas_call(
        kernel,
        out_shape=jax.ShapeDtypeStruct((batch, seq, out_width), jnp.bfloat16),
        grid=(batch,),
        in_specs=([pl.BlockSpec(memory_space=pltpu.SMEM)] + [row(a) for a in arrays]
                  + [_const_spec(a.shape) for a in const_arrays]),
        out_specs=pl.BlockSpec((1, seq, out_width), lambda b: (b, 0, 0)),
        scratch_shapes=[
            pltpu.VMEM((n_tiles, n_heads * V_ROWS, TQ), jnp.bfloat16),
            pltpu.VMEM((n_tiles * n_heads, 2 * V7X_LANES, TQ), jnp.bfloat16),
            pltpu.VMEM((PIPE_HEADS, TQ, TQ), jnp.float32),
            pltpu.VMEM((PIPE_HEADS, TQ, TQ), jnp.bfloat16),
            pltpu.VMEM((PIPE_HEADS, 1, TQ), jnp.float32),
            pltpu.VMEM((n_tiles * n_heads, 1, TQ), jnp.float32),
            pltpu.VMEM((n_tiles * n_heads, V_ROWS, TQ), jnp.float32),
        ],
        compiler_params=_params("arbitrary"),
        name=name,
    )(_tile_pairs(n_tiles, diagonal_last), *arrays, *const_arrays)


def _bias_tile_kernel(table_ref, bucket_ref, o_ref):
    h = pl.program_id(0)
    for k in range(bucket_ref.shape[0]):
        bkt = bucket_ref[k]
        tile = jnp.full(bkt.shape, MASK_NEG, jnp.float32)
        for b in range(T5_BUCKETS):
            tile = jnp.where(bkt == b, table_ref[h, b] * LOG2E, tile)
        o_ref[0, k] = tile


def _moba_bias_tiles(t5_table):
    bucket = _t5_bucket_table(3 * TQ)
    key = np.arange(TQ)[:, None]
    qry = np.arange(TQ)[None, :]
    kinds = []
    for k in range(3):
        dist = qry - key + k * TQ
        kinds.append(np.where(dist >= 0, bucket[np.maximum(dist, 0)], -1))
    buckets = jnp.asarray(np.stack(kinds), jnp.int32)
    return pl.pallas_call(
        _bias_tile_kernel,
        out_shape=jax.ShapeDtypeStruct((MOBA_HEADS, 3, TQ, TQ), jnp.float32),
        grid=(MOBA_HEADS,),
        in_specs=[pl.BlockSpec(memory_space=pltpu.SMEM), _const_spec(buckets.shape)],
        out_specs=pl.BlockSpec((1, 3, TQ, TQ), lambda h: (h, 0, 0, 0)),
        compiler_params=_params("arbitrary"),
        name="t5_bias_tiles",
    )(t5_table.T, buckets)


def _mix_ffn_kernel(x_ref, of_ref, om_ref, oc_ref, mod_ref, gmix_ref, gpre_ref, gpost_ref,
                    wout_ref, wgu_ref, wd_ref, o_ref):
    bf16 = jnp.bfloat16
    gate_a = mod_ref[0, :, 2 * D_MODEL:3 * D_MODEL]
    shift = mod_ref[0, :, 3 * D_MODEL:4 * D_MODEL]
    scale = mod_ref[0, :, 4 * D_MODEL:5 * D_MODEL]
    gate_f = mod_ref[0, :, 5 * D_MODEL:6 * D_MODEL]
    o = jnp.concatenate([of_ref[0], om_ref[0], oc_ref[0]], axis=1)
    x = x_ref[0] + gate_a * _rms(_dot(o, wout_ref[...]), gmix_ref[...])
    h = (_rms(x, gpre_ref[...]) * (1.0 + scale) + shift).astype(bf16)
    acc = None
    for c0, c1 in FFN_CHUNKS:
        g = _dot(h, wgu_ref[:, c0:c1])
        u = _dot(h, wgu_ref[:, D_FF + c0:D_FF + c1])
        a = (g * jax.nn.sigmoid(g) * u).astype(bf16)
        part = _dot(a, wd_ref[c0:c1, :])
        acc = part if acc is None else acc + part
    o_ref[0] = x + gate_f * _rms(acc, gpost_ref[...])


def _mix_ffn(x, o_f, o_m, o_c, mod_l, gmix, gpre, gpost, w_out, wgu, wd):
    batch, seq, d = x.shape

    def tok(width):
        return pl.BlockSpec((1, TM, width), lambda b, t: (b, t, 0))

    return pl.pallas_call(
        _mix_ffn_kernel,
        out_shape=jax.ShapeDtypeStruct(x.shape, x.dtype),
        grid=(batch, seq // TM),
        in_specs=[tok(d), tok(FOX_W), tok(MOBA_W), tok(MLA_W),
                  pl.BlockSpec((1, 1, 6 * d), lambda b, t: (b, 0, 0)),
                  _const_spec((1, d)), _const_spec((1, d)), _const_spec((1, d)),
                  _const_spec(w_out.shape), _const_spec(wgu.shape), _const_spec(wd.shape)],
        out_specs=tok(d),
        compiler_params=_params("arbitrary", "arbitrary"),
        name="mix_ffn",
    )(x, o_f, o_m, o_c, mod_l, gmix, gpre, gpost, w_out, wgu, wd)


def _rope_tables(seq):
    half = MLA_ROPE_DIM // 2
    inv_freq = 1.0 / (ROPE_THETA ** (jnp.arange(half, dtype=jnp.float32) / half))
    ang = jnp.arange(seq).astype(jnp.float32)[:, None] * inv_freq[None, :]
    reps = V7X_LANES // MLA_ROPE_DIM
    cos = jnp.tile(jnp.concatenate([jnp.cos(ang), jnp.cos(ang)], axis=1), (1, reps))
    sin = jnp.tile(jnp.concatenate([-jnp.sin(ang), jnp.sin(ang)], axis=1), (1, reps))
    return cos, sin


def kernel(x, c, t5_table, w_ada, b_ada, g_mix_pre, g_mix_post, w_in, b_forget, g_q_lat, w_uq, g_kv_lat, w_ukv,
           g_group, w_out, g_ffn_pre, g_ffn_post, w_gate_up, w_down):
    batch, seq, d = x.shape
    assert d == D_MODEL and seq % TM == 0 and TM % MOBA_BLOCK == 0 and TQ == MOBA_BLOCK
    bf16 = jnp.bfloat16
    in_cols, uq_cols, ukv_cols = _in_proj_columns(), _uq_columns(), _ukv_columns()
    cos_t, sin_t = _rope_tables(seq)
    tril = np.tril(np.ones((TM, TM), np.float32))
    tril = jnp.asarray(np.concatenate([tril] * N_SPLIT, axis=1), bf16)
    moba_bias = _moba_bias_tiles(t5_table)
    fg_lane = np.arange(V7X_LANES)
    fg_used = (fg_lane < FGATE_SLOT * FOX_HEADS) & (fg_lane % FGATE_SLOT < 2 * N_SPLIT)
    fg_head = np.minimum(fg_lane // FGATE_SLOT, FOX_HEADS - 1)

    mod = _ada_mod(c, w_ada, b_ada)
    for l in range(DEPTH):
        mod_l = mod[l].reshape(batch, 1, 6 * d)
        fbias = jnp.where(jnp.asarray(fg_used), b_forget[l][fg_head], 0.0).reshape(1, V7X_LANES)
        pr = _proj(x, mod_l, g_mix_pre[l].reshape(1, d),
                   _take_columns(w_in[l], in_cols).astype(bf16), fbias,
                   g_q_lat[l].reshape(1, -1), _take_columns(w_uq[l], uq_cols).astype(bf16),
                   g_kv_lat[l].reshape(1, -1), _take_columns(w_ukv[l], ukv_cols).astype(bf16),
                   cos_t, sin_t, tril)
        g_a = g_group[l, :FOX_W].reshape(1, -1)
        g_b = g_group[l, FOX_W:FOX_W + MOBA_W].reshape(1, -1)
        g_c = g_group[l, FOX_W + MOBA_W:].reshape(1, -1)
        o_f = _attn_call(_fox_kernel, "fox_attn", FOX_HEADS, True,
                         [pr["qf"], pr["eq"], pr["kf"], pr["ek"], pr["vf"]], [g_a])
        o_m = _attn_call(_moba_kernel, "moba_attn", MOBA_HEADS, False,
                         [pr["qm"], pr["km"], pr["vm"], pr["kmean"]], [moba_bias, g_b])
        o_c = _attn_call(_mla_kernel, "mla_attn", MLA_HEADS, True,
                         [pr["qn"], pr["qr"], pr["kn"], pr["kr"], pr["vc"]], [g_c])
        x = _mix_ffn(x, o_f, o_m, o_c, mod_l, g_mix_post[l].reshape(1, d), g_ffn_pre[l].reshape(1, d),
                     g_ffn_post[l].reshape(1, d), w_out[l].astype(bf16), w_gate_up[l].astype(bf16),
                     w_down[l].astype(bf16))
    return x
```

```python
import math
from typing import NamedTuple

import jax
import jax.numpy as jnp
import numpy as np
from jax import lax
from jax.experimental import pallas as pl
from jax.experimental.pallas import tpu as pltpu

D_MODEL = 1024
DEPTH = 2
FOX_HEADS = 4
FOX_HEAD_DIM = 64
MOBA_HEADS = 4
MOBA_HEAD_DIM = 64
MOBA_BLOCK = 256
MOBA_TOPK = 3
MLA_HEADS = 8
MLA_NOPE_DIM = 64
MLA_ROPE_DIM = 32
MLA_V_DIM = 64
MLA_Q_RANK = 256
MLA_KV_RANK = 128
ROPE_THETA = 10000.0
T5_BUCKETS = 32
T5_MAX_DISTANCE = 128
D_FF = -(-8 * D_MODEL // (3 * 256)) * 256
RMS_EPS = 1e-6
FOX_W = FOX_HEADS * FOX_HEAD_DIM
MOBA_W = MOBA_HEADS * MOBA_HEAD_DIM
MLA_W = MLA_HEADS * MLA_V_DIM
MIX_WIDTH = FOX_W + MOBA_W + MLA_W
IN_SIZES = (FOX_W, FOX_W, FOX_W, FOX_HEADS, MOBA_W, MOBA_W, MOBA_W, MLA_Q_RANK, MLA_KV_RANK, MLA_ROPE_DIM)

V7X_LANES = 128
V7X_VMEM_LIMIT_BYTES = 56 * 1024 * 1024

TM = 512
TQ = 256
PIPE_HEADS = 4
FFN_CHUNKS = ((0, 1024), (1024, 2048), (2048, D_FF))

HEAD_LANES = 64
MASK_NEG = -1e30
LOG2E = math.log2(math.e)
V_ROWS = 80
FGATE_SLOT = 8
N_SPLIT = 3

_C_QF, _C_KF, _C_VF = 0, 256, 512
_C_QM, _C_KM, _C_VM = 768, 1024, 1280
_C_CQ, _C_CKV = 1536, 1792
_C_FG, _C_KR, _C_KRS = 1920, 2048, 2176
IN_WIDTH_PADDED = 2304


def _in_proj_columns():
    off = np.cumsum((0,) + IN_SIZES)
    q_f, k_f, v_f, f_g, q_m, k_m, v_m, c_q, c_kv, k_r = (np.arange(off[i], off[i + 1]) for i in range(10))
    fg = np.full((V7X_LANES,), -1, np.int64)
    for h in range(FOX_HEADS):
        fg[FGATE_SLOT * h:FGATE_SLOT * h + 2 * N_SPLIT] = f_g[h]
    half = MLA_ROPE_DIM // 2
    kr4 = np.tile(k_r, V7X_LANES // MLA_ROPE_DIM)
    kr4s = np.tile(np.roll(k_r, -half), V7X_LANES // MLA_ROPE_DIM)
    cols = np.concatenate([q_f, k_f, v_f, q_m, k_m, v_m, c_q, c_kv, fg, kr4, kr4s])
    assert cols.shape == (IN_WIDTH_PADDED,)
    return cols


def _uq_columns():
    per = MLA_NOPE_DIM + MLA_ROPE_DIM
    half = MLA_ROPE_DIM // 2
    nope = np.concatenate([np.arange(h * per, h * per + MLA_NOPE_DIM) for h in range(MLA_HEADS)])
    rot = np.concatenate([np.arange(h * per + MLA_NOPE_DIM, (h + 1) * per) for h in range(MLA_HEADS)])
    rots = np.concatenate([np.roll(np.arange(h * per + MLA_NOPE_DIM, (h + 1) * per), -half) for h in range(MLA_HEADS)])
    return np.concatenate([nope, rot, rots])


def _ukv_columns():
    per = MLA_NOPE_DIM + MLA_V_DIM
    nope = np.concatenate([np.arange(h * per, h * per + MLA_NOPE_DIM) for h in range(MLA_HEADS)])
    val = np.concatenate([np.arange(h * per + MLA_NOPE_DIM, (h + 1) * per) for h in range(MLA_HEADS)])
    return np.concatenate([nope, val])


def _take_columns(w, cols):
    g = jnp.take(w, jnp.asarray(np.maximum(cols, 0), jnp.int32), axis=1)
    return jnp.where(jnp.asarray(cols >= 0)[None, :], g, 0.0)


def _t5_bucket_table(n):
    d = np.arange(n, dtype=np.int32)
    max_exact = T5_BUCKETS // 2
    nf = np.maximum(d, max_exact).astype(np.float32)
    ratio = np.log(nf / np.float32(max_exact)) / np.float32(math.log(T5_MAX_DISTANCE / max_exact))
    large = max_exact + (ratio.astype(np.float32) * np.float32(T5_BUCKETS - max_exact)).astype(np.int32)
    large = np.minimum(large, T5_BUCKETS - 1)
    return np.where(d < max_exact, d, large).astype(np.int32)


def _const_spec(shape):
    nd = len(shape)
    return pl.BlockSpec(shape, lambda *_: (0,) * nd, pipeline_mode=pl.Buffered(1))


def _params(*sem):
    return pltpu.CompilerParams(dimension_semantics=sem, vmem_limit_bytes=V7X_VMEM_LIMIT_BYTES)


def _rms(x, g):
    return x * lax.rsqrt(jnp.mean(x * x, axis=-1, keepdims=True) + RMS_EPS) * g


def _split3(v):
    hi = v.astype(jnp.bfloat16)
    r1 = v - hi.astype(jnp.float32)
    mid = r1.astype(jnp.bfloat16)
    lo = (r1 - mid.astype(jnp.float32)).astype(jnp.bfloat16)
    return hi, mid, lo


def _dot(a, b):
    return jnp.dot(a, b, preferred_element_type=jnp.float32)


def _ada_kernel(c_ref, w_ref, b_ref, o_ref):
    c = c_ref[...]
    act = (c * jax.nn.sigmoid(c)).astype(jnp.bfloat16)
    o_ref[0] = _dot(act, w_ref[0].astype(jnp.bfloat16)) + b_ref[0]


def _ada_mod(c, w_ada, b_ada):
    depth, d, six_d = w_ada.shape
    batch = c.shape[0]
    n_col = six_d // d
    return pl.pallas_call(
        _ada_kernel,
        out_shape=jax.ShapeDtypeStruct((depth, batch, six_d), jnp.float32),
        grid=(depth, n_col),
        in_specs=[
            pl.BlockSpec((batch, d), lambda l, j: (0, 0)),
            pl.BlockSpec((1, d, d), lambda l, j: (l, 0, j)),
            pl.BlockSpec((1, 1, d), lambda l, j: (l, 0, j)),
        ],
        out_specs=pl.BlockSpec((1, batch, d), lambda l, j: (l, 0, j)),
        compiler_params=_params("arbitrary", "arbitrary"),
        name="ada_mod",
    )(c, w_ada, b_ada.reshape(depth, 1, six_d))


def _proj_kernel(x_ref, mod_ref, gpre_ref, win_ref, fb_ref, gq_ref, wuq_ref, gkv_ref, wukv_ref,
                 cos_ref, sin_ref, tril_ref,
                 qf_ref, eq_ref, kf_ref, ek_ref, vf_ref, qm_ref, km_ref, vm_ref, kmean_ref,
                 qn_ref, qr_ref, kn_ref, kr_ref, vc_ref, carry_ref):
    t = pl.program_id(1)
    bf16 = jnp.bfloat16
    x = x_ref[0]
    shift = mod_ref[0, :, 0:D_MODEL]
    scale = mod_ref[0, :, D_MODEL:2 * D_MODEL]
    h = (_rms(x, gpre_ref[...]) * (1.0 + scale) + shift).astype(bf16)

    def seg(c0, width):
        return _dot(h, win_ref[:, c0:c0 + width])

    qf_ref[0] = (seg(_C_QF, FOX_W) * (FOX_HEAD_DIM ** -0.5 * LOG2E)).astype(bf16)
    kf_ref[0] = seg(_C_KF, FOX_W).astype(bf16)
    vf_ref[0] = seg(_C_VF, FOX_W).astype(bf16)
    qm_ref[0] = (seg(_C_QM, MOBA_W) * (MOBA_HEAD_DIM ** -0.5 * LOG2E)).astype(bf16)
    km = seg(_C_KM, MOBA_W)
    km_ref[0] = km.astype(bf16)
    vm_ref[0] = seg(_C_VM, MOBA_W).astype(bf16)
    kmean_ref[0, 0] = jnp.mean(km.reshape(TM // MOBA_BLOCK, MOBA_BLOCK, MOBA_W), axis=1)

    lane = lax.broadcasted_iota(jnp.int32, (1, V7X_LANES), 1)
    slot = lane % FGATE_SLOT
    used = (lane < FGATE_SLOT * FOX_HEADS) & (slot < 2 * N_SPLIT)
    fl = seg(_C_FG, V7X_LANES) + fb_ref[...]
    logf = jnp.where(used, jnp.minimum(fl, 0.0) - jnp.log1p(jnp.exp(-jnp.abs(fl))), 0.0)

    @pl.when(t == 0)
    def _():
        carry_ref[...] = jnp.zeros_like(carry_ref)

    fcum = _dot(tril_ref[...], jnp.concatenate(_split3(logf), axis=0)) + carry_ref[0:1, :]
    carry_ref[0:1, :] = fcum[TM - 1:TM, :]
    hi, mid, lo = (p.astype(jnp.float32) for p in _split3(fcum * LOG2E))
    parts = jnp.where(slot % N_SPLIT == 0, hi, jnp.where(slot % N_SPLIT == 1, mid, lo))
    eq_ref[0] = jnp.where(used, jnp.where(slot < N_SPLIT, parts, 1.0), 0.0).astype(bf16)
    ek_ref[0] = jnp.where(used, jnp.where(slot < N_SPLIT, 1.0, -parts), 0.0).astype(bf16)

    cos = cos_ref[...]
    sin = sin_ref[...]
    mla_scale = (MLA_NOPE_DIM + MLA_ROPE_DIM) ** -0.5 * LOG2E
    cq = _rms(seg(_C_CQ, MLA_Q_RANK), gq_ref[...]).astype(bf16)
    n_nope = MLA_HEADS * MLA_NOPE_DIM
    n_rot = MLA_HEADS * MLA_ROPE_DIM
    qn_ref[0] = (_dot(cq, wuq_ref[:, 0:n_nope]) * mla_scale).astype(bf16)
    q_rot = _dot(cq, wuq_ref[:, n_nope:n_nope + n_rot])
    q_rot_sw = _dot(cq, wuq_ref[:, n_nope + n_rot:n_nope + 2 * n_rot])
    cos2 = jnp.concatenate([cos] * (n_rot // V7X_LANES), axis=1)
    sin2 = jnp.concatenate([sin] * (n_rot // V7X_LANES), axis=1)
    qr_ref[0] = ((q_rot * cos2 + q_rot_sw * sin2) * mla_scale).astype(bf16)
    ckv = _rms(seg(_C_CKV, MLA_KV_RANK), gkv_ref[...]).astype(bf16)
    kn_ref[0] = _dot(ckv, wukv_ref[:, 0:n_nope]).astype(bf16)
    vc_ref[0] = _dot(ckv, wukv_ref[:, n_nope:n_nope + MLA_W]).astype(bf16)
    kr_ref[0] = (seg(_C_KR, V7X_LANES) * cos + seg(_C_KRS, V7X_LANES) * sin).astype(bf16)


def _proj(x, mod_l, gpre, win, fbias, gq, wuq, gkv, wukv, cos_t, sin_t, tril):
    batch, seq, d = x.shape
    nt = seq // TM
    bf16 = jnp.bfloat16

    def tok(width):
        return pl.BlockSpec((1, TM, width), lambda b, t: (b, t, 0))

    def out(width):
        return jax.ShapeDtypeStruct((batch, seq, width), bf16)

    widths = dict(qf=FOX_W, eq=V7X_LANES, kf=FOX_W, ek=V7X_LANES, vf=FOX_W, qm=MOBA_W, km=MOBA_W, vm=MOBA_W)
    mla_widths = dict(qn=MLA_HEADS * MLA_NOPE_DIM, qr=MLA_HEADS * MLA_ROPE_DIM, kn=MLA_HEADS * MLA_NOPE_DIM,
                      kr=V7X_LANES, vc=MLA_W)
    nb = TM // MOBA_BLOCK
    out_shape = ([out(w) for w in widths.values()]
                 + [jax.ShapeDtypeStruct((batch, nt, nb, MOBA_W), jnp.float32)]
                 + [out(w) for w in mla_widths.values()])
    out_specs = ([tok(w) for w in widths.values()]
                 + [pl.BlockSpec((1, 1, nb, MOBA_W), lambda b, t: (b, t, 0, 0))]
                 + [tok(w) for w in mla_widths.values()])
    res = pl.pallas_call(
        _proj_kernel,
        out_shape=out_shape,
        grid=(batch, nt),
        in_specs=[
            tok(d),
            pl.BlockSpec((1, 1, 6 * d), lambda b, t: (b, 0, 0)),
            _const_spec((1, d)),
            _const_spec(win.shape),
            _const_spec((1, V7X_LANES)),
            _const_spec((1, MLA_Q_RANK)),
            _const_spec(wuq.shape),
            _const_spec((1, MLA_KV_RANK)),
            _const_spec(wukv.shape),
            pl.BlockSpec((TM, V7X_LANES), lambda b, t: (t, 0)),
            pl.BlockSpec((TM, V7X_LANES), lambda b, t: (t, 0)),
            _const_spec(tril.shape),
        ],
        out_specs=out_specs,
        scratch_shapes=[pltpu.VMEM((8, V7X_LANES), jnp.float32)],
        compiler_params=_params("arbitrary", "arbitrary"),
        name="in_proj",
    )(x, mod_l, gpre, win, fbias, gq, wuq, gkv, wukv, cos_t, sin_t, tril)
    names = list(widths) + ["kmean"] + list(mla_widths)
    r = dict(zip(names, res))
    r["kmean"] = r["kmean"].reshape(batch, seq // MOBA_BLOCK, MOBA_W)
    return r


def _transpose_to_bf16(x):
    return x.astype(jnp.float32).T.astype(jnp.bfloat16)


def _head_rows(x_pair, e):
    row = lax.broadcasted_iota(jnp.int32, x_pair.shape, 0)
    keep = (row >= HEAD_LANES) if e else (row < HEAD_LANES)
    return jnp.where(keep, x_pair, 0.0)


def _causal_bias_t():
    key = lax.broadcasted_iota(jnp.int32, (TQ, TQ), 0)
    qry = lax.broadcasted_iota(jnp.int32, (TQ, TQ), 1)
    return jnp.where(key <= qry, 0.0, MASK_NEG)


def _tile_rows(t):
    return pl.ds(pl.multiple_of(t * TQ, TQ), TQ)


class _FlashScratch(NamedTuple):
    vt: object
    rhs: object
    s: object
    p: object
    alpha: object
    m: object
    acc: object


def _flash_begin(v_ref, sc):
    n_heads = sc.vt.shape[1] // V_ROWS
    extra = lax.broadcasted_iota(jnp.int32, (V_ROWS - HEAD_LANES, TQ), 0)
    ones_row = jnp.where(extra == 0, 1.0, 0.0).astype(jnp.bfloat16)
    for c in range(sc.vt.shape[0]):
        v_t = _transpose_to_bf16(v_ref[0, c * TQ:(c + 1) * TQ, :])
        for h in range(n_heads):
            sc.vt[c, h * V_ROWS:h * V_ROWS + HEAD_LANES, :] = v_t[h * HEAD_LANES:(h + 1) * HEAD_LANES]
            sc.vt[c, h * V_ROWS + HEAD_LANES:(h + 1) * V_ROWS, :] = ones_row
    sc.m[...] = jnp.full(sc.m.shape, MASK_NEG, jnp.float32)
    sc.acc[...] = jnp.zeros(sc.acc.shape, jnp.float32)


def _flash_pipeline(tab_ref, segments, heads, n_heads, lhs_tile, sc):
    slot = {h: h - heads[0] for h in heads}
    n_steps = sum(count for count, _ in segments)

    def ij(t):
        t = jnp.clip(t, 0, n_steps - 1)
        return tab_ref[0, t], tab_ref[1, t]

    def scores(t, bias_fn):
        i, j = ij(t)
        lhs = {h // 2: lhs_tile(j, h // 2) for h in heads if h % 2 == 0}
        out = {}
        for h in heads:
            s = _dot(lhs[h // 2], sc.rhs[i * n_heads + h])
            b = bias_fn(i, j, h)
            out[h] = s if b is None else s + b
        return out

    def store_scores(vals):
        for h in heads:
            sc.s[slot[h]] = vals[h]

    def softmax(t):
        i, _ = ij(t)
        s_val = {h: sc.s[slot[h]] for h in heads}
        m_old = {h: sc.m[i * n_heads + h] for h in heads}
        m_new = {h: jnp.maximum(m_old[h], jnp.max(s_val[h], axis=0, keepdims=True)) for h in heads}
        alpha = {h: jnp.exp2(m_old[h] - m_new[h]) for h in heads}
        probs = {h: jnp.exp2(s_val[h] - m_new[h]) for h in heads}
        for h in heads:
            st = i * n_heads + h
            sc.m[st] = m_new[h]
            sc.alpha[slot[h]] = alpha[h]
            sc.p[slot[h]] = probs[h].astype(jnp.bfloat16)

    def values(t):
        i, j = ij(t)
        pv = {h: _dot(sc.vt[j, h * V_ROWS:(h + 1) * V_ROWS, :], sc.p[slot[h]]) for h in heads}
        for h in heads:
            st = i * n_heads + h
            sc.acc[st] = sc.alpha[slot[h]] * sc.acc[st] + pv[h]

    def prologue(_, carry):
        for h in heads:
            sc.alpha[slot[h]] = jnp.ones(sc.alpha.shape[1:], jnp.float32)
            sc.p[slot[h]] = jnp.zeros(sc.p.shape[1:], jnp.bfloat16)
        store_scores(scores(0, segments[0][1]))
        return carry

    lax.fori_loop(0, tab_ref[0, n_steps], prologue, 0)

    first = 0
    for count, bias_fn in segments:

        def body(t, carry, bias_fn=bias_fn):
            nxt = scores(t + 1, bias_fn)
            values(t - 1)
            softmax(t)
            store_scores(nxt)
            return carry

        lax.fori_loop(max(first - 1, 0), first + count - 1, body, 0)
        first += count
    values(n_steps - 2)
    softmax(n_steps - 1)
    values(n_steps - 1)


def _flash_finish(n_heads, sc, g_ref, o_ref):
    def body(iq, carry):
        heads = []
        for h in range(n_heads):
            acc = sc.acc[iq * n_heads + h]
            heads.append(acc[0:HEAD_LANES] * (1.0 / acc[HEAD_LANES:HEAD_LANES + 1]))
        o_t = jnp.concatenate(heads, axis=0)
        o_ref[0, _tile_rows(iq), :] = _rms(o_t.T, g_ref[...]).astype(o_ref.dtype)
        return carry

    lax.fori_loop(0, o_ref.shape[1] // TQ, body, 0)


def _causal_segments(n_tiles):
    return [(n_tiles * (n_tiles - 1) // 2, lambda i, j, h: None), (n_tiles, lambda i, j, h: _causal_bias_t())]


def _fox_kernel(tab_ref, q_ref, eq_ref, k_ref, ek_ref, v_ref, g_ref, o_ref, *scratch):
    sc = _FlashScratch(*scratch)
    n_tiles = q_ref.shape[1] // TQ
    _flash_begin(v_ref, sc)

    def build_rhs(iq, carry):
        rows = _tile_rows(iq)
        q_t = q_ref[0, rows, :].astype(jnp.float32).T
        eq_t = eq_ref[0, rows, :].astype(jnp.float32).T
        slot_head = lax.broadcasted_iota(jnp.int32, eq_t.shape, 0) // FGATE_SLOT
        for h in range(FOX_HEADS):
            pair, e = divmod(h, 2)
            q_h = _head_rows(q_t[pair * V7X_LANES:(pair + 1) * V7X_LANES], e)
            e_h = jnp.where(slot_head == h, eq_t, 0.0)
            sc.rhs[iq * FOX_HEADS + h] = jnp.concatenate([q_h, e_h], axis=0).astype(jnp.bfloat16)
        return carry

    lax.fori_loop(0, n_tiles, build_rhs, 0)

    def lhs_tile(j, pair):
        rows = _tile_rows(j)
        return jnp.concatenate([k_ref[0, rows, pair * V7X_LANES:(pair + 1) * V7X_LANES], ek_ref[0, rows, :]], axis=1)

    _flash_pipeline(tab_ref, _causal_segments(n_tiles), range(FOX_HEADS), FOX_HEADS, lhs_tile, sc)
    _flash_finish(FOX_HEADS, sc, g_ref, o_ref)


def _moba_kernel(tab_ref, q_ref, k_ref, v_ref, kmean_ref, bias_ref, g_ref, o_ref, *scratch):
    sc = _FlashScratch(*scratch)
    f32, bf16 = jnp.float32, jnp.bfloat16
    n_tiles = q_ref.shape[1] // TQ
    n_blocks = kmean_ref.shape[1]
    _flash_begin(v_ref, sc)
    sel_rows = 16
    blk = lax.broadcasted_iota(jnp.int32, (sel_rows, TQ), 0)
    lane = lax.broadcasted_iota(jnp.int32, (1, V7X_LANES), 1)

    def build_rhs(iq, carry):
        q_t = q_ref[0, _tile_rows(iq), :].astype(f32).T
        for h in range(MOBA_HEADS):
            pair, e = divmod(h, 2)
            q_h = _head_rows(q_t[pair * V7X_LANES:(pair + 1) * V7X_LANES], e)
            q_hb = q_h.astype(bf16)
            kmean = kmean_ref[0, :, pair * V7X_LANES:(pair + 1) * V7X_LANES]
            km_hi = kmean.astype(bf16).astype(f32)
            gate_lhs = jnp.concatenate([jnp.concatenate([km_hi, kmean - km_hi], axis=1),
                                        jnp.zeros((sel_rows - n_blocks, 2 * V7X_LANES), f32)], axis=0).astype(bf16)
            gate = _dot(gate_lhs, jnp.concatenate([q_hb, q_hb], axis=0))
            beaten = jnp.zeros((sel_rows, TQ), f32)
            for mblk in range(n_blocks - 1):
                gm = gate[mblk:mblk + 1, :]
                wins = (gm > gate) | ((gm == gate) & (mblk < blk))
                beaten = beaten + jnp.where(wins, jnp.where(mblk < iq, 1.0, 0.0), 0.0)
            keep = ((blk < iq) & (beaten < MOBA_TOPK)) | (blk == iq)
            sel = jnp.where(keep, 0.0, MASK_NEG)[0:n_blocks]
            far = lax.broadcasted_iota(jnp.int32, (n_blocks, TQ), 0) <= iq - 2
            c = bias_ref[h, 2, 0:1, :]
            c_hi = c.astype(bf16).astype(f32)
            extras = [sel, jnp.where(far, c_hi, 0.0), jnp.where(far, c - c_hi, 0.0)]
            pad = jnp.zeros((V7X_LANES - len(extras) * n_blocks, TQ), f32)
            sc.rhs[iq * MOBA_HEADS + h] = jnp.concatenate([q_h] + extras + [pad], axis=0).astype(bf16)
        return carry

    lax.fori_loop(0, n_tiles, build_rhs, 0)

    def lhs_tile(j, pair):
        hit = (lane % n_blocks == j) & (lane < 3 * n_blocks)
        onehot = jnp.broadcast_to(jnp.where(hit, 1.0, 0.0).astype(bf16), (TQ, V7X_LANES))
        return jnp.concatenate([k_ref[0, _tile_rows(j), pair * V7X_LANES:(pair + 1) * V7X_LANES], onehot], axis=1)

    n_far = (n_tiles - 1) * (n_tiles - 2) // 2
    segments = [(n_far, lambda i, j, h: None), (n_tiles - 1, lambda i, j, h: bias_ref[h, 1]),
                (n_tiles, lambda i, j, h: bias_ref[h, 0])]
    _flash_pipeline(tab_ref, segments, range(MOBA_HEADS), MOBA_HEADS, lhs_tile, sc)
    _flash_finish(MOBA_HEADS, sc, g_ref, o_ref)


def _mla_kernel(tab_ref, qn_ref, qr_ref, kn_ref, kr_ref, v_ref, g_ref, o_ref, *scratch):
    sc = _FlashScratch(*scratch)
    n_tiles = qn_ref.shape[1] // TQ
    _flash_begin(v_ref, sc)
    heads_per_rot = V7X_LANES // MLA_ROPE_DIM
    rot_slot = lax.broadcasted_iota(jnp.int32, (V7X_LANES, TQ), 0) // MLA_ROPE_DIM

    def build_rhs(iq, carry):
        rows = _tile_rows(iq)
        qn_t = qn_ref[0, rows, :].astype(jnp.float32).T
        qr_t = qr_ref[0, rows, :].astype(jnp.float32).T
        for h in range(MLA_HEADS):
            pair, e = divmod(h, 2)
            quad, slot = divmod(h, heads_per_rot)
            q_h = _head_rows(qn_t[pair * V7X_LANES:(pair + 1) * V7X_LANES], e)
            r_h = jnp.where(rot_slot == slot, qr_t[quad * V7X_LANES:(quad + 1) * V7X_LANES], 0.0)
            sc.rhs[iq * MLA_HEADS + h] = jnp.concatenate([q_h, r_h], axis=0).astype(jnp.bfloat16)
        return carry

    lax.fori_loop(0, n_tiles, build_rhs, 0)

    def lhs_tile(j, pair):
        rows = _tile_rows(j)
        return jnp.concatenate([kn_ref[0, rows, pair * V7X_LANES:(pair + 1) * V7X_LANES], kr_ref[0, rows, :]], axis=1)

    for h0 in range(0, MLA_HEADS, PIPE_HEADS):
        _flash_pipeline(tab_ref, _causal_segments(n_tiles), range(h0, h0 + PIPE_HEADS), MLA_HEADS, lhs_tile, sc)
    _flash_finish(MLA_HEADS, sc, g_ref, o_ref)


def _tile_pairs(n_tiles, split_previous):
    near = 2 if split_previous else 1
    pairs = [(i, j) for i in range(n_tiles) for j in range(i - near + 1)]
    for d in range(near - 1, -1, -1):
        pairs += [(i, i - d) for i in range(d, n_tiles)]
    return jnp.asarray(np.array(pairs + [(1, 1)], np.int32).T)


def _attn_call(kernel, name, n_heads, split_previous, arrays, const_arrays):
    batch, seq, _ = arrays[0].shape
    n_tiles = seq // TQ
    out_width = n_heads * HEAD_LANES
    row = lambda a: pl.BlockSpec((1,) + a.shape[1:], lambda b: (b, 0, 0))
    return pl.pallas_call(
        kernel,
        out_shape=jax.ShapeDtypeStruct((batch, seq, out_width), jnp.bfloat16),
        grid=(batch,),
        in_specs=([pl.BlockSpec(memory_space=pltpu.SMEM)] + [row(a) for a in arrays]
                  + [_const_spec(a.shape) for a in const_arrays]),
        out_specs=pl.BlockSpec((1, seq, out_width), lambda b: (b, 0, 0)),
        scratch_shapes=[
            pltpu.VMEM((n_tiles, n_heads * V_ROWS, TQ), jnp.bfloat16),
            pltpu.VMEM((n_tiles * n_heads, 2 * V7X_LANES, TQ), jnp.bfloat16),
            pltpu.VMEM((PIPE_HEADS, TQ, TQ), jnp.float32),
            pltpu.VMEM((PIPE_HEADS, TQ, TQ), jnp.bfloat16),
            pltpu.VMEM((PIPE_HEADS, 1, TQ), jnp.float32),
            pltpu.VMEM((n_tiles * n_heads, 1, TQ), jnp.float32),
            pltpu.VMEM((n_tiles * n_heads, V_ROWS, TQ), jnp.float32),
        ],
        compiler_params=_params("arbitrary"),
        name=name,
    )(_tile_pairs(n_tiles, split_previous), *arrays, *const_arrays)


def _bias_tile_kernel(table_ref, bucket_ref, o_ref):
    h = pl.program_id(0)
    for k in range(bucket_ref.shape[0]):
        bkt = bucket_ref[k]
        tile = jnp.full(bkt.shape, MASK_NEG, jnp.float32)
        for b in range(T5_BUCKETS):
            tile = jnp.where(bkt == b, table_ref[h, b] * LOG2E, tile)
        o_ref[0, k] = tile


def _moba_bias_tiles(t5_table):
    bucket = _t5_bucket_table(3 * TQ)
    key = np.arange(TQ)[:, None]
    qry = np.arange(TQ)[None, :]
    kinds = []
    for k in range(3):
        dist = qry - key + k * TQ
        kinds.append(np.where(dist >= 0, bucket[np.maximum(dist, 0)], -1))
    buckets = jnp.asarray(np.stack(kinds), jnp.int32)
    return pl.pallas_call(
        _bias_tile_kernel,
        out_shape=jax.ShapeDtypeStruct((MOBA_HEADS, 3, TQ, TQ), jnp.float32),
        grid=(MOBA_HEADS,),
        in_specs=[pl.BlockSpec(memory_space=pltpu.SMEM), _const_spec(buckets.shape)],
        out_specs=pl.BlockSpec((1, 3, TQ, TQ), lambda h: (h, 0, 0, 0)),
        compiler_params=_params("arbitrary"),
        name="t5_bias_tiles",
    )(t5_table.T, buckets)


def _mix_ffn_kernel(x_ref, of_ref, om_ref, oc_ref, mod_ref, gmix_ref, gpre_ref, gpost_ref,
                    wout_ref, wgu_ref, wd_ref, o_ref):
    bf16 = jnp.bfloat16
    gate_a = mod_ref[0, :, 2 * D_MODEL:3 * D_MODEL]
    shift = mod_ref[0, :, 3 * D_MODEL:4 * D_MODEL]
    scale = mod_ref[0, :, 4 * D_MODEL:5 * D_MODEL]
    gate_f = mod_ref[0, :, 5 * D_MODEL:6 * D_MODEL]
    o = jnp.concatenate([of_ref[0], om_ref[0], oc_ref[0]], axis=1)
    x = x_ref[0] + gate_a * _rms(_dot(o, wout_ref[...]), gmix_ref[...])
    h = (_rms(x, gpre_ref[...]) * (1.0 + scale) + shift).astype(bf16)
    acc = None
    for c0, c1 in FFN_CHUNKS:
        g = _dot(h, wgu_ref[:, c0:c1])
        u = _dot(h, wgu_ref[:, D_FF + c0:D_FF + c1])
        a = (g * jax.nn.sigmoid(g) * u).astype(bf16)
        part = _dot(a, wd_ref[c0:c1, :])
        acc = part if acc is None else acc + part
    o_ref[0] = x + gate_f * _rms(acc, gpost_ref[...])


def _mix_ffn(x, o_f, o_m, o_c, mod_l, gmix, gpre, gpost, w_out, wgu, wd):
    batch, seq, d = x.shape

    def tok(width):
        return pl.BlockSpec((1, TM, width), lambda b, t: (b, t, 0))

    return pl.pallas_call(
        _mix_ffn_kernel,
        out_shape=jax.ShapeDtypeStruct(x.shape, x.dtype),
        grid=(batch, seq // TM),
        in_specs=[tok(d), tok(FOX_W), tok(MOBA_W), tok(MLA_W),
                  pl.BlockSpec((1, 1, 6 * d), lambda b, t: (b, 0, 0)),
                  _const_spec((1, d)), _const_spec((1, d)), _const_spec((1, d)),
                  _const_spec(w_out.shape), _const_spec(wgu.shape), _const_spec(wd.shape)],
        out_specs=tok(d),
        compiler_params=_params("arbitrary", "arbitrary"),
        name="mix_ffn",
    )(x, o_f, o_m, o_c, mod_l, gmix, gpre, gpost, w_out, wgu, wd)


def _rope_tables(seq):
    half = MLA_ROPE_DIM // 2
    inv_freq = 1.0 / (ROPE_THETA ** (jnp.arange(half, dtype=jnp.float32) / half))
    ang = jnp.arange(seq).astype(jnp.float32)[:, None] * inv_freq[None, :]
    reps = V7X_LANES // MLA_ROPE_DIM
    cos = jnp.tile(jnp.concatenate([jnp.cos(ang), jnp.cos(ang)], axis=1), (1, reps))
    sin = jnp.tile(jnp.concatenate([-jnp.sin(ang), jnp.sin(ang)], axis=1), (1, reps))
    return cos, sin


def kernel(x, c, t5_table, w_ada, b_ada, g_mix_pre, g_mix_post, w_in, b_forget, g_q_lat, w_uq, g_kv_lat, w_ukv,
           g_group, w_out, g_ffn_pre, g_ffn_post, w_gate_up, w_down):
    batch, seq, d = x.shape
    assert d == D_MODEL and seq % TM == 0 and TM % MOBA_BLOCK == 0 and TQ == MOBA_BLOCK
    bf16 = jnp.bfloat16
    in_cols, uq_cols, ukv_cols = _in_proj_columns(), _uq_columns(), _ukv_columns()
    cos_t, sin_t = _rope_tables(seq)
    tril = np.tril(np.ones((TM, TM), np.float32))
    tril = jnp.asarray(np.concatenate([tril] * N_SPLIT, axis=1), bf16)
    moba_bias = _moba_bias_tiles(t5_table)
    fg_lane = np.arange(V7X_LANES)
    fg_used = (fg_lane < FGATE_SLOT * FOX_HEADS) & (fg_lane % FGATE_SLOT < 2 * N_SPLIT)
    fg_head = np.minimum(fg_lane // FGATE_SLOT, FOX_HEADS - 1)

    mod = _ada_mod(c, w_ada, b_ada)
    for l in range(DEPTH):
        mod_l = mod[l].reshape(batch, 1, 6 * d)
        fbias = jnp.where(jnp.asarray(fg_used), b_forget[l][fg_head], 0.0).reshape(1, V7X_LANES)
        pr = _proj(x, mod_l, g_mix_pre[l].reshape(1, d),
                   _take_columns(w_in[l], in_cols).astype(bf16), fbias,
                   g_q_lat[l].reshape(1, -1), _take_columns(w_uq[l], uq_cols).astype(bf16),
                   g_kv_lat[l].reshape(1, -1), _take_columns(w_ukv[l], ukv_cols).astype(bf16),
                   cos_t, sin_t, tril)
        g_a = g_group[l, :FOX_W].reshape(1, -1)
        g_b = g_group[l, FOX_W:FOX_W + MOBA_W].reshape(1, -1)
        g_c = g_group[l, FOX_W + MOBA_W:].reshape(1, -1)
        o_f = _attn_call(_fox_kernel, "fox_attn", FOX_HEADS, False,
                         [pr["qf"], pr["eq"], pr["kf"], pr["ek"], pr["vf"]], [g_a])
        o_m = _attn_call(_moba_kernel, "moba_attn", MOBA_HEADS, True,
                         [pr["qm"], pr["km"], pr["vm"], pr["kmean"]], [moba_bias, g_b])
        o_c = _attn_call(_mla_kernel, "mla_attn", MLA_HEADS, False,
                         [pr["qn"], pr["qr"], pr["kn"], pr["kr"], pr["vc"]], [g_c])
        x = _mix_ffn(x, o_f, o_m, o_c, mod_l, g_mix_post[l].reshape(1, d), g_ffn_pre[l].reshape(1, d),
                     g_ffn_post[l].reshape(1, d), w_out[l].astype(bf16), w_gate_up[l].astype(bf16),
                     w_down[l].astype(bf16))
    return x
```

```python
import math
from typing import NamedTuple

import jax
import jax.numpy as jnp
import numpy as np
from jax import lax
from jax.experimental import pallas as pl
from jax.experimental.pallas import tpu as pltpu

D_MODEL = 1024
DEPTH = 2
FOX_HEADS = 4
FOX_HEAD_DIM = 64
MOBA_HEADS = 4
MOBA_HEAD_DIM = 64
MOBA_BLOCK = 256
MOBA_TOPK = 3
MLA_HEADS = 8
MLA_NOPE_DIM = 64
MLA_ROPE_DIM = 32
MLA_V_DIM = 64
MLA_Q_RANK = 256
MLA_KV_RANK = 128
ROPE_THETA = 10000.0
T5_BUCKETS = 32
T5_MAX_DISTANCE = 128
D_FF = -(-8 * D_MODEL // (3 * 256)) * 256
RMS_EPS = 1e-6
FOX_W = FOX_HEADS * FOX_HEAD_DIM
MOBA_W = MOBA_HEADS * MOBA_HEAD_DIM
MLA_W = MLA_HEADS * MLA_V_DIM
MIX_WIDTH = FOX_W + MOBA_W + MLA_W
IN_SIZES = (FOX_W, FOX_W, FOX_W, FOX_HEADS, MOBA_W, MOBA_W, MOBA_W, MLA_Q_RANK, MLA_KV_RANK, MLA_ROPE_DIM)

V7X_LANES = 128
V7X_VMEM_LIMIT_BYTES = 56 * 1024 * 1024

TM = 512
TQ = 256
PIPE_HEADS = 4
SETUP_UNROLL = 4
FFN_CHUNKS = ((0, 1024), (1024, 2048), (2048, D_FF))

HEAD_LANES = 64
MASK_NEG = -1e30
LOG2E = math.log2(math.e)
V_ROWS = 80
FGATE_SLOT = 8
N_SPLIT = 3

_C_QF, _C_KF, _C_VF = 0, 256, 512
_C_QM, _C_KM, _C_VM = 768, 1024, 1280
_C_CQ, _C_CKV = 1536, 1792
_C_FG, _C_KR, _C_KRS = 1920, 2048, 2176
IN_WIDTH_PADDED = 2304


def _in_proj_columns():
    off = np.cumsum((0,) + IN_SIZES)
    q_f, k_f, v_f, f_g, q_m, k_m, v_m, c_q, c_kv, k_r = (np.arange(off[i], off[i + 1]) for i in range(10))
    fg = np.full((V7X_LANES,), -1, np.int64)
    for h in range(FOX_HEADS):
        fg[FGATE_SLOT * h:FGATE_SLOT * h + 2 * N_SPLIT] = f_g[h]
    half = MLA_ROPE_DIM // 2
    kr4 = np.tile(k_r, V7X_LANES // MLA_ROPE_DIM)
    kr4s = np.tile(np.roll(k_r, -half), V7X_LANES // MLA_ROPE_DIM)
    cols = np.concatenate([q_f, k_f, v_f, q_m, k_m, v_m, c_q, c_kv, fg, kr4, kr4s])
    assert cols.shape == (IN_WIDTH_PADDED,)
    return cols


def _uq_columns():
    per = MLA_NOPE_DIM + MLA_ROPE_DIM
    half = MLA_ROPE_DIM // 2
    nope = np.concatenate([np.arange(h * per, h * per + MLA_NOPE_DIM) for h in range(MLA_HEADS)])
    rot = np.concatenate([np.arange(h * per + MLA_NOPE_DIM, (h + 1) * per) for h in range(MLA_HEADS)])
    rots = np.concatenate([np.roll(np.arange(h * per + MLA_NOPE_DIM, (h + 1) * per), -half) for h in range(MLA_HEADS)])
    return np.concatenate([nope, rot, rots])


def _ukv_columns():
    per = MLA_NOPE_DIM + MLA_V_DIM
    nope = np.concatenate([np.arange(h * per, h * per + MLA_NOPE_DIM) for h in range(MLA_HEADS)])
    val = np.concatenate([np.arange(h * per + MLA_NOPE_DIM, (h + 1) * per) for h in range(MLA_HEADS)])
    return np.concatenate([nope, val])


def _take_columns(w, cols):
    g = jnp.take(w, jnp.asarray(np.maximum(cols, 0), jnp.int32), axis=1)
    return jnp.where(jnp.asarray(cols >= 0)[None, :], g, 0.0)


def _t5_bucket_table(n):
    d = np.arange(n, dtype=np.int32)
    max_exact = T5_BUCKETS // 2
    nf = np.maximum(d, max_exact).astype(np.float32)
    ratio = np.log(nf / np.float32(max_exact)) / np.float32(math.log(T5_MAX_DISTANCE / max_exact))
    large = max_exact + (ratio.astype(np.float32) * np.float32(T5_BUCKETS - max_exact)).astype(np.int32)
    large = np.minimum(large, T5_BUCKETS - 1)
    return np.where(d < max_exact, d, large).astype(np.int32)


def _const_spec(shape):
    nd = len(shape)
    return pl.BlockSpec(shape, lambda *_: (0,) * nd, pipeline_mode=pl.Buffered(1))


def _params(*sem):
    return pltpu.CompilerParams(dimension_semantics=sem, vmem_limit_bytes=V7X_VMEM_LIMIT_BYTES)


def _rms(x, g):
    return x * lax.rsqrt(jnp.mean(x * x, axis=-1, keepdims=True) + RMS_EPS) * g


def _split3(v):
    hi = v.astype(jnp.bfloat16)
    r1 = v - hi.astype(jnp.float32)
    mid = r1.astype(jnp.bfloat16)
    lo = (r1 - mid.astype(jnp.float32)).astype(jnp.bfloat16)
    return hi, mid, lo


def _dot(a, b):
    return jnp.dot(a, b, preferred_element_type=jnp.float32)


def _ada_kernel(c_ref, w_ref, b_ref, o_ref):
    c = c_ref[...]
    act = (c * jax.nn.sigmoid(c)).astype(jnp.bfloat16)
    o_ref[0] = _dot(act, w_ref[0].astype(jnp.bfloat16)) + b_ref[0]


def _ada_mod(c, w_ada, b_ada):
    depth, d, six_d = w_ada.shape
    batch = c.shape[0]
    n_col = six_d // d
    return pl.pallas_call(
        _ada_kernel,
        out_shape=jax.ShapeDtypeStruct((depth, batch, six_d), jnp.float32),
        grid=(depth, n_col),
        in_specs=[
            pl.BlockSpec((batch, d), lambda l, j: (0, 0)),
            pl.BlockSpec((1, d, d), lambda l, j: (l, 0, j)),
            pl.BlockSpec((1, 1, d), lambda l, j: (l, 0, j)),
        ],
        out_specs=pl.BlockSpec((1, batch, d), lambda l, j: (l, 0, j)),
        compiler_params=_params("arbitrary", "arbitrary"),
        name="ada_mod",
    )(c, w_ada, b_ada.reshape(depth, 1, six_d))


def _proj_kernel(x_ref, mod_ref, gpre_ref, win_ref, fb_ref, gq_ref, wuq_ref, gkv_ref, wukv_ref,
                 cos_ref, sin_ref, tril_ref,
                 qf_ref, eq_ref, kf_ref, ek_ref, vf_ref, qm_ref, km_ref, vm_ref, kmean_ref,
                 qn_ref, qr_ref, kn_ref, kr_ref, vc_ref, carry_ref):
    t = pl.program_id(1)
    bf16 = jnp.bfloat16
    x = x_ref[0]
    shift = mod_ref[0, :, 0:D_MODEL]
    scale = mod_ref[0, :, D_MODEL:2 * D_MODEL]
    h = (_rms(x, gpre_ref[...]) * (1.0 + scale) + shift).astype(bf16)

    def seg(c0, width):
        return _dot(h, win_ref[:, c0:c0 + width])

    qf_ref[0] = (seg(_C_QF, FOX_W) * (FOX_HEAD_DIM ** -0.5 * LOG2E)).astype(bf16)
    kf_ref[0] = seg(_C_KF, FOX_W).astype(bf16)
    vf_ref[0] = seg(_C_VF, FOX_W).astype(bf16)
    qm_ref[0] = (seg(_C_QM, MOBA_W) * (MOBA_HEAD_DIM ** -0.5 * LOG2E)).astype(bf16)
    km = seg(_C_KM, MOBA_W)
    km_ref[0] = km.astype(bf16)
    vm_ref[0] = seg(_C_VM, MOBA_W).astype(bf16)
    kmean_ref[0, 0] = jnp.mean(km.reshape(TM // MOBA_BLOCK, MOBA_BLOCK, MOBA_W), axis=1)

    lane = lax.broadcasted_iota(jnp.int32, (1, V7X_LANES), 1)
    slot = lane % FGATE_SLOT
    used = (lane < FGATE_SLOT * FOX_HEADS) & (slot < 2 * N_SPLIT)
    fl = seg(_C_FG, V7X_LANES) + fb_ref[...]
    logf = jnp.where(used, jnp.minimum(fl, 0.0) - jnp.log1p(jnp.exp(-jnp.abs(fl))), 0.0)

    @pl.when(t == 0)
    def _():
        carry_ref[...] = jnp.zeros_like(carry_ref)

    fcum = _dot(tril_ref[...], jnp.concatenate(_split3(logf), axis=0)) + carry_ref[0:1, :]
    carry_ref[0:1, :] = fcum[TM - 1:TM, :]
    hi, mid, lo = (p.astype(jnp.float32) for p in _split3(fcum * LOG2E))
    parts = jnp.where(slot % N_SPLIT == 0, hi, jnp.where(slot % N_SPLIT == 1, mid, lo))
    eq_ref[0] = jnp.where(used, jnp.where(slot < N_SPLIT, parts, 1.0), 0.0).astype(bf16)
    ek_ref[0] = jnp.where(used, jnp.where(slot < N_SPLIT, 1.0, -parts), 0.0).astype(bf16)

    cos = cos_ref[...]
    sin = sin_ref[...]
    mla_scale = (MLA_NOPE_DIM + MLA_ROPE_DIM) ** -0.5 * LOG2E
    cq = _rms(seg(_C_CQ, MLA_Q_RANK), gq_ref[...]).astype(bf16)
    n_nope = MLA_HEADS * MLA_NOPE_DIM
    n_rot = MLA_HEADS * MLA_ROPE_DIM
    qn_ref[0] = (_dot(cq, wuq_ref[:, 0:n_nope]) * mla_scale).astype(bf16)
    q_rot = _dot(cq, wuq_ref[:, n_nope:n_nope + n_rot])
    q_rot_sw = _dot(cq, wuq_ref[:, n_nope + n_rot:n_nope + 2 * n_rot])
    cos2 = jnp.concatenate([cos] * (n_rot // V7X_LANES), axis=1)
    sin2 = jnp.concatenate([sin] * (n_rot // V7X_LANES), axis=1)
    qr_ref[0] = ((q_rot * cos2 + q_rot_sw * sin2) * mla_scale).astype(bf16)
    ckv = _rms(seg(_C_CKV, MLA_KV_RANK), gkv_ref[...]).astype(bf16)
    kn_ref[0] = _dot(ckv, wukv_ref[:, 0:n_nope]).astype(bf16)
    vc_ref[0] = _dot(ckv, wukv_ref[:, n_nope:n_nope + MLA_W]).astype(bf16)
    kr_ref[0] = (seg(_C_KR, V7X_LANES) * cos + seg(_C_KRS, V7X_LANES) * sin).astype(bf16)


def _proj(x, mod_l, gpre, win, fbias, gq, wuq, gkv, wukv, cos_t, sin_t, tril):
    batch, seq, d = x.shape
    nt = seq // TM
    bf16 = jnp.bfloat16

    def tok(width):
        return pl.BlockSpec((1, TM, width), lambda b, t: (b, t, 0))

    def out(width):
        return jax.ShapeDtypeStruct((batch, seq, width), bf16)

    widths = dict(qf=FOX_W, eq=V7X_LANES, kf=FOX_W, ek=V7X_LANES, vf=FOX_W, qm=MOBA_W, km=MOBA_W, vm=MOBA_W)
    mla_widths = dict(qn=MLA_HEADS * MLA_NOPE_DIM, qr=MLA_HEADS * MLA_ROPE_DIM, kn=MLA_HEADS * MLA_NOPE_DIM,
                      kr=V7X_LANES, vc=MLA_W)
    nb = TM // MOBA_BLOCK
    out_shape = ([out(w) for w in widths.values()]
                 + [jax.ShapeDtypeStruct((batch, nt, nb, MOBA_W), jnp.float32)]
                 + [out(w) for w in mla_widths.values()])
    out_specs = ([tok(w) for w in widths.values()]
                 + [pl.BlockSpec((1, 1, nb, MOBA_W), lambda b, t: (b, t, 0, 0))]
                 + [tok(w) for w in mla_widths.values()])
    res = pl.pallas_call(
        _proj_kernel,
        out_shape=out_shape,
        grid=(batch, nt),
        in_specs=[
            tok(d),
            pl.BlockSpec((1, 1, 6 * d), lambda b, t: (b, 0, 0)),
            _const_spec((1, d)),
            _const_spec(win.shape),
            _const_spec((1, V7X_LANES)),
            _const_spec((1, MLA_Q_RANK)),
            _const_spec(wuq.shape),
            _const_spec((1, MLA_KV_RANK)),
            _const_spec(wukv.shape),
            pl.BlockSpec((TM, V7X_LANES), lambda b, t: (t, 0)),
            pl.BlockSpec((TM, V7X_LANES), lambda b, t: (t, 0)),
            _const_spec(tril.shape),
        ],
        out_specs=out_specs,
        scratch_shapes=[pltpu.VMEM((8, V7X_LANES), jnp.float32)],
        compiler_params=_params("arbitrary", "arbitrary"),
        name="in_proj",
    )(x, mod_l, gpre, win, fbias, gq, wuq, gkv, wukv, cos_t, sin_t, tril)
    names = list(widths) + ["kmean"] + list(mla_widths)
    r = dict(zip(names, res))
    r["kmean"] = r["kmean"].reshape(batch, seq // MOBA_BLOCK, MOBA_W)
    return r


def _transpose_to_bf16(x):
    return x.astype(jnp.float32).T.astype(jnp.bfloat16)


def _head_rows(x_pair, e):
    row = lax.broadcasted_iota(jnp.int32, x_pair.shape, 0)
    keep = (row >= HEAD_LANES) if e else (row < HEAD_LANES)
    return jnp.where(keep, x_pair, 0.0)


def _causal_bias_t():
    key = lax.broadcasted_iota(jnp.int32, (TQ, TQ), 0)
    qry = lax.broadcasted_iota(jnp.int32, (TQ, TQ), 1)
    return jnp.where(key <= qry, 0.0, MASK_NEG)


def _tile_rows(t):
    return pl.ds(pl.multiple_of(t * TQ, TQ), TQ)


class _FlashScratch(NamedTuple):
    vt: object
    rhs: object
    s: object
    p: object
    alpha: object
    m: object
    acc: object


def _flash_begin(v_ref, sc):
    n_heads = sc.vt.shape[1] // V_ROWS
    extra = lax.broadcasted_iota(jnp.int32, (V_ROWS - HEAD_LANES, TQ), 0)
    ones_row = jnp.where(extra == 0, 1.0, 0.0).astype(jnp.bfloat16)
    for c in range(sc.vt.shape[0]):
        v_t = _transpose_to_bf16(v_ref[0, c * TQ:(c + 1) * TQ, :])
        for h in range(n_heads):
            sc.vt[c, h * V_ROWS:h * V_ROWS + HEAD_LANES, :] = v_t[h * HEAD_LANES:(h + 1) * HEAD_LANES]
            sc.vt[c, h * V_ROWS + HEAD_LANES:(h + 1) * V_ROWS, :] = ones_row
    sc.m[...] = jnp.full(sc.m.shape, MASK_NEG, jnp.float32)
    sc.acc[...] = jnp.zeros(sc.acc.shape, jnp.float32)


def _flash_pipeline(tab_ref, segments, heads, n_heads, lhs_tile, sc):
    slot = {h: h - heads[0] for h in heads}
    n_steps = sum(count for count, _ in segments)

    def ij(t):
        t = jnp.clip(t, 0, n_steps - 1)
        return tab_ref[0, t], tab_ref[1, t]

    def scores(t, bias_fn):
        i, j = ij(t)
        lhs = {h // 2: lhs_tile(j, h // 2) for h in heads if h % 2 == 0}
        out = {}
        for h in heads:
            s = _dot(lhs[h // 2], sc.rhs[i * n_heads + h])
            b = bias_fn(i, j, h)
            out[h] = s if b is None else s + b
        return out

    def store_scores(vals):
        for h in heads:
            sc.s[slot[h]] = vals[h]

    def softmax(t):
        i, _ = ij(t)
        s_val = {h: sc.s[slot[h]] for h in heads}
        m_old = {h: sc.m[i * n_heads + h] for h in heads}
        m_new = {h: jnp.maximum(m_old[h], jnp.max(s_val[h], axis=0, keepdims=True)) for h in heads}
        alpha = {h: jnp.exp2(m_old[h] - m_new[h]) for h in heads}
        probs = {h: jnp.exp2(s_val[h] - m_new[h]) for h in heads}
        for h in heads:
            st = i * n_heads + h
            sc.m[st] = m_new[h]
            sc.alpha[slot[h]] = alpha[h]
            sc.p[slot[h]] = probs[h].astype(jnp.bfloat16)

    def values(t):
        i, j = ij(t)
        pv = {h: _dot(sc.vt[j, h * V_ROWS:(h + 1) * V_ROWS, :], sc.p[slot[h]]) for h in heads}
        for h in heads:
            st = i * n_heads + h
            sc.acc[st] = sc.alpha[slot[h]] * sc.acc[st] + pv[h]

    def prologue(_, carry):
        for h in heads:
            sc.alpha[slot[h]] = jnp.ones(sc.alpha.shape[1:], jnp.float32)
            sc.p[slot[h]] = jnp.zeros(sc.p.shape[1:], jnp.bfloat16)
        store_scores(scores(0, segments[0][1]))
        return carry

    lax.fori_loop(0, tab_ref[0, n_steps], prologue, 0)

    first = 0
    for count, bias_fn in segments:

        def body(t, carry, bias_fn=bias_fn):
            nxt = scores(t + 1, bias_fn)
            values(t - 1)
            softmax(t)
            store_scores(nxt)
            return carry

        lax.fori_loop(max(first - 1, 0), first + count - 1, body, 0)
        first += count
    values(n_steps - 2)
    softmax(n_steps - 1)
    values(n_steps - 1)


def _flash_finish(n_heads, sc, g_ref, o_ref):
    def body(iq, carry):
        heads = []
        for h in range(n_heads):
            acc = sc.acc[iq * n_heads + h]
            heads.append(acc[0:HEAD_LANES] * (1.0 / acc[HEAD_LANES:HEAD_LANES + 1]))
        o_t = jnp.concatenate(heads, axis=0)
        o_ref[0, _tile_rows(iq), :] = _rms(o_t.T, g_ref[...]).astype(o_ref.dtype)
        return carry

    lax.fori_loop(0, o_ref.shape[1] // TQ, body, 0, unroll=SETUP_UNROLL)


def _causal_segments(n_tiles):
    return [(n_tiles * (n_tiles - 1) // 2, lambda i, j, h: None), (n_tiles, lambda i, j, h: _causal_bias_t())]


def _fox_kernel(tab_ref, q_ref, eq_ref, k_ref, ek_ref, v_ref, g_ref, o_ref, *scratch):
    sc = _FlashScratch(*scratch)
    n_tiles = q_ref.shape[1] // TQ
    _flash_begin(v_ref, sc)

    def build_rhs(iq, carry):
        rows = _tile_rows(iq)
        q_t = q_ref[0, rows, :].astype(jnp.float32).T
        eq_t = eq_ref[0, rows, :].astype(jnp.float32).T
        slot_head = lax.broadcasted_iota(jnp.int32, eq_t.shape, 0) // FGATE_SLOT
        for h in range(FOX_HEADS):
            pair, e = divmod(h, 2)
            q_h = _head_rows(q_t[pair * V7X_LANES:(pair + 1) * V7X_LANES], e)
            e_h = jnp.where(slot_head == h, eq_t, 0.0)
            sc.rhs[iq * FOX_HEADS + h] = jnp.concatenate([q_h, e_h], axis=0).astype(jnp.bfloat16)
        return carry

    lax.fori_loop(0, n_tiles, build_rhs, 0, unroll=SETUP_UNROLL)

    def lhs_tile(j, pair):
        rows = _tile_rows(j)
        return jnp.concatenate([k_ref[0, rows, pair * V7X_LANES:(pair + 1) * V7X_LANES], ek_ref[0, rows, :]], axis=1)

    _flash_pipeline(tab_ref, _causal_segments(n_tiles), range(FOX_HEADS), FOX_HEADS, lhs_tile, sc)
    _flash_finish(FOX_HEADS, sc, g_ref, o_ref)


def _moba_kernel(tab_ref, q_ref, k_ref, v_ref, kmean_ref, bias_ref, g_ref, o_ref, *scratch):
    sc = _FlashScratch(*scratch)
    f32, bf16 = jnp.float32, jnp.bfloat16
    n_tiles = q_ref.shape[1] // TQ
    n_blocks = kmean_ref.shape[1]
    _flash_begin(v_ref, sc)
    sel_rows = 16
    blk = lax.broadcasted_iota(jnp.int32, (sel_rows, TQ), 0)
    lane = lax.broadcasted_iota(jnp.int32, (1, V7X_LANES), 1)

    def build_rhs(iq, carry):
        q_t = q_ref[0, _tile_rows(iq), :].astype(f32).T
        for h in range(MOBA_HEADS):
            pair, e = divmod(h, 2)
            q_h = _head_rows(q_t[pair * V7X_LANES:(pair + 1) * V7X_LANES], e)
            q_hb = q_h.astype(bf16)
            kmean = kmean_ref[0, :, pair * V7X_LANES:(pair + 1) * V7X_LANES]
            km_hi = kmean.astype(bf16).astype(f32)
            gate_lhs = jnp.concatenate([jnp.concatenate([km_hi, kmean - km_hi], axis=1),
                                        jnp.zeros((sel_rows - n_blocks, 2 * V7X_LANES), f32)], axis=0).astype(bf16)
            gate = _dot(gate_lhs, jnp.concatenate([q_hb, q_hb], axis=0))
            beaten = jnp.zeros((sel_rows, TQ), f32)
            for mblk in range(n_blocks - 1):
                gm = gate[mblk:mblk + 1, :]
                wins = (gm > gate) | ((gm == gate) & (mblk < blk))
                beaten = beaten + jnp.where(wins, jnp.where(mblk < iq, 1.0, 0.0), 0.0)
            keep = ((blk < iq) & (beaten < MOBA_TOPK)) | (blk == iq)
            sel = jnp.where(keep, 0.0, MASK_NEG)[0:n_blocks]
            far = lax.broadcasted_iota(jnp.int32, (n_blocks, TQ), 0) <= iq - 2
            c = bias_ref[h, 2, 0:1, :]
            c_hi = c.astype(bf16).astype(f32)
            extras = [sel, jnp.where(far, c_hi, 0.0), jnp.where(far, c - c_hi, 0.0)]
            pad = jnp.zeros((V7X_LANES - len(extras) * n_blocks, TQ), f32)
            sc.rhs[iq * MOBA_HEADS + h] = jnp.concatenate([q_h] + extras + [pad], axis=0).astype(bf16)
        return carry

    lax.fori_loop(0, n_tiles, build_rhs, 0, unroll=SETUP_UNROLL)

    def lhs_tile(j, pair):
        hit = (lane % n_blocks == j) & (lane < 3 * n_blocks)
        onehot = jnp.broadcast_to(jnp.where(hit, 1.0, 0.0).astype(bf16), (TQ, V7X_LANES))
        return jnp.concatenate([k_ref[0, _tile_rows(j), pair * V7X_LANES:(pair + 1) * V7X_LANES], onehot], axis=1)

    n_far = (n_tiles - 1) * (n_tiles - 2) // 2
    segments = [(n_far, lambda i, j, h: None), (n_tiles - 1, lambda i, j, h: bias_ref[h, 1]),
                (n_tiles, lambda i, j, h: bias_ref[h, 0])]
    _flash_pipeline(tab_ref, segments, range(MOBA_HEADS), MOBA_HEADS, lhs_tile, sc)
    _flash_finish(MOBA_HEADS, sc, g_ref, o_ref)


def _mla_kernel(tab_ref, qn_ref, qr_ref, kn_ref, kr_ref, v_ref, g_ref, o_ref, *scratch):
    sc = _FlashScratch(*scratch)
    n_tiles = qn_ref.shape[1] // TQ
    _flash_begin(v_ref, sc)
    heads_per_rot = V7X_LANES // MLA_ROPE_DIM
    rot_slot = lax.broadcasted_iota(jnp.int32, (V7X_LANES, TQ), 0) // MLA_ROPE_DIM

    def build_rhs(iq, carry):
        rows = _tile_rows(iq)
        qn_t = qn_ref[0, rows, :].astype(jnp.float32).T
        qr_t = qr_ref[0, rows, :].astype(jnp.float32).T
        for h in range(MLA_HEADS):
            pair, e = divmod(h, 2)
            quad, slot = divmod(h, heads_per_rot)
            q_h = _head_rows(qn_t[pair * V7X_LANES:(pair + 1) * V7X_LANES], e)
            r_h = jnp.where(rot_slot == slot, qr_t[quad * V7X_LANES:(quad + 1) * V7X_LANES], 0.0)
            sc.rhs[iq * MLA_HEADS + h] = jnp.concatenate([q_h, r_h], axis=0).astype(jnp.bfloat16)
        return carry

    lax.fori_loop(0, n_tiles, build_rhs, 0, unroll=SETUP_UNROLL)

    def lhs_tile(j, pair):
        rows = _tile_rows(j)
        return jnp.concatenate([kn_ref[0, rows, pair * V7X_LANES:(pair + 1) * V7X_LANES], kr_ref[0, rows, :]], axis=1)

    for h0 in range(0, MLA_HEADS, PIPE_HEADS):
        _flash_pipeline(tab_ref, _causal_segments(n_tiles), range(h0, h0 + PIPE_HEADS), MLA_HEADS, lhs_tile, sc)
    _flash_finish(MLA_HEADS, sc, g_ref, o_ref)


def _tile_pairs(n_tiles, split_previous):
    near = 2 if split_previous else 1
    pairs = [(i, j) for i in range(n_tiles) for j in range(i - near + 1)]
    for d in range(near - 1, -1, -1):
        pairs += [(i, i - d) for i in range(d, n_tiles)]
    return jnp.asarray(np.array(pairs + [(1, 1)], np.int32).T)


def _attn_call(kernel, name, n_heads, split_previous, arrays, const_arrays):
    batch, seq, _ = arrays[0].shape
    n_tiles = seq // TQ
    out_width = n_heads * HEAD_LANES
    row = lambda a: pl.BlockSpec((1,) + a.shape[1:], lambda b: (b, 0, 0))
    return pl.pallas_call(
        kernel,
        out_shape=jax.ShapeDtypeStruct((batch, seq, out_width), jnp.bfloat16),
        grid=(batch,),
        in_specs=([pl.BlockSpec(memory_space=pltpu.SMEM)] + [row(a) for a in arrays]
                  + [_const_spec(a.shape) for a in const_arrays]),
        out_specs=pl.BlockSpec((1, seq, out_width), lambda b: (b, 0, 0)),
        scratch_shapes=[
            pltpu.VMEM((n_tiles, n_heads * V_ROWS, TQ), jnp.bfloat16),
            pltpu.VMEM((n_tiles * n_heads, 2 * V7X_LANES, TQ), jnp.bfloat16),
            pltpu.VMEM((PIPE_HEADS, TQ, TQ), jnp.float32),
            pltpu.VMEM((PIPE_HEADS, TQ, TQ), jnp.bfloat16),
            pltpu.VMEM((PIPE_HEADS, 1, TQ), jnp.float32),
            pltpu.VMEM((n_tiles * n_heads, 1, TQ), jnp.float32),
            pltpu.VMEM((n_tiles * n_heads, V_ROWS, TQ), jnp.float32),
        ],
        compiler_params=_params("arbitrary"),
        name=name,
    )(_tile_pairs(n_tiles, split_previous), *arrays, *const_arrays)


def _bias_tile_kernel(table_ref, bucket_ref, o_ref):
    h = pl.program_id(0)
    for k in range(bucket_ref.shape[0]):
        bkt = bucket_ref[k]
        tile = jnp.full(bkt.shape, MASK_NEG, jnp.float32)
        for b in range(T5_BUCKETS):
            tile = jnp.where(bkt == b, table_ref[h, b] * LOG2E, tile)
        o_ref[0, k] = tile


def _moba_bias_tiles(t5_table):
    bucket = _t5_bucket_table(3 * TQ)
    key = np.arange(TQ)[:, None]
    qry = np.arange(TQ)[None, :]
    kinds = []
    for k in range(3):
        dist = qry - key + k * TQ
        kinds.append(np.where(dist >= 0, bucket[np.maximum(dist, 0)], -1))
    buckets = jnp.asarray(np.stack(kinds), jnp.int32)
    return pl.pallas_call(
        _bias_tile_kernel,
        out_shape=jax.ShapeDtypeStruct((MOBA_HEADS, 3, TQ, TQ), jnp.float32),
        grid=(MOBA_HEADS,),
        in_specs=[pl.BlockSpec(memory_space=pltpu.SMEM), _const_spec(buckets.shape)],
        out_specs=pl.BlockSpec((1, 3, TQ, TQ), lambda h: (h, 0, 0, 0)),
        compiler_params=_params("arbitrary"),
        name="t5_bias_tiles",
    )(t5_table.T, buckets)


def _mix_ffn_kernel(x_ref, of_ref, om_ref, oc_ref, mod_ref, gmix_ref, gpre_ref, gpost_ref,
                    wout_ref, wgu_ref, wd_ref, o_ref):
    bf16 = jnp.bfloat16
    gate_a = mod_ref[0, :, 2 * D_MODEL:3 * D_MODEL]
    shift = mod_ref[0, :, 3 * D_MODEL:4 * D_MODEL]
    scale = mod_ref[0, :, 4 * D_MODEL:5 * D_MODEL]
    gate_f = mod_ref[0, :, 5 * D_MODEL:6 * D_MODEL]
    o = jnp.concatenate([of_ref[0], om_ref[0], oc_ref[0]], axis=1)
    x = x_ref[0] + gate_a * _rms(_dot(o, wout_ref[...]), gmix_ref[...])
    h = (_rms(x, gpre_ref[...]) * (1.0 + scale) + shift).astype(bf16)
    acc = None
    for c0, c1 in FFN_CHUNKS:
        g = _dot(h, wgu_ref[:, c0:c1])
        u = _dot(h, wgu_ref[:, D_FF + c0:D_FF + c1])
        a = (g * jax.nn.sigmoid(g) * u).astype(bf16)
        part = _dot(a, wd_ref[c0:c1, :])
        acc = part if acc is None else acc + part
    o_ref[0] = x + gate_f * _rms(acc, gpost_ref[...])


def _mix_ffn(x, o_f, o_m, o_c, mod_l, gmix, gpre, gpost, w_out, wgu, wd):
    batch, seq, d = x.shape

    def tok(width):
        return pl.BlockSpec((1, TM, width), lambda b, t: (b, t, 0))

    return pl.pallas_call(
        _mix_ffn_kernel,
        out_shape=jax.ShapeDtypeStruct(x.shape, x.dtype),
        grid=(batch, seq // TM),
        in_specs=[tok(d), tok(FOX_W), tok(MOBA_W), tok(MLA_W),
                  pl.BlockSpec((1, 1, 6 * d), lambda b, t: (b, 0, 0)),
                  _const_spec((1, d)), _const_spec((1, d)), _const_spec((1, d)),
                  _const_spec(w_out.shape), _const_spec(wgu.shape), _const_spec(wd.shape)],
        out_specs=tok(d),
        compiler_params=_params("arbitrary", "arbitrary"),
        name="mix_ffn",
    )(x, o_f, o_m, o_c, mod_l, gmix, gpre, gpost, w_out, wgu, wd)


def _rope_tables(seq):
    half = MLA_ROPE_DIM // 2
    inv_freq = 1.0 / (ROPE_THETA ** (jnp.arange(half, dtype=jnp.float32) / half))
    ang = jnp.arange(seq).astype(jnp.float32)[:, None] * inv_freq[None, :]
    reps = V7X_LANES // MLA_ROPE_DIM
    cos = jnp.tile(jnp.concatenate([jnp.cos(ang), jnp.cos(ang)], axis=1), (1, reps))
    sin = jnp.tile(jnp.concatenate([-jnp.sin(ang), jnp.sin(ang)], axis=1), (1, reps))
    return cos, sin


def kernel(x, c, t5_table, w_ada, b_ada, g_mix_pre, g_mix_post, w_in, b_forget, g_q_lat, w_uq, g_kv_lat, w_ukv,
           g_group, w_out, g_ffn_pre, g_ffn_post, w_gate_up, w_down):
    batch, seq, d = x.shape
    assert d == D_MODEL and seq % TM == 0 and TM % MOBA_BLOCK == 0 and TQ == MOBA_BLOCK
    bf16 = jnp.bfloat16
    in_cols, uq_cols, ukv_cols = _in_proj_columns(), _uq_columns(), _ukv_columns()
    cos_t, sin_t = _rope_tables(seq)
    tril = np.tril(np.ones((TM, TM), np.float32))
    tril = jnp.asarray(np.concatenate([tril] * N_SPLIT, axis=1), bf16)
    moba_bias = _moba_bias_tiles(t5_table)
    fg_lane = np.arange(V7X_LANES)
    fg_used = (fg_lane < FGATE_SLOT * FOX_HEADS) & (fg_lane % FGATE_SLOT < 2 * N_SPLIT)
    fg_head = np.minimum(fg_lane // FGATE_SLOT, FOX_HEADS - 1)

    mod = _ada_mod(c, w_ada, b_ada)
    for l in range(DEPTH):
        mod_l = mod[l].reshape(batch, 1, 6 * d)
        fbias = jnp.where(jnp.asarray(fg_used), b_forget[l][fg_head], 0.0).reshape(1, V7X_LANES)
        pr = _proj(x, mod_l, g_mix_pre[l].reshape(1, d),
                   _take_columns(w_in[l], in_cols).astype(bf16), fbias,
                   g_q_lat[l].reshape(1, -1), _take_columns(w_uq[l], uq_cols).astype(bf16),
                   g_kv_lat[l].reshape(1, -1), _take_columns(w_ukv[l], ukv_cols).astype(bf16),
                   cos_t, sin_t, tril)
        g_a = g_group[l, :FOX_W].reshape(1, -1)
        g_b = g_group[l, FOX_W:FOX_W + MOBA_W].reshape(1, -1)
        g_c = g_group[l, FOX_W + MOBA_W:].reshape(1, -1)
        o_f = _attn_call(_fox_kernel, "fox_attn", FOX_HEADS, False,
                         [pr["qf"], pr["eq"], pr["kf"], pr["ek"], pr["vf"]], [g_a])
        o_m = _attn_call(_moba_kernel, "moba_attn", MOBA_HEADS, True,
                         [pr["qm"], pr["km"], pr["vm"], pr["kmean"]], [moba_bias, g_b])
        o_c = _attn_call(_mla_kernel, "mla_attn", MLA_HEADS, False,
                         [pr["qn"], pr["qr"], pr["kn"], pr["kr"], pr["vc"]], [g_c])
        x = _mix_ffn(x, o_f, o_m, o_c, mod_l, g_mix_post[l].reshape(1, d), g_ffn_pre[l].reshape(1, d),
                     g_ffn_post[l].reshape(1, d), w_out[l].astype(bf16), w_gate_up[l].astype(bf16),
                     w_down[l].astype(bf16))
    return x
```

```python
import math
from typing import NamedTuple

import jax
import jax.numpy as jnp
import numpy as np
from jax import lax
from jax.experimental import pallas as pl
from jax.experimental.pallas import tpu as pltpu

D_MODEL = 1024
DEPTH = 2
FOX_HEADS = 4
FOX_HEAD_DIM = 64
MOBA_HEADS = 4
MOBA_HEAD_DIM = 64
MOBA_BLOCK = 256
MOBA_TOPK = 3
MLA_HEADS = 8
MLA_NOPE_DIM = 64
MLA_ROPE_DIM = 32
MLA_V_DIM = 64
MLA_Q_RANK = 256
MLA_KV_RANK = 128
ROPE_THETA = 10000.0
T5_BUCKETS = 32
T5_MAX_DISTANCE = 128
D_FF = -(-8 * D_MODEL // (3 * 256)) * 256
RMS_EPS = 1e-6
FOX_W = FOX_HEADS * FOX_HEAD_DIM
MOBA_W = MOBA_HEADS * MOBA_HEAD_DIM
MLA_W = MLA_HEADS * MLA_V_DIM
MIX_WIDTH = FOX_W + MOBA_W + MLA_W
IN_SIZES = (FOX_W, FOX_W, FOX_W, FOX_HEADS, MOBA_W, MOBA_W, MOBA_W, MLA_Q_RANK, MLA_KV_RANK, MLA_ROPE_DIM)

V7X_LANES = 128
V7X_VMEM_LIMIT_BYTES = 56 * 1024 * 1024

TM = 512
TQ = 256
PIPE_HEADS = 4
SETUP_UNROLL = 4
FFN_CHUNKS = ((0, 1024), (1024, 2048), (2048, D_FF))

HEAD_LANES = 64
MASK_NEG = -1e30
LOG2E = math.log2(math.e)
V_ROWS = 80
FGATE_SLOT = 8
N_SPLIT = 3

_C_QF, _C_KF, _C_VF = 0, 256, 512
_C_QM, _C_KM, _C_VM = 768, 1024, 1280
_C_CQ, _C_CKV = 1536, 1792
_C_FG, _C_KR, _C_KRS = 1920, 2048, 2176
IN_WIDTH_PADDED = 2304


def _in_proj_columns():
    off = np.cumsum((0,) + IN_SIZES)
    q_f, k_f, v_f, f_g, q_m, k_m, v_m, c_q, c_kv, k_r = (np.arange(off[i], off[i + 1]) for i in range(10))
    fg = np.full((V7X_LANES,), -1, np.int64)
    for h in range(FOX_HEADS):
        fg[FGATE_SLOT * h:FGATE_SLOT * h + 2 * N_SPLIT] = f_g[h]
    half = MLA_ROPE_DIM // 2
    kr4 = np.tile(k_r, V7X_LANES // MLA_ROPE_DIM)
    kr4s = np.tile(np.roll(k_r, -half), V7X_LANES // MLA_ROPE_DIM)
    cols = np.concatenate([q_f, k_f, v_f, q_m, k_m, v_m, c_q, c_kv, fg, kr4, kr4s])
    assert cols.shape == (IN_WIDTH_PADDED,)
    return cols


def _uq_columns():
    per = MLA_NOPE_DIM + MLA_ROPE_DIM
    half = MLA_ROPE_DIM // 2
    nope = np.concatenate([np.arange(h * per, h * per + MLA_NOPE_DIM) for h in range(MLA_HEADS)])
    rot = np.concatenate([np.arange(h * per + MLA_NOPE_DIM, (h + 1) * per) for h in range(MLA_HEADS)])
    rots = np.concatenate([np.roll(np.arange(h * per + MLA_NOPE_DIM, (h + 1) * per), -half) for h in range(MLA_HEADS)])
    return np.concatenate([nope, rot, rots])


def _ukv_columns():
    per = MLA_NOPE_DIM + MLA_V_DIM
    nope = np.concatenate([np.arange(h * per, h * per + MLA_NOPE_DIM) for h in range(MLA_HEADS)])
    val = np.concatenate([np.arange(h * per + MLA_NOPE_DIM, (h + 1) * per) for h in range(MLA_HEADS)])
    return np.concatenate([nope, val])


def _take_columns(w, cols):
    g = jnp.take(w, jnp.asarray(np.maximum(cols, 0), jnp.int32), axis=1)
    return jnp.where(jnp.asarray(cols >= 0)[None, :], g, 0.0)


def _t5_bucket_table(n):
    d = np.arange(n, dtype=np.int32)
    max_exact = T5_BUCKETS // 2
    nf = np.maximum(d, max_exact).astype(np.float32)
    ratio = np.log(nf / np.float32(max_exact)) / np.float32(math.log(T5_MAX_DISTANCE / max_exact))
    large = max_exact + (ratio.astype(np.float32) * np.float32(T5_BUCKETS - max_exact)).astype(np.int32)
    large = np.minimum(large, T5_BUCKETS - 1)
    return np.where(d < max_exact, d, large).astype(np.int32)


def _const_spec(shape):
    nd = len(shape)
    return pl.BlockSpec(shape, lambda *_: (0,) * nd, pipeline_mode=pl.Buffered(1))


def _params(*sem):
    return pltpu.CompilerParams(dimension_semantics=sem, vmem_limit_bytes=V7X_VMEM_LIMIT_BYTES)


def _rms(x, g):
    return x * lax.rsqrt(jnp.mean(x * x, axis=-1, keepdims=True) + RMS_EPS) * g


def _split3(v):
    hi = v.astype(jnp.bfloat16)
    r1 = v - hi.astype(jnp.float32)
    mid = r1.astype(jnp.bfloat16)
    lo = (r1 - mid.astype(jnp.float32)).astype(jnp.bfloat16)
    return hi, mid, lo


def _dot(a, b):
    return jnp.dot(a, b, preferred_element_type=jnp.float32)


def _ada_kernel(c_ref, w_ref, b_ref, o_ref):
    c = c_ref[...]
    act = (c * jax.nn.sigmoid(c)).astype(jnp.bfloat16)
    o_ref[0] = _dot(act, w_ref[0].astype(jnp.bfloat16)) + b_ref[0]


def _ada_mod(c, w_ada, b_ada):
    depth, d, six_d = w_ada.shape
    batch = c.shape[0]
    n_col = six_d // d
    return pl.pallas_call(
        _ada_kernel,
        out_shape=jax.ShapeDtypeStruct((depth, batch, six_d), jnp.float32),
        grid=(depth, n_col),
        in_specs=[
            pl.BlockSpec((batch, d), lambda l, j: (0, 0)),
            pl.BlockSpec((1, d, d), lambda l, j: (l, 0, j)),
            pl.BlockSpec((1, 1, d), lambda l, j: (l, 0, j)),
        ],
        out_specs=pl.BlockSpec((1, batch, d), lambda l, j: (l, 0, j)),
        compiler_params=_params("arbitrary", "arbitrary"),
        name="ada_mod",
    )(c, w_ada, b_ada.reshape(depth, 1, six_d))


def _proj_kernel(x_ref, mod_ref, gpre_ref, win_ref, fb_ref, gq_ref, wuq_ref, gkv_ref, wukv_ref,
                 cos_ref, sin_ref, tril_ref,
                 qf_ref, eq_ref, kf_ref, ek_ref, vf_ref, qm_ref, km_ref, vm_ref, kmean_ref,
                 qn_ref, qr_ref, kn_ref, kr_ref, vc_ref, carry_ref):
    t = pl.program_id(1)
    bf16 = jnp.bfloat16
    x = x_ref[0]
    shift = mod_ref[0, :, 0:D_MODEL]
    scale = mod_ref[0, :, D_MODEL:2 * D_MODEL]
    h = (_rms(x, gpre_ref[...]) * (1.0 + scale) + shift).astype(bf16)

    def seg(c0, width):
        return _dot(h, win_ref[:, c0:c0 + width])

    qf_ref[0] = (seg(_C_QF, FOX_W) * (FOX_HEAD_DIM ** -0.5 * LOG2E)).astype(bf16)
    kf_ref[0] = seg(_C_KF, FOX_W).astype(bf16)
    vf_ref[0] = seg(_C_VF, FOX_W).astype(bf16)
    qm_ref[0] = (seg(_C_QM, MOBA_W) * (MOBA_HEAD_DIM ** -0.5 * LOG2E)).astype(bf16)
    km = seg(_C_KM, MOBA_W)
    km_ref[0] = km.astype(bf16)
    vm_ref[0] = seg(_C_VM, MOBA_W).astype(bf16)
    kmean_ref[0, 0] = jnp.mean(km.reshape(TM // MOBA_BLOCK, MOBA_BLOCK, MOBA_W), axis=1)

    lane = lax.broadcasted_iota(jnp.int32, (1, V7X_LANES), 1)
    slot = lane % FGATE_SLOT
    used = (lane < FGATE_SLOT * FOX_HEADS) & (slot < 2 * N_SPLIT)
    fl = seg(_C_FG, V7X_LANES) + fb_ref[...]
    logf = jnp.where(used, jnp.minimum(fl, 0.0) - jnp.log1p(jnp.exp(-jnp.abs(fl))), 0.0)

    @pl.when(t == 0)
    def _():
        carry_ref[...] = jnp.zeros_like(carry_ref)

    fcum = _dot(tril_ref[...], jnp.concatenate(_split3(logf), axis=0)) + carry_ref[0:1, :]
    carry_ref[0:1, :] = fcum[TM - 1:TM, :]
    hi, mid, lo = (p.astype(jnp.float32) for p in _split3(fcum * LOG2E))
    parts = jnp.where(slot % N_SPLIT == 0, hi, jnp.where(slot % N_SPLIT == 1, mid, lo))
    eq_ref[0] = jnp.where(used, jnp.where(slot < N_SPLIT, parts, 1.0), 0.0).astype(bf16)
    ek_ref[0] = jnp.where(used, jnp.where(slot < N_SPLIT, 1.0, -parts), 0.0).astype(bf16)

    cos = cos_ref[...]
    sin = sin_ref[...]
    mla_scale = (MLA_NOPE_DIM + MLA_ROPE_DIM) ** -0.5 * LOG2E
    cq = _rms(seg(_C_CQ, MLA_Q_RANK), gq_ref[...]).astype(bf16)
    n_nope = MLA_HEADS * MLA_NOPE_DIM
    n_rot = MLA_HEADS * MLA_ROPE_DIM
    qn_ref[0] = (_dot(cq, wuq_ref[:, 0:n_nope]) * mla_scale).astype(bf16)
    q_rot = _dot(cq, wuq_ref[:, n_nope:n_nope + n_rot])
    q_rot_sw = _dot(cq, wuq_ref[:, n_nope + n_rot:n_nope + 2 * n_rot])
    cos2 = jnp.concatenate([cos] * (n_rot // V7X_LANES), axis=1)
    sin2 = jnp.concatenate([sin] * (n_rot // V7X_LANES), axis=1)
    qr_ref[0] = ((q_rot * cos2 + q_rot_sw * sin2) * mla_scale).astype(bf16)
    ckv = _rms(seg(_C_CKV, MLA_KV_RANK), gkv_ref[...]).astype(bf16)
    kn_ref[0] = _dot(ckv, wukv_ref[:, 0:n_nope]).astype(bf16)
    vc_ref[0] = _dot(ckv, wukv_ref[:, n_nope:n_nope + MLA_W]).astype(bf16)
    kr_ref[0] = (seg(_C_KR, V7X_LANES) * cos + seg(_C_KRS, V7X_LANES) * sin).astype(bf16)


def _proj(x, mod_l, gpre, win, fbias, gq, wuq, gkv, wukv, cos_t, sin_t, tril):
    batch, seq, d = x.shape
    nt = seq // TM
    bf16 = jnp.bfloat16

    def tok(width):
        return pl.BlockSpec((1, TM, width), lambda b, t: (b, t, 0))

    def out(width):
        return jax.ShapeDtypeStruct((batch, seq, width), bf16)

    widths = dict(qf=FOX_W, eq=V7X_LANES, kf=FOX_W, ek=V7X_LANES, vf=FOX_W, qm=MOBA_W, km=MOBA_W, vm=MOBA_W)
    mla_widths = dict(qn=MLA_HEADS * MLA_NOPE_DIM, qr=MLA_HEADS * MLA_ROPE_DIM, kn=MLA_HEADS * MLA_NOPE_DIM,
                      kr=V7X_LANES, vc=MLA_W)
    nb = TM // MOBA_BLOCK
    out_shape = ([out(w) for w in widths.values()]
                 + [jax.ShapeDtypeStruct((batch, nt, nb, MOBA_W), jnp.float32)]
                 + [out(w) for w in mla_widths.values()])
    out_specs = ([tok(w) for w in widths.values()]
                 + [pl.BlockSpec((1, 1, nb, MOBA_W), lambda b, t: (b, t, 0, 0))]
                 + [tok(w) for w in mla_widths.values()])
    res = pl.pallas_call(
        _proj_kernel,
        out_shape=out_shape,
        grid=(batch, nt),
        in_specs=[
            tok(d),
            pl.BlockSpec((1, 1, 6 * d), lambda b, t: (b, 0, 0)),
            _const_spec((1, d)),
            _const_spec(win.shape),
            _const_spec((1, V7X_LANES)),
            _const_spec((1, MLA_Q_RANK)),
            _const_spec(wuq.shape),
            _const_spec((1, MLA_KV_RANK)),
            _const_spec(wukv.shape),
            pl.BlockSpec((TM, V7X_LANES), lambda b, t: (t, 0)),
            pl.BlockSpec((TM, V7X_LANES), lambda b, t: (t, 0)),
            _const_spec(tril.shape),
        ],
        out_specs=out_specs,
        scratch_shapes=[pltpu.VMEM((8, V7X_LANES), jnp.float32)],
        compiler_params=_params("arbitrary", "arbitrary"),
        name="in_proj",
    )(x, mod_l, gpre, win, fbias, gq, wuq, gkv, wukv, cos_t, sin_t, tril)
    names = list(widths) + ["kmean"] + list(mla_widths)
    r = dict(zip(names, res))
    r["kmean"] = r["kmean"].reshape(batch, seq // MOBA_BLOCK, MOBA_W)
    return r


def _transpose_to_bf16(x):
    return x.astype(jnp.float32).T.astype(jnp.bfloat16)


def _head_rows(x_pair, e):
    row = lax.broadcasted_iota(jnp.int32, x_pair.shape, 0)
    keep = (row >= HEAD_LANES) if e else (row < HEAD_LANES)
    return jnp.where(keep, x_pair, 0.0)


def _causal_bias_t():
    key = lax.broadcasted_iota(jnp.int32, (TQ, TQ), 0)
    qry = lax.broadcasted_iota(jnp.int32, (TQ, TQ), 1)
    return jnp.where(key <= qry, 0.0, MASK_NEG)


def _tile_rows(t):
    return pl.ds(pl.multiple_of(t * TQ, TQ), TQ)


class _FlashScratch(NamedTuple):
    vt: object
    rhs: object
    s: object
    p: object
    alpha: object
    m: object
    acc: object


def _flash_begin(v_ref, sc):
    n_heads = sc.vt.shape[1] // V_ROWS
    extra = lax.broadcasted_iota(jnp.int32, (V_ROWS - HEAD_LANES, TQ), 0)
    ones_row = jnp.where(extra == 0, 1.0, 0.0).astype(jnp.bfloat16)
    for c in range(sc.vt.shape[0]):
        v_t = _transpose_to_bf16(v_ref[0, c * TQ:(c + 1) * TQ, :])
        for h in range(n_heads):
            sc.vt[c, h * V_ROWS:h * V_ROWS + HEAD_LANES, :] = v_t[h * HEAD_LANES:(h + 1) * HEAD_LANES]
            sc.vt[c, h * V_ROWS + HEAD_LANES:(h + 1) * V_ROWS, :] = ones_row
    sc.m[...] = jnp.full(sc.m.shape, MASK_NEG, jnp.float32)
    sc.acc[...] = jnp.zeros(sc.acc.shape, jnp.float32)


def _flash_pipeline(tab_ref, segments, heads, n_heads, lhs_tile, sc):
    slot = {h: h - heads[0] for h in heads}
    n_steps = sum(count for count, _ in segments)

    def ij(t):
        t = jnp.clip(t, 0, n_steps - 1)
        return tab_ref[0, t], tab_ref[1, t]

    def scores(t, bias_fn):
        i, j = ij(t)
        lhs = {h // 2: lhs_tile(j, h // 2) for h in heads if h % 2 == 0}
        out = {}
        for h in heads:
            s = _dot(lhs[h // 2], sc.rhs[i * n_heads + h])
            b = bias_fn(i, j, h)
            out[h] = s if b is None else s + b
        return out

    def store_scores(vals):
        for h in heads:
            sc.s[slot[h]] = vals[h]

    def softmax(t):
        i, _ = ij(t)
        s_val = {h: sc.s[slot[h]] for h in heads}
        m_old = {h: sc.m[i * n_heads + h] for h in heads}
        m_new = {h: jnp.maximum(m_old[h], jnp.max(s_val[h], axis=0, keepdims=True)) for h in heads}
        alpha = {h: jnp.exp2(m_old[h] - m_new[h]) for h in heads}
        probs = {h: jnp.exp2(s_val[h] - m_new[h]) for h in heads}
        for h in heads:
            st = i * n_heads + h
            sc.m[st] = m_new[h]
            sc.alpha[slot[h]] = alpha[h]
            sc.p[slot[h]] = probs[h].astype(jnp.bfloat16)

    def values(t):
        i, j = ij(t)
        pv = {h: _dot(sc.vt[j, h * V_ROWS:(h + 1) * V_ROWS, :], sc.p[slot[h]]) for h in heads}
        for h in heads:
            st = i * n_heads + h
            sc.acc[st] = sc.alpha[slot[h]] * sc.acc[st] + pv[h]

    def prologue(_, carry):
        for h in heads:
            sc.alpha[slot[h]] = jnp.ones(sc.alpha.shape[1:], jnp.float32)
            sc.p[slot[h]] = jnp.zeros(sc.p.shape[1:], jnp.bfloat16)
        store_scores(scores(0, segments[0][1]))
        return carry

    lax.fori_loop(0, tab_ref[0, n_steps], prologue, 0)

    first = 0
    for count, bias_fn in segments:

        def body(t, carry, bias_fn=bias_fn):
            nxt = scores(t + 1, bias_fn)
            values(t - 1)
            softmax(t)
            store_scores(nxt)
            return carry

        lax.fori_loop(max(first - 1, 0), first + count - 1, body, 0)
        first += count
    values(n_steps - 2)
    softmax(n_steps - 1)
    values(n_steps - 1)


def _flash_finish(n_heads, sc, g_ref, o_ref):
    def body(iq, carry):
        heads = []
        for h in range(n_heads):
            acc = sc.acc[iq * n_heads + h]
            heads.append(acc[0:HEAD_LANES] * (1.0 / acc[HEAD_LANES:HEAD_LANES + 1]))
        o_t = jnp.concatenate(heads, axis=0)
        o_ref[0, _tile_rows(iq), :] = _rms(o_t.T, g_ref[...]).astype(o_ref.dtype)
        return carry

    lax.fori_loop(0, o_ref.shape[1] // TQ, body, 0, unroll=SETUP_UNROLL)


def _causal_segments(n_tiles):
    return [(n_tiles * (n_tiles - 1) // 2, lambda i, j, h: None), (n_tiles, lambda i, j, h: _causal_bias_t())]


def _fox_kernel(tab_ref, q_ref, eq_ref, k_ref, ek_ref, v_ref, g_ref, o_ref, *scratch):
    sc = _FlashScratch(*scratch)
    n_tiles = q_ref.shape[1] // TQ
    _flash_begin(v_ref, sc)

    def build_rhs(iq, carry):
        rows = _tile_rows(iq)
        q_t = q_ref[0, rows, :].astype(jnp.float32).T
        eq_t = eq_ref[0, rows, :].astype(jnp.float32).T
        slot_head = lax.broadcasted_iota(jnp.int32, eq_t.shape, 0) // FGATE_SLOT
        for h in range(FOX_HEADS):
            pair, e = divmod(h, 2)
            q_h = _head_rows(q_t[pair * V7X_LANES:(pair + 1) * V7X_LANES], e)
            e_h = jnp.where(slot_head == h, eq_t, 0.0)
            sc.rhs[iq * FOX_HEADS + h] = jnp.concatenate([q_h, e_h], axis=0).astype(jnp.bfloat16)
        return carry

    lax.fori_loop(0, n_tiles, build_rhs, 0, unroll=SETUP_UNROLL)

    def lhs_tile(j, pair):
        rows = _tile_rows(j)
        return jnp.concatenate([k_ref[0, rows, pair * V7X_LANES:(pair + 1) * V7X_LANES], ek_ref[0, rows, :]], axis=1)

    _flash_pipeline(tab_ref, _causal_segments(n_tiles), range(FOX_HEADS), FOX_HEADS, lhs_tile, sc)
    _flash_finish(FOX_HEADS, sc, g_ref, o_ref)


def _moba_kernel(tab_ref, q_ref, k_ref, v_ref, kmean_ref, bias_ref, g_ref, o_ref, *scratch):
    sc = _FlashScratch(*scratch)
    f32, bf16 = jnp.float32, jnp.bfloat16
    n_tiles = q_ref.shape[1] // TQ
    n_blocks = kmean_ref.shape[1]
    _flash_begin(v_ref, sc)
    sel_rows = 16
    blk = lax.broadcasted_iota(jnp.int32, (sel_rows, TQ), 0)
    lane = lax.broadcasted_iota(jnp.int32, (1, V7X_LANES), 1)

    def build_rhs(iq, carry):
        q_t = q_ref[0, _tile_rows(iq), :].astype(f32).T
        for h in range(MOBA_HEADS):
            pair, e = divmod(h, 2)
            q_h = _head_rows(q_t[pair * V7X_LANES:(pair + 1) * V7X_LANES], e)
            q_hb = q_h.astype(bf16)
            kmean = kmean_ref[0, :, pair * V7X_LANES:(pair + 1) * V7X_LANES]
            km_hi = kmean.astype(bf16).astype(f32)
            gate_lhs = jnp.concatenate([jnp.concatenate([km_hi, kmean - km_hi], axis=1),
                                        jnp.zeros((sel_rows - n_blocks, 2 * V7X_LANES), f32)], axis=0).astype(bf16)
            gate = _dot(gate_lhs, jnp.concatenate([q_hb, q_hb], axis=0))
            beaten = jnp.zeros((sel_rows, TQ), f32)
            for mblk in range(n_blocks - 1):
                gm = gate[mblk:mblk + 1, :]
                wins = (gm > gate) | ((gm == gate) & (mblk < blk))
                beaten = beaten + jnp.where(wins, jnp.where(mblk < iq, 1.0, 0.0), 0.0)
            keep = ((blk < iq) & (beaten < MOBA_TOPK)) | (blk == iq)
            sel = jnp.where(keep, 0.0, MASK_NEG)[0:n_blocks]
            far = lax.broadcasted_iota(jnp.int32, (n_blocks, TQ), 0) <= iq - 2
            c = bias_ref[h, 2, 0:1, :]
            c_hi = c.astype(bf16).astype(f32)
            extras = [sel, jnp.where(far, c_hi, 0.0), jnp.where(far, c - c_hi, 0.0)]
            pad = jnp.zeros((V7X_LANES - len(extras) * n_blocks, TQ), f32)
            sc.rhs[iq * MOBA_HEADS + h] = jnp.concatenate([q_h] + extras + [pad], axis=0).astype(bf16)
        return carry

    lax.fori_loop(0, n_tiles, build_rhs, 0, unroll=SETUP_UNROLL)

    def lhs_tile(j, pair):
        hit = (lane % n_blocks == j) & (lane < 3 * n_blocks)
        onehot = jnp.broadcast_to(jnp.where(hit, 1.0, 0.0).astype(bf16), (TQ, V7X_LANES))
        return jnp.concatenate([k_ref[0, _tile_rows(j), pair * V7X_LANES:(pair + 1) * V7X_LANES], onehot], axis=1)

    n_far = (n_tiles - 1) * (n_tiles - 2) // 2
    segments = [(n_far, lambda i, j, h: None), (n_tiles - 1, lambda i, j, h: bias_ref[h, 1]),
                (n_tiles, lambda i, j, h: bias_ref[h, 0])]
    _flash_pipeline(tab_ref, segments, range(MOBA_HEADS), MOBA_HEADS, lhs_tile, sc)
    _flash_finish(MOBA_HEADS, sc, g_ref, o_ref)


def _mla_kernel(tab_ref, qn_ref, qr_ref, kn_ref, kr_ref, v_ref, g_ref, o_ref, *scratch):
    sc = _FlashScratch(*scratch)
    n_tiles = qn_ref.shape[1] // TQ
    _flash_begin(v_ref, sc)
    heads_per_rot = V7X_LANES // MLA_ROPE_DIM
    rot_slot = lax.broadcasted_iota(jnp.int32, (V7X_LANES, TQ), 0) // MLA_ROPE_DIM

    def build_rhs(iq, carry):
        rows = _tile_rows(iq)
        qn_t = qn_ref[0, rows, :].astype(jnp.float32).T
        qr_t = qr_ref[0, rows, :].astype(jnp.float32).T
        for h in range(MLA_HEADS):
            pair, e = divmod(h, 2)
            quad, slot = divmod(h, heads_per_rot)
            q_h = _head_rows(qn_t[pair * V7X_LANES:(pair + 1) * V7X_LANES], e)
            r_h = jnp.where(rot_slot == slot, qr_t[quad * V7X_LANES:(quad + 1) * V7X_LANES], 0.0)
            sc.rhs[iq * MLA_HEADS + h] = jnp.concatenate([q_h, r_h], axis=0).astype(jnp.bfloat16)
        return carry

    lax.fori_loop(0, n_tiles, build_rhs, 0, unroll=SETUP_UNROLL)

    def lhs_tile(j, pair):
        rows = _tile_rows(j)
        return jnp.concatenate([kn_ref[0, rows, pair * V7X_LANES:(pair + 1) * V7X_LANES], kr_ref[0, rows, :]], axis=1)

    for h0 in range(0, MLA_HEADS, PIPE_HEADS):
        _flash_pipeline(tab_ref, _causal_segments(n_tiles), range(h0, h0 + PIPE_HEADS), MLA_HEADS, lhs_tile, sc)
    _flash_finish(MLA_HEADS, sc, g_ref, o_ref)


def _tile_pairs(n_tiles, split_previous):
    near = 2 if split_previous else 1
    pairs = [(i, j) for i in range(n_tiles) for j in range(i - near + 1)]
    for d in range(near - 1, -1, -1):
        pairs += [(i, i - d) for i in range(d, n_tiles)]
    return jnp.asarray(np.array(pairs + [(1, 1)], np.int32).T)


def _attn_call(kernel, name, n_heads, split_previous, arrays, const_arrays):
    batch, seq, _ = arrays[0].shape
    n_tiles = seq // TQ
    out_width = n_heads * HEAD_LANES
    row = lambda a: pl.BlockSpec((1,) + a.shape[1:], lambda b: (b, 0, 0))
    return pl.pallas_call(
        kernel,
        out_shape=jax.ShapeDtypeStruct((batch, seq, out_width), jnp.bfloat16),
        grid=(batch,),
        in_specs=([pl.BlockSpec(memory_space=pltpu.SMEM)] + [row(a) for a in arrays]
                  + [_const_spec(a.shape) for a in const_arrays]),
        out_specs=pl.BlockSpec((1, seq, out_width), lambda b: (b, 0, 0)),
        scratch_shapes=[
            pltpu.VMEM((n_tiles, n_heads * V_ROWS, TQ), jnp.bfloat16),
            pltpu.VMEM((n_tiles * n_heads, 2 * V7X_LANES, TQ), jnp.bfloat16),
            pltpu.VMEM((PIPE_HEADS, TQ, TQ), jnp.float32),
            pltpu.VMEM((PIPE_HEADS, TQ, TQ), jnp.bfloat16),
            pltpu.VMEM((PIPE_HEADS, 1, TQ), jnp.float32),
            pltpu.VMEM((n_tiles * n_heads, 1, TQ), jnp.float32),
            pltpu.VMEM((n_tiles * n_heads, V_ROWS, TQ), jnp.float32),
        ],
        compiler_params=_params("arbitrary"),
        name=name,
    )(_tile_pairs(n_tiles, split_previous), *arrays, *const_arrays)


def _bias_tile_kernel(table_ref, bucket_ref, o_ref):
    h = pl.program_id(0)
    for k in range(bucket_ref.shape[0]):
        bkt = bucket_ref[k]
        tile = jnp.full(bkt.shape, MASK_NEG, jnp.float32)
        for b in range(T5_BUCKETS):
            tile = jnp.where(bkt == b, table_ref[h, b] * LOG2E, tile)
        o_ref[0, k] = tile


def _moba_bias_tiles(t5_table):
    bucket = _t5_bucket_table(3 * TQ)
    key = np.arange(TQ)[:, None]
    qry = np.arange(TQ)[None, :]
    kinds = []
    for k in range(3):
        dist = qry - key + k * TQ
        kinds.append(np.where(dist >= 0, bucket[np.maximum(dist, 0)], -1))
    buckets = jnp.asarray(np.stack(kinds), jnp.int32)
    return pl.pallas_call(
        _bias_tile_kernel,
        out_shape=jax.ShapeDtypeStruct((MOBA_HEADS, 3, TQ, TQ), jnp.float32),
        grid=(MOBA_HEADS,),
        in_specs=[pl.BlockSpec(memory_space=pltpu.SMEM), _const_spec(buckets.shape)],
        out_specs=pl.BlockSpec((1, 3, TQ, TQ), lambda h: (h, 0, 0, 0)),
        compiler_params=_params("arbitrary"),
        name="t5_bias_tiles",
    )(t5_table.T, buckets)


def _mix_ffn_kernel(x_ref, of_ref, om_ref, oc_ref, mod_ref, gmix_ref, gpre_ref, gpost_ref,
                    wout_ref, wgu_ref, wd_ref, o_ref):
    bf16 = jnp.bfloat16
    gate_a = mod_ref[0, :, 2 * D_MODEL:3 * D_MODEL]
    shift = mod_ref[0, :, 3 * D_MODEL:4 * D_MODEL]
    scale = mod_ref[0, :, 4 * D_MODEL:5 * D_MODEL]
    gate_f = mod_ref[0, :, 5 * D_MODEL:6 * D_MODEL]
    half = TM // 2

    def mix(r):
        rows = slice(r * half, (r + 1) * half)
        o = jnp.concatenate([of_ref[0, rows, :], om_ref[0, rows, :], oc_ref[0, rows, :]], axis=1)
        x = x_ref[0, rows, :] + gate_a * _rms(_dot(o, wout_ref[...]), gmix_ref[...])
        return x, (_rms(x, gpre_ref[...]) * (1.0 + scale) + shift).astype(bf16)

    def chunk(h, c0, c1):
        g = _dot(h, wgu_ref[:, c0:c1])
        u = _dot(h, wgu_ref[:, D_FF + c0:D_FF + c1])
        return _dot((g * jax.nn.sigmoid(g) * u).astype(bf16), wd_ref[c0:c1, :])

    def finish(r, x, acc):
        o_ref[0, r * half:(r + 1) * half, :] = x + gate_f * _rms(acc, gpost_ref[...])

    x0, h0 = mix(0)
    acc0 = chunk(h0, *FFN_CHUNKS[0])
    x1, h1 = mix(1)
    acc0 = acc0 + chunk(h0, *FFN_CHUNKS[1])
    acc0 = acc0 + chunk(h0, *FFN_CHUNKS[2])
    acc1 = chunk(h1, *FFN_CHUNKS[0])
    finish(0, x0, acc0)
    acc1 = acc1 + chunk(h1, *FFN_CHUNKS[1])
    acc1 = acc1 + chunk(h1, *FFN_CHUNKS[2])
    finish(1, x1, acc1)


def _mix_ffn(x, o_f, o_m, o_c, mod_l, gmix, gpre, gpost, w_out, wgu, wd):
    batch, seq, d = x.shape

    def tok(width):
        return pl.BlockSpec((1, TM, width), lambda b, t: (b, t, 0))

    return pl.pallas_call(
        _mix_ffn_kernel,
        out_shape=jax.ShapeDtypeStruct(x.shape, x.dtype),
        grid=(batch, seq // TM),
        in_specs=[tok(d), tok(FOX_W), tok(MOBA_W), tok(MLA_W),
                  pl.BlockSpec((1, 1, 6 * d), lambda b, t: (b, 0, 0)),
                  _const_spec((1, d)), _const_spec((1, d)), _const_spec((1, d)),
                  _const_spec(w_out.shape), _const_spec(wgu.shape), _const_spec(wd.shape)],
        out_specs=tok(d),
        compiler_params=_params("arbitrary", "arbitrary"),
        name="mix_ffn",
    )(x, o_f, o_m, o_c, mod_l, gmix, gpre, gpost, w_out, wgu, wd)


def _rope_tables(seq):
    half = MLA_ROPE_DIM // 2
    inv_freq = 1.0 / (ROPE_THETA ** (jnp.arange(half, dtype=jnp.float32) / half))
    ang = jnp.arange(seq).astype(jnp.float32)[:, None] * inv_freq[None, :]
    reps = V7X_LANES // MLA_ROPE_DIM
    cos = jnp.tile(jnp.concatenate([jnp.cos(ang), jnp.cos(ang)], axis=1), (1, reps))
    sin = jnp.tile(jnp.concatenate([-jnp.sin(ang), jnp.sin(ang)], axis=1), (1, reps))
    return cos, sin


def kernel(x, c, t5_table, w_ada, b_ada, g_mix_pre, g_mix_post, w_in, b_forget, g_q_lat, w_uq, g_kv_lat, w_ukv,
           g_group, w_out, g_ffn_pre, g_ffn_post, w_gate_up, w_down):
    batch, seq, d = x.shape
    assert d == D_MODEL and seq % TM == 0 and TM % MOBA_BLOCK == 0 and TQ == MOBA_BLOCK
    bf16 = jnp.bfloat16
    in_cols, uq_cols, ukv_cols = _in_proj_columns(), _uq_columns(), _ukv_columns()
    cos_t, sin_t = _rope_tables(seq)
    tril = np.tril(np.ones((TM, TM), np.float32))
    tril = jnp.asarray(np.concatenate([tril] * N_SPLIT, axis=1), bf16)
    moba_bias = _moba_bias_tiles(t5_table)
    fg_lane = np.arange(V7X_LANES)
    fg_used = (fg_lane < FGATE_SLOT * FOX_HEADS) & (fg_lane % FGATE_SLOT < 2 * N_SPLIT)
    fg_head = np.minimum(fg_lane // FGATE_SLOT, FOX_HEADS - 1)

    mod = _ada_mod(c, w_ada, b_ada)
    for l in range(DEPTH):
        mod_l = mod[l].reshape(batch, 1, 6 * d)
        fbias = jnp.where(jnp.asarray(fg_used), b_forget[l][fg_head], 0.0).reshape(1, V7X_LANES)
        pr = _proj(x, mod_l, g_mix_pre[l].reshape(1, d),
                   _take_columns(w_in[l], in_cols).astype(bf16), fbias,
                   g_q_lat[l].reshape(1, -1), _take_columns(w_uq[l], uq_cols).astype(bf16),
                   g_kv_lat[l].reshape(1, -1), _take_columns(w_ukv[l], ukv_cols).astype(bf16),
                   cos_t, sin_t, tril)
        g_a = g_group[l, :FOX_W].reshape(1, -1)
        g_b = g_group[l, FOX_W:FOX_W + MOBA_W].reshape(1, -1)
        g_c = g_group[l, FOX_W + MOBA_W:].reshape(1, -1)
        o_f = _attn_call(_fox_kernel, "fox_attn", FOX_HEADS, False,
                         [pr["qf"], pr["eq"], pr["kf"], pr["ek"], pr["vf"]], [g_a])
        o_m = _attn_call(_moba_kernel, "moba_attn", MOBA_HEADS, True,
                         [pr["qm"], pr["km"], pr["vm"], pr["kmean"]], [moba_bias, g_b])
        o_c = _attn_call(_mla_kernel, "mla_attn", MLA_HEADS, False,
                         [pr["qn"], pr["qr"], pr["kn"], pr["kr"], pr["vc"]], [g_c])
        x = _mix_ffn(x, o_f, o_m, o_c, mod_l, g_mix_post[l].reshape(1, d), g_ffn_pre[l].reshape(1, d),
                     g_ffn_post[l].reshape(1, d), w_out[l].astype(bf16), w_gate_up[l].astype(bf16),
                     w_down[l].astype(bf16))
    return x
```

```python
import math
from typing import NamedTuple

import jax
import jax.numpy as jnp
import numpy as np
from jax import lax
from jax.experimental import pallas as pl
from jax.experimental.pallas import tpu as pltpu

D_MODEL = 1024
DEPTH = 2
FOX_HEADS = 4
FOX_HEAD_DIM = 64
MOBA_HEADS = 4
MOBA_HEAD_DIM = 64
MOBA_BLOCK = 256
MOBA_TOPK = 3
MLA_HEADS = 8
MLA_NOPE_DIM = 64
MLA_ROPE_DIM = 32
MLA_V_DIM = 64
MLA_Q_RANK = 256
MLA_KV_RANK = 128
ROPE_THETA = 10000.0
T5_BUCKETS = 32
T5_MAX_DISTANCE = 128
D_FF = -(-8 * D_MODEL // (3 * 256)) * 256
RMS_EPS = 1e-6
FOX_W = FOX_HEADS * FOX_HEAD_DIM
MOBA_W = MOBA_HEADS * MOBA_HEAD_DIM
MLA_W = MLA_HEADS * MLA_V_DIM
MIX_WIDTH = FOX_W + MOBA_W + MLA_W
IN_SIZES = (FOX_W, FOX_W, FOX_W, FOX_HEADS, MOBA_W, MOBA_W, MOBA_W, MLA_Q_RANK, MLA_KV_RANK, MLA_ROPE_DIM)

V7X_LANES = 128
V7X_VMEM_LIMIT_BYTES = 56 * 1024 * 1024

TM = 512
TQ = 256
PIPE_HEADS = 4
SETUP_UNROLL = 4
FFN_CHUNKS = ((0, 1024), (1024, 2048), (2048, D_FF))

HEAD_LANES = 64
MASK_NEG = -1e30
LOG2E = math.log2(math.e)
V_ROWS = 80
FGATE_SLOT = 8
N_SPLIT = 3

_C_QF, _C_KF, _C_VF = 0, 256, 512
_C_QM, _C_KM, _C_VM = 768, 1024, 1280
_C_CQ, _C_CKV = 1536, 1792
_C_FG, _C_KR, _C_KRS = 1920, 2048, 2176
IN_WIDTH_PADDED = 2304


def _reorder_in_proj(w_in):
    off = np.cumsum((0,) + IN_SIZES)
    depth, d, _ = w_in.shape
    f_g = w_in[..., off[3]:off[4]]
    k_r = w_in[..., off[9]:off[10]]
    half = MLA_ROPE_DIM // 2
    k_r_sw = jnp.concatenate([k_r[..., half:], k_r[..., :half]], axis=-1)
    fg = jnp.broadcast_to(f_g[..., None], f_g.shape + (2 * N_SPLIT,))
    fg = jnp.pad(fg, ((0, 0), (0, 0), (0, 0), (0, FGATE_SLOT - 2 * N_SPLIT))).reshape(depth, d, -1)
    fg = jnp.pad(fg, ((0, 0), (0, 0), (0, V7X_LANES - fg.shape[-1])))
    reps = V7X_LANES // MLA_ROPE_DIM
    out = jnp.concatenate([w_in[..., :off[3]], w_in[..., off[4]:off[9]], fg,
                           jnp.tile(k_r, (1, 1, reps)), jnp.tile(k_r_sw, (1, 1, reps))], axis=-1)
    assert out.shape[-1] == IN_WIDTH_PADDED
    return out


def _reorder_uq(w_uq):
    depth, r, _ = w_uq.shape
    half = MLA_ROPE_DIM // 2
    w = w_uq.reshape(depth, r, MLA_HEADS, MLA_NOPE_DIM + MLA_ROPE_DIM)
    rot = w[..., MLA_NOPE_DIM:]
    rot_sw = jnp.concatenate([rot[..., half:], rot[..., :half]], axis=-1)
    return jnp.concatenate([w[..., :MLA_NOPE_DIM].reshape(depth, r, -1), rot.reshape(depth, r, -1),
                            rot_sw.reshape(depth, r, -1)], axis=-1)


def _reorder_ukv(w_ukv):
    depth, r, _ = w_ukv.shape
    w = w_ukv.reshape(depth, r, MLA_HEADS, MLA_NOPE_DIM + MLA_V_DIM)
    return jnp.concatenate([w[..., :MLA_NOPE_DIM].reshape(depth, r, -1), w[..., MLA_NOPE_DIM:].reshape(depth, r, -1)],
                           axis=-1)


def _t5_bucket_table(n):
    d = np.arange(n, dtype=np.int32)
    max_exact = T5_BUCKETS // 2
    nf = np.maximum(d, max_exact).astype(np.float32)
    ratio = np.log(nf / np.float32(max_exact)) / np.float32(math.log(T5_MAX_DISTANCE / max_exact))
    large = max_exact + (ratio.astype(np.float32) * np.float32(T5_BUCKETS - max_exact)).astype(np.int32)
    large = np.minimum(large, T5_BUCKETS - 1)
    return np.where(d < max_exact, d, large).astype(np.int32)


def _const_spec(shape):
    nd = len(shape)
    return pl.BlockSpec(shape, lambda *_: (0,) * nd, pipeline_mode=pl.Buffered(1))


def _params(*sem):
    return pltpu.CompilerParams(dimension_semantics=sem, vmem_limit_bytes=V7X_VMEM_LIMIT_BYTES)


def _rms(x, g):
    return x * lax.rsqrt(jnp.mean(x * x, axis=-1, keepdims=True) + RMS_EPS) * g


def _split3(v):
    hi = v.astype(jnp.bfloat16)
    r1 = v - hi.astype(jnp.float32)
    mid = r1.astype(jnp.bfloat16)
    lo = (r1 - mid.astype(jnp.float32)).astype(jnp.bfloat16)
    return hi, mid, lo


def _dot(a, b):
    return jnp.dot(a, b, preferred_element_type=jnp.float32)


def _ada_kernel(c_ref, w_ref, b_ref, o_ref):
    c = c_ref[...]
    act = (c * jax.nn.sigmoid(c)).astype(jnp.bfloat16)
    o_ref[0] = _dot(act, w_ref[0].astype(jnp.bfloat16)) + b_ref[0]


def _ada_mod(c, w_ada, b_ada):
    depth, d, six_d = w_ada.shape
    batch = c.shape[0]
    n_col = six_d // d
    return pl.pallas_call(
        _ada_kernel,
        out_shape=jax.ShapeDtypeStruct((depth, batch, six_d), jnp.float32),
        grid=(depth, n_col),
        in_specs=[
            pl.BlockSpec((batch, d), lambda l, j: (0, 0)),
            pl.BlockSpec((1, d, d), lambda l, j: (l, 0, j)),
            pl.BlockSpec((1, 1, d), lambda l, j: (l, 0, j)),
        ],
        out_specs=pl.BlockSpec((1, batch, d), lambda l, j: (l, 0, j)),
        compiler_params=_params("arbitrary", "arbitrary"),
        name="ada_mod",
    )(c, w_ada, b_ada.reshape(depth, 1, six_d))


def _proj_kernel(x_ref, mod_ref, gpre_ref, win_ref, fb_ref, gq_ref, wuq_ref, gkv_ref, wukv_ref,
                 cos_ref, sin_ref, tril_ref,
                 qf_ref, eq_ref, kf_ref, ek_ref, vf_ref, qm_ref, km_ref, vm_ref, kmean_ref,
                 qn_ref, qr_ref, kn_ref, kr_ref, vc_ref, carry_ref):
    t = pl.program_id(1)
    bf16 = jnp.bfloat16
    x = x_ref[0]
    shift = mod_ref[0, :, 0:D_MODEL]
    scale = mod_ref[0, :, D_MODEL:2 * D_MODEL]
    h = (_rms(x, gpre_ref[...]) * (1.0 + scale) + shift).astype(bf16)

    def seg(c0, width):
        return _dot(h, win_ref[:, c0:c0 + width])

    qf_ref[0] = (seg(_C_QF, FOX_W) * (FOX_HEAD_DIM ** -0.5 * LOG2E)).astype(bf16)
    kf_ref[0] = seg(_C_KF, FOX_W).astype(bf16)
    vf_ref[0] = seg(_C_VF, FOX_W).astype(bf16)
    qm_ref[0] = (seg(_C_QM, MOBA_W) * (MOBA_HEAD_DIM ** -0.5 * LOG2E)).astype(bf16)
    km = seg(_C_KM, MOBA_W)
    km_ref[0] = km.astype(bf16)
    vm_ref[0] = seg(_C_VM, MOBA_W).astype(bf16)
    kmean_ref[0, 0] = jnp.mean(km.reshape(TM // MOBA_BLOCK, MOBA_BLOCK, MOBA_W), axis=1)

    lane = lax.broadcasted_iota(jnp.int32, (1, V7X_LANES), 1)
    slot = lane % FGATE_SLOT
    used = (lane < FGATE_SLOT * FOX_HEADS) & (slot < 2 * N_SPLIT)
    fl = seg(_C_FG, V7X_LANES) + fb_ref[...]
    logf = jnp.where(used, jnp.minimum(fl, 0.0) - jnp.log1p(jnp.exp(-jnp.abs(fl))), 0.0)

    @pl.when(t == 0)
    def _():
        carry_ref[...] = jnp.zeros_like(carry_ref)

    fcum = _dot(tril_ref[...], jnp.concatenate(_split3(logf), axis=0)) + carry_ref[0:1, :]
    carry_ref[0:1, :] = fcum[TM - 1:TM, :]
    hi, mid, lo = (p.astype(jnp.float32) for p in _split3(fcum * LOG2E))
    parts = jnp.where(slot % N_SPLIT == 0, hi, jnp.where(slot % N_SPLIT == 1, mid, lo))
    eq_ref[0] = jnp.where(used, jnp.where(slot < N_SPLIT, parts, 1.0), 0.0).astype(bf16)
    ek_ref[0] = jnp.where(used, jnp.where(slot < N_SPLIT, 1.0, -parts), 0.0).astype(bf16)

    cos = cos_ref[...]
    sin = sin_ref[...]
    mla_scale = (MLA_NOPE_DIM + MLA_ROPE_DIM) ** -0.5 * LOG2E
    cq = _rms(seg(_C_CQ, MLA_Q_RANK), gq_ref[...]).astype(bf16)
    n_nope = MLA_HEADS * MLA_NOPE_DIM
    n_rot = MLA_HEADS * MLA_ROPE_DIM
    qn_ref[0] = (_dot(cq, wuq_ref[:, 0:n_nope]) * mla_scale).astype(bf16)
    q_rot = _dot(cq, wuq_ref[:, n_nope:n_nope + n_rot])
    q_rot_sw = _dot(cq, wuq_ref[:, n_nope + n_rot:n_nope + 2 * n_rot])
    cos2 = jnp.concatenate([cos] * (n_rot // V7X_LANES), axis=1)
    sin2 = jnp.concatenate([sin] * (n_rot // V7X_LANES), axis=1)
    qr_ref[0] = ((q_rot * cos2 + q_rot_sw * sin2) * mla_scale).astype(bf16)
    ckv = _rms(seg(_C_CKV, MLA_KV_RANK), gkv_ref[...]).astype(bf16)
    kn_ref[0] = _dot(ckv, wukv_ref[:, 0:n_nope]).astype(bf16)
    vc_ref[0] = _dot(ckv, wukv_ref[:, n_nope:n_nope + MLA_W]).astype(bf16)
    kr_ref[0] = (seg(_C_KR, V7X_LANES) * cos + seg(_C_KRS, V7X_LANES) * sin).astype(bf16)


def _proj(x, mod_l, gpre, win, fbias, gq, wuq, gkv, wukv, cos_t, sin_t, tril):
    batch, seq, d = x.shape
    nt = seq // TM
    bf16 = jnp.bfloat16

    def tok(width):
        return pl.BlockSpec((1, TM, width), lambda b, t: (b, t, 0))

    def out(width):
        return jax.ShapeDtypeStruct((batch, seq, width), bf16)

    widths = dict(qf=FOX_W, eq=V7X_LANES, kf=FOX_W, ek=V7X_LANES, vf=FOX_W, qm=MOBA_W, km=MOBA_W, vm=MOBA_W)
    mla_widths = dict(qn=MLA_HEADS * MLA_NOPE_DIM, qr=MLA_HEADS * MLA_ROPE_DIM, kn=MLA_HEADS * MLA_NOPE_DIM,
                      kr=V7X_LANES, vc=MLA_W)
    nb = TM // MOBA_BLOCK
    out_shape = ([out(w) for w in widths.values()]
                 + [jax.ShapeDtypeStruct((batch, nt, nb, MOBA_W), jnp.float32)]
                 + [out(w) for w in mla_widths.values()])
    out_specs = ([tok(w) for w in widths.values()]
                 + [pl.BlockSpec((1, 1, nb, MOBA_W), lambda b, t: (b, t, 0, 0))]
                 + [tok(w) for w in mla_widths.values()])
    res = pl.pallas_call(
        _proj_kernel,
        out_shape=out_shape,
        grid=(batch, nt),
        in_specs=[
            tok(d),
            pl.BlockSpec((1, 1, 6 * d), lambda b, t: (b, 0, 0)),
            _const_spec((1, d)),
            _const_spec(win.shape),
            _const_spec((1, V7X_LANES)),
            _const_spec((1, MLA_Q_RANK)),
            _const_spec(wuq.shape),
            _const_spec((1, MLA_KV_RANK)),
            _const_spec(wukv.shape),
            pl.BlockSpec((TM, V7X_LANES), lambda b, t: (t, 0)),
            pl.BlockSpec((TM, V7X_LANES), lambda b, t: (t, 0)),
            _const_spec(tril.shape),
        ],
        out_specs=out_specs,
        scratch_shapes=[pltpu.VMEM((8, V7X_LANES), jnp.float32)],
        compiler_params=_params("arbitrary", "arbitrary"),
        name="in_proj",
    )(x, mod_l, gpre, win, fbias, gq, wuq, gkv, wukv, cos_t, sin_t, tril)
    names = list(widths) + ["kmean"] + list(mla_widths)
    r = dict(zip(names, res))
    r["kmean"] = r["kmean"].reshape(batch, seq // MOBA_BLOCK, MOBA_W)
    return r


def _transpose_to_bf16(x):
    return x.astype(jnp.float32).T.astype(jnp.bfloat16)


def _head_rows(x_pair, e):
    row = lax.broadcasted_iota(jnp.int32, x_pair.shape, 0)
    keep = (row >= HEAD_LANES) if e else (row < HEAD_LANES)
    return jnp.where(keep, x_pair, 0.0)


def _causal_bias_t():
    key = lax.broadcasted_iota(jnp.int32, (TQ, TQ), 0)
    qry = lax.broadcasted_iota(jnp.int32, (TQ, TQ), 1)
    return jnp.where(key <= qry, 0.0, MASK_NEG)


def _tile_rows(t):
    return pl.ds(pl.multiple_of(t * TQ, TQ), TQ)


class _FlashScratch(NamedTuple):
    vt: object
    rhs: object
    s: object
    p: object
    alpha: object
    m: object
    acc: object


def _flash_begin(v_ref, sc):
    n_heads = sc.vt.shape[1] // V_ROWS
    extra = lax.broadcasted_iota(jnp.int32, (V_ROWS - HEAD_LANES, TQ), 0)
    ones_row = jnp.where(extra == 0, 1.0, 0.0).astype(jnp.bfloat16)
    for c in range(sc.vt.shape[0]):
        v_t = _transpose_to_bf16(v_ref[0, c * TQ:(c + 1) * TQ, :])
        for h in range(n_heads):
            sc.vt[c, h * V_ROWS:h * V_ROWS + HEAD_LANES, :] = v_t[h * HEAD_LANES:(h + 1) * HEAD_LANES]
            sc.vt[c, h * V_ROWS + HEAD_LANES:(h + 1) * V_ROWS, :] = ones_row
    sc.m[...] = jnp.full(sc.m.shape, MASK_NEG, jnp.float32)
    sc.acc[...] = jnp.zeros(sc.acc.shape, jnp.float32)


def _flash_pipeline(tab_ref, segments, heads, n_heads, lhs_tile, sc):
    slot = {h: h - heads[0] for h in heads}
    n_steps = sum(count for count, _ in segments)

    def ij(t):
        t = jnp.clip(t, 0, n_steps - 1)
        return tab_ref[0, t], tab_ref[1, t]

    def scores(t, bias_fn):
        i, j = ij(t)
        lhs = {h // 2: lhs_tile(j, h // 2) for h in heads if h % 2 == 0}
        out = {}
        for h in heads:
            s = _dot(lhs[h // 2], sc.rhs[i * n_heads + h])
            b = bias_fn(i, j, h)
            out[h] = s if b is None else s + b
        return out

    def store_scores(vals):
        for h in heads:
            sc.s[slot[h]] = vals[h]

    def softmax(t):
        i, _ = ij(t)
        s_val = {h: sc.s[slot[h]] for h in heads}
        m_old = {h: sc.m[i * n_heads + h] for h in heads}
        m_new = {h: jnp.maximum(m_old[h], jnp.max(s_val[h], axis=0, keepdims=True)) for h in heads}
        alpha = {h: jnp.exp2(m_old[h] - m_new[h]) for h in heads}
        probs = {h: jnp.exp2(s_val[h] - m_new[h]) for h in heads}
        for h in heads:
            st = i * n_heads + h
            sc.m[st] = m_new[h]
            sc.alpha[slot[h]] = alpha[h]
            sc.p[slot[h]] = probs[h].astype(jnp.bfloat16)

    def values(t):
        i, j = ij(t)
        pv = {h: _dot(sc.vt[j, h * V_ROWS:(h + 1) * V_ROWS, :], sc.p[slot[h]]) for h in heads}
        for h in heads:
            st = i * n_heads + h
            sc.acc[st] = sc.alpha[slot[h]] * sc.acc[st] + pv[h]

    def prologue(_, carry):
        for h in heads:
            sc.alpha[slot[h]] = jnp.ones(sc.alpha.shape[1:], jnp.float32)
            sc.p[slot[h]] = jnp.zeros(sc.p.shape[1:], jnp.bfloat16)
        store_scores(scores(0, segments[0][1]))
        return carry

    lax.fori_loop(0, tab_ref[0, n_steps], prologue, 0)

    first = 0
    for count, bias_fn in segments:

        def body(t, carry, bias_fn=bias_fn):
            nxt = scores(t + 1, bias_fn)
            values(t - 1)
            softmax(t)
            store_scores(nxt)
            return carry

        lax.fori_loop(max(first - 1, 0), first + count - 1, body, 0)
        first += count
    values(n_steps - 2)
    softmax(n_steps - 1)
    values(n_steps - 1)


def _flash_finish(n_heads, sc, g_ref, o_ref):
    def body(iq, carry):
        heads = []
        for h in range(n_heads):
            acc = sc.acc[iq * n_heads + h]
            heads.append(acc[0:HEAD_LANES] * (1.0 / acc[HEAD_LANES:HEAD_LANES + 1]))
        o_t = jnp.concatenate(heads, axis=0)
        o_ref[0, _tile_rows(iq), :] = _rms(o_t.T, g_ref[...]).astype(o_ref.dtype)
        return carry

    lax.fori_loop(0, o_ref.shape[1] // TQ, body, 0, unroll=SETUP_UNROLL)


def _causal_segments(n_tiles):
    return [(n_tiles * (n_tiles - 1) // 2, lambda i, j, h: None), (n_tiles, lambda i, j, h: _causal_bias_t())]


def _fox_kernel(tab_ref, q_ref, eq_ref, k_ref, ek_ref, v_ref, g_ref, o_ref, *scratch):
    sc = _FlashScratch(*scratch)
    n_tiles = q_ref.shape[1] // TQ
    _flash_begin(v_ref, sc)

    def build_rhs(iq, carry):
        rows = _tile_rows(iq)
        q_t = q_ref[0, rows, :].astype(jnp.float32).T
        eq_t = eq_ref[0, rows, :].astype(jnp.float32).T
        slot_head = lax.broadcasted_iota(jnp.int32, eq_t.shape, 0) // FGATE_SLOT
        for h in range(FOX_HEADS):
            pair, e = divmod(h, 2)
            q_h = _head_rows(q_t[pair * V7X_LANES:(pair + 1) * V7X_LANES], e)
            e_h = jnp.where(slot_head == h, eq_t, 0.0)
            sc.rhs[iq * FOX_HEADS + h] = jnp.concatenate([q_h, e_h], axis=0).astype(jnp.bfloat16)
        return carry

    lax.fori_loop(0, n_tiles, build_rhs, 0, unroll=SETUP_UNROLL)

    def lhs_tile(j, pair):
        rows = _tile_rows(j)
        return jnp.concatenate([k_ref[0, rows, pair * V7X_LANES:(pair + 1) * V7X_LANES], ek_ref[0, rows, :]], axis=1)

    _flash_pipeline(tab_ref, _causal_segments(n_tiles), range(FOX_HEADS), FOX_HEADS, lhs_tile, sc)
    _flash_finish(FOX_HEADS, sc, g_ref, o_ref)


def _moba_kernel(tab_ref, q_ref, k_ref, v_ref, kmean_ref, bias_ref, g_ref, o_ref, *scratch):
    sc = _FlashScratch(*scratch)
    f32, bf16 = jnp.float32, jnp.bfloat16
    n_tiles = q_ref.shape[1] // TQ
    n_blocks = kmean_ref.shape[1]
    _flash_begin(v_ref, sc)
    sel_rows = 16
    blk = lax.broadcasted_iota(jnp.int32, (sel_rows, TQ), 0)
    lane = lax.broadcasted_iota(jnp.int32, (1, V7X_LANES), 1)

    def build_rhs(iq, carry):
        q_t = q_ref[0, _tile_rows(iq), :].astype(f32).T
        for h in range(MOBA_HEADS):
            pair, e = divmod(h, 2)
            q_h = _head_rows(q_t[pair * V7X_LANES:(pair + 1) * V7X_LANES], e)
            q_hb = q_h.astype(bf16)
            kmean = kmean_ref[0, :, pair * V7X_LANES:(pair + 1) * V7X_LANES]
            km_hi = kmean.astype(bf16).astype(f32)
            gate_lhs = jnp.concatenate([jnp.concatenate([km_hi, kmean - km_hi], axis=1),
                                        jnp.zeros((sel_rows - n_blocks, 2 * V7X_LANES), f32)], axis=0).astype(bf16)
            gate = _dot(gate_lhs, jnp.concatenate([q_hb, q_hb], axis=0))
            beaten = jnp.zeros((sel_rows, TQ), f32)
            for mblk in range(n_blocks - 1):
                gm = gate[mblk:mblk + 1, :]
                wins = (gm > gate) | ((gm == gate) & (mblk < blk))
                beaten = beaten + jnp.where(wins, jnp.where(mblk < iq, 1.0, 0.0), 0.0)
            keep = ((blk < iq) & (beaten < MOBA_TOPK)) | (blk == iq)
            sel = jnp.where(keep, 0.0, MASK_NEG)[0:n_blocks]
            far = lax.broadcasted_iota(jnp.int32, (n_blocks, TQ), 0) <= iq - 2
            c = bias_ref[h, 2, 0:1, :]
            c_hi = c.astype(bf16).astype(f32)
            extras = [sel, jnp.where(far, c_hi, 0.0), jnp.where(far, c - c_hi, 0.0)]
            pad = jnp.zeros((V7X_LANES - len(extras) * n_blocks, TQ), f32)
            sc.rhs[iq * MOBA_HEADS + h] = jnp.concatenate([q_h] + extras + [pad], axis=0).astype(bf16)
        return carry

    lax.fori_loop(0, n_tiles, build_rhs, 0, unroll=SETUP_UNROLL)

    def lhs_tile(j, pair):
        hit = (lane % n_blocks == j) & (lane < 3 * n_blocks)
        onehot = jnp.broadcast_to(jnp.where(hit, 1.0, 0.0).astype(bf16), (TQ, V7X_LANES))
        return jnp.concatenate([k_ref[0, _tile_rows(j), pair * V7X_LANES:(pair + 1) * V7X_LANES], onehot], axis=1)

    n_far = (n_tiles - 1) * (n_tiles - 2) // 2
    segments = [(n_far, lambda i, j, h: None), (n_tiles - 1, lambda i, j, h: bias_ref[h, 1]),
                (n_tiles, lambda i, j, h: bias_ref[h, 0])]
    _flash_pipeline(tab_ref, segments, range(MOBA_HEADS), MOBA_HEADS, lhs_tile, sc)
    _flash_finish(MOBA_HEADS, sc, g_ref, o_ref)


def _mla_kernel(tab_ref, qn_ref, qr_ref, kn_ref, kr_ref, v_ref, g_ref, o_ref, *scratch):
    sc = _FlashScratch(*scratch)
    n_tiles = qn_ref.shape[1] // TQ
    _flash_begin(v_ref, sc)
    heads_per_rot = V7X_LANES // MLA_ROPE_DIM
    rot_slot = lax.broadcasted_iota(jnp.int32, (V7X_LANES, TQ), 0) // MLA_ROPE_DIM

    def build_rhs(iq, carry):
        rows = _tile_rows(iq)
        qn_t = qn_ref[0, rows, :].astype(jnp.float32).T
        qr_t = qr_ref[0, rows, :].astype(jnp.float32).T
        for h in range(MLA_HEADS):
            pair, e = divmod(h, 2)
            quad, slot = divmod(h, heads_per_rot)
            q_h = _head_rows(qn_t[pair * V7X_LANES:(pair + 1) * V7X_LANES], e)
            r_h = jnp.where(rot_slot == slot, qr_t[quad * V7X_LANES:(quad + 1) * V7X_LANES], 0.0)
            sc.rhs[iq * MLA_HEADS + h] = jnp.concatenate([q_h, r_h], axis=0).astype(jnp.bfloat16)
        return carry

    lax.fori_loop(0, n_tiles, build_rhs, 0, unroll=SETUP_UNROLL)

    def lhs_tile(j, pair):
        rows = _tile_rows(j)
        return jnp.concatenate([kn_ref[0, rows, pair * V7X_LANES:(pair + 1) * V7X_LANES], kr_ref[0, rows, :]], axis=1)

    for h0 in range(0, MLA_HEADS, PIPE_HEADS):
        _flash_pipeline(tab_ref, _causal_segments(n_tiles), range(h0, h0 + PIPE_HEADS), MLA_HEADS, lhs_tile, sc)
    _flash_finish(MLA_HEADS, sc, g_ref, o_ref)


def _tile_pairs(n_tiles, split_previous):
    near = 2 if split_previous else 1
    pairs = [(i, j) for i in range(n_tiles) for j in range(i - near + 1)]
    for d in range(near - 1, -1, -1):
        pairs += [(i, i - d) for i in range(d, n_tiles)]
    return jnp.asarray(np.array(pairs + [(1, 1)], np.int32).T)


def _attn_call(kernel, name, n_heads, split_previous, arrays, const_arrays):
    batch, seq, _ = arrays[0].shape
    n_tiles = seq // TQ
    out_width = n_heads * HEAD_LANES
    row = lambda a: pl.BlockSpec((1,) + a.shape[1:], lambda b: (b, 0, 0))
    return pl.pallas_call(
        kernel,
        out_shape=jax.ShapeDtypeStruct((batch, seq, out_width), jnp.bfloat16),
        grid=(batch,),
        in_specs=([pl.BlockSpec(memory_space=pltpu.SMEM)] + [row(a) for a in arrays]
                  + [_const_spec(a.shape) for a in const_arrays]),
        out_specs=pl.BlockSpec((1, seq, out_width), lambda b: (b, 0, 0)),
        scratch_shapes=[
            pltpu.VMEM((n_tiles, n_heads * V_ROWS, TQ), jnp.bfloat16),
            pltpu.VMEM((n_tiles * n_heads, 2 * V7X_LANES, TQ), jnp.bfloat16),
            pltpu.VMEM((PIPE_HEADS, TQ, TQ), jnp.float32),
            pltpu.VMEM((PIPE_HEADS, TQ, TQ), jnp.bfloat16),
            pltpu.VMEM((PIPE_HEADS, 1, TQ), jnp.float32),
            pltpu.VMEM((n_tiles * n_heads, 1, TQ), jnp.float32),
            pltpu.VMEM((n_tiles * n_heads, V_ROWS, TQ), jnp.float32),
        ],
        compiler_params=_params("arbitrary"),
        name=name,
    )(_tile_pairs(n_tiles, split_previous), *arrays, *const_arrays)


def _bias_tile_kernel(table_ref, bucket_ref, o_ref):
    h = pl.program_id(0)
    for k in range(bucket_ref.shape[0]):
        bkt = bucket_ref[k]
        tile = jnp.full(bkt.shape, MASK_NEG, jnp.float32)
        for b in range(T5_BUCKETS):
            tile = jnp.where(bkt == b, table_ref[h, b] * LOG2E, tile)
        o_ref[0, k] = tile


def _moba_bias_tiles(t5_table):
    bucket = _t5_bucket_table(3 * TQ)
    key = np.arange(TQ)[:, None]
    qry = np.arange(TQ)[None, :]
    kinds = []
    for k in range(3):
        dist = qry - key + k * TQ
        kinds.append(np.where(dist >= 0, bucket[np.maximum(dist, 0)], -1))
    buckets = jnp.asarray(np.stack(kinds), jnp.int32)
    return pl.pallas_call(
        _bias_tile_kernel,
        out_shape=jax.ShapeDtypeStruct((MOBA_HEADS, 3, TQ, TQ), jnp.float32),
        grid=(MOBA_HEADS,),
        in_specs=[pl.BlockSpec(memory_space=pltpu.SMEM), _const_spec(buckets.shape)],
        out_specs=pl.BlockSpec((1, 3, TQ, TQ), lambda h: (h, 0, 0, 0)),
        compiler_params=_params("arbitrary"),
        name="t5_bias_tiles",
    )(t5_table.T, buckets)


def _mix_ffn_kernel(x_ref, of_ref, om_ref, oc_ref, mod_ref, gmix_ref, gpre_ref, gpost_ref,
                    wout_ref, wgu_ref, wd_ref, o_ref):
    bf16 = jnp.bfloat16
    gate_a = mod_ref[0, :, 2 * D_MODEL:3 * D_MODEL]
    shift = mod_ref[0, :, 3 * D_MODEL:4 * D_MODEL]
    scale = mod_ref[0, :, 4 * D_MODEL:5 * D_MODEL]
    gate_f = mod_ref[0, :, 5 * D_MODEL:6 * D_MODEL]
    half = TM // 2

    def mix(r):
        rows = slice(r * half, (r + 1) * half)
        o = jnp.concatenate([of_ref[0, rows, :], om_ref[0, rows, :], oc_ref[0, rows, :]], axis=1)
        x = x_ref[0, rows, :] + gate_a * _rms(_dot(o, wout_ref[...]), gmix_ref[...])
        return x, (_rms(x, gpre_ref[...]) * (1.0 + scale) + shift).astype(bf16)

    def chunk(h, c0, c1):
        g = _dot(h, wgu_ref[:, c0:c1])
        u = _dot(h, wgu_ref[:, D_FF + c0:D_FF + c1])
        return _dot((g * jax.nn.sigmoid(g) * u).astype(bf16), wd_ref[c0:c1, :])

    def finish(r, x, acc):
        o_ref[0, r * half:(r + 1) * half, :] = x + gate_f * _rms(acc, gpost_ref[...])

    x0, h0 = mix(0)
    acc0 = chunk(h0, *FFN_CHUNKS[0])
    x1, h1 = mix(1)
    acc0 = acc0 + chunk(h0, *FFN_CHUNKS[1])
    acc0 = acc0 + chunk(h0, *FFN_CHUNKS[2])
    acc1 = chunk(h1, *FFN_CHUNKS[0])
    finish(0, x0, acc0)
    acc1 = acc1 + chunk(h1, *FFN_CHUNKS[1])
    acc1 = acc1 + chunk(h1, *FFN_CHUNKS[2])
    finish(1, x1, acc1)


def _mix_ffn(x, o_f, o_m, o_c, mod_l, gmix, gpre, gpost, w_out, wgu, wd):
    batch, seq, d = x.shape

    def tok(width):
        return pl.BlockSpec((1, TM, width), lambda b, t: (b, t, 0))

    return pl.pallas_call(
        _mix_ffn_kernel,
        out_shape=jax.ShapeDtypeStruct(x.shape, x.dtype),
        grid=(batch, seq // TM),
        in_specs=[tok(d), tok(FOX_W), tok(MOBA_W), tok(MLA_W),
                  pl.BlockSpec((1, 1, 6 * d), lambda b, t: (b, 0, 0)),
                  _const_spec((1, d)), _const_spec((1, d)), _const_spec((1, d)),
                  _const_spec(w_out.shape), _const_spec(wgu.shape), _const_spec(wd.shape)],
        out_specs=tok(d),
        compiler_params=_params("arbitrary", "arbitrary"),
        name="mix_ffn",
    )(x, o_f, o_m, o_c, mod_l, gmix, gpre, gpost, w_out, wgu, wd)


def _rope_tables(seq):
    half = MLA_ROPE_DIM // 2
    inv_freq = 1.0 / (ROPE_THETA ** (jnp.arange(half, dtype=jnp.float32) / half))
    ang = jnp.arange(seq).astype(jnp.float32)[:, None] * inv_freq[None, :]
    reps = V7X_LANES // MLA_ROPE_DIM
    cos = jnp.tile(jnp.concatenate([jnp.cos(ang), jnp.cos(ang)], axis=1), (1, reps))
    sin = jnp.tile(jnp.concatenate([-jnp.sin(ang), jnp.sin(ang)], axis=1), (1, reps))
    return cos, sin


def kernel(x, c, t5_table, w_ada, b_ada, g_mix_pre, g_mix_post, w_in, b_forget, g_q_lat, w_uq, g_kv_lat, w_ukv,
           g_group, w_out, g_ffn_pre, g_ffn_post, w_gate_up, w_down):
    batch, seq, d = x.shape
    assert d == D_MODEL and seq % TM == 0 and TM % MOBA_BLOCK == 0 and TQ == MOBA_BLOCK
    bf16 = jnp.bfloat16
    win, wuq, wukv = (f(w).astype(bf16) for f, w in ((_reorder_in_proj, w_in), (_reorder_uq, w_uq), (_reorder_ukv, w_ukv)))
    wout, wgu, wd = w_out.astype(bf16), w_gate_up.astype(bf16), w_down.astype(bf16)
    cos_t, sin_t = _rope_tables(seq)
    tril = np.tril(np.ones((TM, TM), np.float32))
    tril = jnp.asarray(np.concatenate([tril] * N_SPLIT, axis=1), bf16)
    moba_bias = _moba_bias_tiles(t5_table)
    fg_lane = np.arange(V7X_LANES)
    fg_used = (fg_lane < FGATE_SLOT * FOX_HEADS) & (fg_lane % FGATE_SLOT < 2 * N_SPLIT)
    fg_head = np.minimum(fg_lane // FGATE_SLOT, FOX_HEADS - 1)

    mod = _ada_mod(c, w_ada, b_ada)
    for l in range(DEPTH):
        mod_l = mod[l].reshape(batch, 1, 6 * d)
        fbias = jnp.where(jnp.asarray(fg_used), b_forget[l][fg_head], 0.0).reshape(1, V7X_LANES)
        pr = _proj(x, mod_l, g_mix_pre[l].reshape(1, d),
                   win[l], fbias, g_q_lat[l].reshape(1, -1), wuq[l], g_kv_lat[l].reshape(1, -1), wukv[l],
                   cos_t, sin_t, tril)
        g_a = g_group[l, :FOX_W].reshape(1, -1)
        g_b = g_group[l, FOX_W:FOX_W + MOBA_W].reshape(1, -1)
        g_c = g_group[l, FOX_W + MOBA_W:].reshape(1, -1)
        o_f = _attn_call(_fox_kernel, "fox_attn", FOX_HEADS, False,
                         [pr["qf"], pr["eq"], pr["kf"], pr["ek"], pr["vf"]], [g_a])
        o_m = _attn_call(_moba_kernel, "moba_attn", MOBA_HEADS, True,
                         [pr["qm"], pr["km"], pr["vm"], pr["kmean"]], [moba_bias, g_b])
        o_c = _attn_call(_mla_kernel, "mla_attn", MLA_HEADS, False,
                         [pr["qn"], pr["qr"], pr["kn"], pr["kr"], pr["vc"]], [g_c])
        x = _mix_ffn(x, o_f, o_m, o_c, mod_l, g_mix_post[l].reshape(1, d), g_ffn_pre[l].reshape(1, d),
                     g_ffn_post[l].reshape(1, d), wout[l], wgu[l], wd[l])
    return x
```

```python
import math
from typing import NamedTuple

import jax
import jax.numpy as jnp
import numpy as np
from jax import lax
from jax.experimental import pallas as pl
from jax.experimental.pallas import tpu as pltpu

D_MODEL = 1024
DEPTH = 2
FOX_HEADS = 4
FOX_HEAD_DIM = 64
MOBA_HEADS = 4
MOBA_HEAD_DIM = 64
MOBA_BLOCK = 256
MOBA_TOPK = 3
MLA_HEADS = 8
MLA_NOPE_DIM = 64
MLA_ROPE_DIM = 32
MLA_V_DIM = 64
MLA_Q_RANK = 256
MLA_KV_RANK = 128
ROPE_THETA = 10000.0
T5_BUCKETS = 32
T5_MAX_DISTANCE = 128
D_FF = -(-8 * D_MODEL // (3 * 256)) * 256
RMS_EPS = 1e-6
FOX_W = FOX_HEADS * FOX_HEAD_DIM
MOBA_W = MOBA_HEADS * MOBA_HEAD_DIM
MLA_W = MLA_HEADS * MLA_V_DIM
MIX_WIDTH = FOX_W + MOBA_W + MLA_W
IN_SIZES = (FOX_W, FOX_W, FOX_W, FOX_HEADS, MOBA_W, MOBA_W, MOBA_W, MLA_Q_RANK, MLA_KV_RANK, MLA_ROPE_DIM)

V7X_LANES = 128
V7X_BF16_SUBLANES = 16
V7X_VMEM_LIMIT_BYTES = 56 * 1024 * 1024

TM = 512
TQ = 256
PIPE_HEADS = 4
SETUP_UNROLL = 4
FFN_CHUNKS = ((0, 1024), (1024, 2048), (2048, D_FF))

HEAD_LANES = 64
MASK_NEG = -1e30
LOG2E = math.log2(math.e)
V_ROWS = HEAD_LANES + V7X_BF16_SUBLANES
SCORE_K = 2 * V7X_LANES
FGATE_SLOT = 8
N_SPLIT = 3

_IN_SEGMENTS = (FOX_W, FOX_W, FOX_W, MOBA_W, MOBA_W, MOBA_W, MLA_Q_RANK, MLA_KV_RANK, V7X_LANES, V7X_LANES, V7X_LANES)
(_C_QF, _C_KF, _C_VF, _C_QM, _C_KM, _C_VM, _C_CQ, _C_CKV, _C_FG, _C_KR, _C_KRS,
 IN_WIDTH_PADDED) = (int(c) for c in np.cumsum((0,) + _IN_SEGMENTS))


def _reorder_in_proj(w_in):
    off = np.cumsum((0,) + IN_SIZES)
    depth, d, _ = w_in.shape
    f_g = w_in[..., off[3]:off[4]]
    k_r = w_in[..., off[9]:off[10]]
    half = MLA_ROPE_DIM // 2
    k_r_sw = jnp.concatenate([k_r[..., half:], k_r[..., :half]], axis=-1)
    fg = jnp.broadcast_to(f_g[..., None], f_g.shape + (2 * N_SPLIT,))
    fg = jnp.pad(fg, ((0, 0), (0, 0), (0, 0), (0, FGATE_SLOT - 2 * N_SPLIT))).reshape(depth, d, -1)
    fg = jnp.pad(fg, ((0, 0), (0, 0), (0, V7X_LANES - fg.shape[-1])))
    reps = V7X_LANES // MLA_ROPE_DIM
    out = jnp.concatenate([w_in[..., :off[3]], w_in[..., off[4]:off[9]], fg,
                           jnp.tile(k_r, (1, 1, reps)), jnp.tile(k_r_sw, (1, 1, reps))], axis=-1)
    assert out.shape[-1] == IN_WIDTH_PADDED
    return out


def _reorder_uq(w_uq):
    depth, r, _ = w_uq.shape
    half = MLA_ROPE_DIM // 2
    w = w_uq.reshape(depth, r, MLA_HEADS, MLA_NOPE_DIM + MLA_ROPE_DIM)
    rot = w[..., MLA_NOPE_DIM:]
    rot_sw = jnp.concatenate([rot[..., half:], rot[..., :half]], axis=-1)
    return jnp.concatenate([w[..., :MLA_NOPE_DIM].reshape(depth, r, -1), rot.reshape(depth, r, -1),
                            rot_sw.reshape(depth, r, -1)], axis=-1)


def _reorder_ukv(w_ukv):
    depth, r, _ = w_ukv.shape
    w = w_ukv.reshape(depth, r, MLA_HEADS, MLA_NOPE_DIM + MLA_V_DIM)
    return jnp.concatenate([w[..., :MLA_NOPE_DIM].reshape(depth, r, -1), w[..., MLA_NOPE_DIM:].reshape(depth, r, -1)],
                           axis=-1)


def _t5_bucket_table(n):
    d = np.arange(n, dtype=np.int32)
    max_exact = T5_BUCKETS // 2
    nf = np.maximum(d, max_exact).astype(np.float32)
    ratio = np.log(nf / np.float32(max_exact)) / np.float32(math.log(T5_MAX_DISTANCE / max_exact))
    large = max_exact + (ratio.astype(np.float32) * np.float32(T5_BUCKETS - max_exact)).astype(np.int32)
    large = np.minimum(large, T5_BUCKETS - 1)
    return np.where(d < max_exact, d, large).astype(np.int32)


def _const_spec(shape):
    nd = len(shape)
    return pl.BlockSpec(shape, lambda *_: (0,) * nd, pipeline_mode=pl.Buffered(1))


def _params(*sem):
    return pltpu.CompilerParams(dimension_semantics=sem, vmem_limit_bytes=V7X_VMEM_LIMIT_BYTES)


def _rms(x, g):
    return x * lax.rsqrt(jnp.mean(x * x, axis=-1, keepdims=True) + RMS_EPS) * g


def _split3(v):
    hi = v.astype(jnp.bfloat16)
    r1 = v - hi.astype(jnp.float32)
    mid = r1.astype(jnp.bfloat16)
    lo = (r1 - mid.astype(jnp.float32)).astype(jnp.bfloat16)
    return hi, mid, lo


def _dot(a, b):
    return jnp.dot(a, b, preferred_element_type=jnp.float32)


def _ada_kernel(c_ref, w_ref, b_ref, o_ref):
    c = c_ref[...]
    act = (c * jax.nn.sigmoid(c)).astype(jnp.bfloat16)
    o_ref[0] = _dot(act, w_ref[0].astype(jnp.bfloat16)) + b_ref[0]


def _ada_mod(c, w_ada, b_ada):
    depth, d, six_d = w_ada.shape
    batch = c.shape[0]
    n_col = six_d // d
    return pl.pallas_call(
        _ada_kernel,
        out_shape=jax.ShapeDtypeStruct((depth, batch, six_d), jnp.float32),
        grid=(depth, n_col),
        in_specs=[
            pl.BlockSpec((batch, d), lambda l, j: (0, 0)),
            pl.BlockSpec((1, d, d), lambda l, j: (l, 0, j)),
            pl.BlockSpec((1, 1, d), lambda l, j: (l, 0, j)),
        ],
        out_specs=pl.BlockSpec((1, batch, d), lambda l, j: (l, 0, j)),
        compiler_params=_params("arbitrary", "arbitrary"),
        name="ada_mod",
    )(c, w_ada, b_ada.reshape(depth, 1, six_d))


def _proj_kernel(x_ref, mod_ref, gpre_ref, win_ref, fb_ref, gq_ref, wuq_ref, gkv_ref, wukv_ref,
                 cos_ref, sin_ref, tril_ref,
                 qf_ref, eq_ref, kf_ref, ek_ref, vf_ref, qm_ref, km_ref, vm_ref, kmean_ref,
                 qn_ref, qr_ref, kn_ref, kr_ref, vc_ref, carry_ref):
    t = pl.program_id(1)
    bf16 = jnp.bfloat16
    x = x_ref[0]
    shift = mod_ref[0, :, 0:D_MODEL]
    scale = mod_ref[0, :, D_MODEL:2 * D_MODEL]
    h = (_rms(x, gpre_ref[...]) * (1.0 + scale) + shift).astype(bf16)

    def seg(c0, width):
        return _dot(h, win_ref[:, c0:c0 + width])

    qf_ref[0] = (seg(_C_QF, FOX_W) * (FOX_HEAD_DIM ** -0.5 * LOG2E)).astype(bf16)
    kf_ref[0] = seg(_C_KF, FOX_W).astype(bf16)
    vf_ref[0] = seg(_C_VF, FOX_W).astype(bf16)
    qm_ref[0] = (seg(_C_QM, MOBA_W) * (MOBA_HEAD_DIM ** -0.5 * LOG2E)).astype(bf16)
    km = seg(_C_KM, MOBA_W)
    km_ref[0] = km.astype(bf16)
    vm_ref[0] = seg(_C_VM, MOBA_W).astype(bf16)
    kmean_ref[0, 0] = jnp.mean(km.reshape(TM // MOBA_BLOCK, MOBA_BLOCK, MOBA_W), axis=1)

    lane = lax.broadcasted_iota(jnp.int32, (1, V7X_LANES), 1)
    slot = lane % FGATE_SLOT
    used = (lane < FGATE_SLOT * FOX_HEADS) & (slot < 2 * N_SPLIT)
    fl = seg(_C_FG, V7X_LANES) + fb_ref[...]
    logf = jnp.where(used, jnp.minimum(fl, 0.0) - jnp.log1p(jnp.exp(-jnp.abs(fl))), 0.0)

    @pl.when(t == 0)
    def _():
        carry_ref[...] = jnp.zeros_like(carry_ref)

    fcum = _dot(tril_ref[...], jnp.concatenate(_split3(logf), axis=0)) + carry_ref[0:1, :]
    carry_ref[0:1, :] = fcum[TM - 1:TM, :]
    hi, mid, lo = (p.astype(jnp.float32) for p in _split3(fcum * LOG2E))
    parts = jnp.where(slot % N_SPLIT == 0, hi, jnp.where(slot % N_SPLIT == 1, mid, lo))
    eq_ref[0] = jnp.where(used, jnp.where(slot < N_SPLIT, parts, 1.0), 0.0).astype(bf16)
    ek_ref[0] = jnp.where(used, jnp.where(slot < N_SPLIT, 1.0, -parts), 0.0).astype(bf16)

    cos = cos_ref[...]
    sin = sin_ref[...]
    mla_scale = (MLA_NOPE_DIM + MLA_ROPE_DIM) ** -0.5 * LOG2E
    cq = _rms(seg(_C_CQ, MLA_Q_RANK), gq_ref[...]).astype(bf16)
    n_nope = MLA_HEADS * MLA_NOPE_DIM
    n_rot = MLA_HEADS * MLA_ROPE_DIM
    qn_ref[0] = (_dot(cq, wuq_ref[:, 0:n_nope]) * mla_scale).astype(bf16)
    q_rot = _dot(cq, wuq_ref[:, n_nope:n_nope + n_rot])
    q_rot_sw = _dot(cq, wuq_ref[:, n_nope + n_rot:n_nope + 2 * n_rot])
    cos2 = jnp.concatenate([cos] * (n_rot // V7X_LANES), axis=1)
    sin2 = jnp.concatenate([sin] * (n_rot // V7X_LANES), axis=1)
    qr_ref[0] = ((q_rot * cos2 + q_rot_sw * sin2) * mla_scale).astype(bf16)
    ckv = _rms(seg(_C_CKV, MLA_KV_RANK), gkv_ref[...]).astype(bf16)
    kn_ref[0] = _dot(ckv, wukv_ref[:, 0:n_nope]).astype(bf16)
    vc_ref[0] = _dot(ckv, wukv_ref[:, n_nope:n_nope + MLA_W]).astype(bf16)
    kr_ref[0] = (seg(_C_KR, V7X_LANES) * cos + seg(_C_KRS, V7X_LANES) * sin).astype(bf16)


def _proj(x, mod_l, gpre, win, fbias, gq, wuq, gkv, wukv, cos_t, sin_t, tril):
    batch, seq, d = x.shape
    nt = seq // TM
    bf16 = jnp.bfloat16

    def tok(width):
        return pl.BlockSpec((1, TM, width), lambda b, t: (b, t, 0))

    def out(width):
        return jax.ShapeDtypeStruct((batch, seq, width), bf16)

    widths = dict(qf=FOX_W, eq=V7X_LANES, kf=FOX_W, ek=V7X_LANES, vf=FOX_W, qm=MOBA_W, km=MOBA_W, vm=MOBA_W)
    mla_widths = dict(qn=MLA_HEADS * MLA_NOPE_DIM, qr=MLA_HEADS * MLA_ROPE_DIM, kn=MLA_HEADS * MLA_NOPE_DIM,
                      kr=V7X_LANES, vc=MLA_W)
    nb = TM // MOBA_BLOCK
    out_shape = ([out(w) for w in widths.values()]
                 + [jax.ShapeDtypeStruct((batch, nt, nb, MOBA_W), jnp.float32)]
                 + [out(w) for w in mla_widths.values()])
    out_specs = ([tok(w) for w in widths.values()]
                 + [pl.BlockSpec((1, 1, nb, MOBA_W), lambda b, t: (b, t, 0, 0))]
                 + [tok(w) for w in mla_widths.values()])
    res = pl.pallas_call(
        _proj_kernel,
        out_shape=out_shape,
        grid=(batch, nt),
        in_specs=[
            tok(d),
            pl.BlockSpec((1, 1, 6 * d), lambda b, t: (b, 0, 0)),
            _const_spec((1, d)),
            _const_spec(win.shape),
            _const_spec((1, V7X_LANES)),
            _const_spec((1, MLA_Q_RANK)),
            _const_spec(wuq.shape),
            _const_spec((1, MLA_KV_RANK)),
            _const_spec(wukv.shape),
            pl.BlockSpec((TM, V7X_LANES), lambda b, t: (t, 0)),
            pl.BlockSpec((TM, V7X_LANES), lambda b, t: (t, 0)),
            _const_spec(tril.shape),
        ],
        out_specs=out_specs,
        scratch_shapes=[pltpu.VMEM((8, V7X_LANES), jnp.float32)],
        compiler_params=_params("arbitrary", "arbitrary"),
        name="in_proj",
    )(x, mod_l, gpre, win, fbias, gq, wuq, gkv, wukv, cos_t, sin_t, tril)
    names = list(widths) + ["kmean"] + list(mla_widths)
    r = dict(zip(names, res))
    r["kmean"] = r["kmean"].reshape(batch, seq // MOBA_BLOCK, MOBA_W)
    return r


def _transpose_to_bf16(x):
    return x.astype(jnp.float32).T.astype(jnp.bfloat16)


def _head_rows(x_pair, e):
    row = lax.broadcasted_iota(jnp.int32, x_pair.shape, 0)
    keep = (row >= HEAD_LANES) if e else (row < HEAD_LANES)
    return jnp.where(keep, x_pair, 0.0)


def _causal_bias_t():
    key = lax.broadcasted_iota(jnp.int32, (TQ, TQ), 0)
    qry = lax.broadcasted_iota(jnp.int32, (TQ, TQ), 1)
    return jnp.where(key <= qry, 0.0, MASK_NEG)


def _tile_rows(t):
    return pl.ds(pl.multiple_of(t * TQ, TQ), TQ)


class _FlashScratch(NamedTuple):
    vt: object
    rhs: object
    s: object
    p: object
    alpha: object
    m: object
    acc: object


def _flash_begin(v_ref, sc):
    n_heads = sc.vt.shape[1] // V_ROWS
    extra = lax.broadcasted_iota(jnp.int32, (V_ROWS - HEAD_LANES, TQ), 0)
    ones_row = jnp.where(extra == 0, 1.0, 0.0).astype(jnp.bfloat16)
    for c in range(sc.vt.shape[0]):
        v_t = _transpose_to_bf16(v_ref[0, c * TQ:(c + 1) * TQ, :])
        for h in range(n_heads):
            sc.vt[c, h * V_ROWS:h * V_ROWS + HEAD_LANES, :] = v_t[h * HEAD_LANES:(h + 1) * HEAD_LANES]
            sc.vt[c, h * V_ROWS + HEAD_LANES:(h + 1) * V_ROWS, :] = ones_row
    sc.m[...] = jnp.full(sc.m.shape, MASK_NEG, jnp.float32)
    sc.acc[...] = jnp.zeros(sc.acc.shape, jnp.float32)


def _flash_pipeline(tab_ref, segments, heads, n_heads, lhs_tile, sc):
    slot = {h: h - heads[0] for h in heads}
    n_steps = sum(count for count, _ in segments)

    def ij(t):
        t = jnp.clip(t, 0, n_steps - 1)
        return tab_ref[0, t], tab_ref[1, t]

    def scores(t, bias_fn):
        i, j = ij(t)
        lhs = {h // 2: lhs_tile(j, h // 2) for h in heads if h % 2 == 0}
        out = {}
        for h in heads:
            s = _dot(lhs[h // 2], sc.rhs[i * n_heads + h])
            b = bias_fn(i, j, h)
            out[h] = s if b is None else s + b
        return out

    def store_scores(vals):
        for h in heads:
            sc.s[slot[h]] = vals[h]

    def softmax(t):
        i, _ = ij(t)
        s_val = {h: sc.s[slot[h]] for h in heads}
        m_old = {h: sc.m[i * n_heads + h] for h in heads}
        m_new = {h: jnp.maximum(m_old[h], jnp.max(s_val[h], axis=0, keepdims=True)) for h in heads}
        alpha = {h: jnp.exp2(m_old[h] - m_new[h]) for h in heads}
        probs = {h: jnp.exp2(s_val[h] - m_new[h]) for h in heads}
        for h in heads:
            st = i * n_heads + h
            sc.m[st] = m_new[h]
            sc.alpha[slot[h]] = alpha[h]
            sc.p[slot[h]] = probs[h].astype(jnp.bfloat16)

    def values(t):
        i, j = ij(t)
        pv = {h: _dot(sc.vt[j, h * V_ROWS:(h + 1) * V_ROWS, :], sc.p[slot[h]]) for h in heads}
        for h in heads:
            st = i * n_heads + h
            sc.acc[st] = sc.alpha[slot[h]] * sc.acc[st] + pv[h]

    def prologue(_, carry):
        for h in heads:
            sc.alpha[slot[h]] = jnp.ones(sc.alpha.shape[1:], jnp.float32)
            sc.p[slot[h]] = jnp.zeros(sc.p.shape[1:], jnp.bfloat16)
        store_scores(scores(0, segments[0][1]))
        return carry

    lax.fori_loop(0, tab_ref[0, n_steps], prologue, 0)

    first = 0
    for count, bias_fn in segments:

        def body(t, carry, bias_fn=bias_fn):
            nxt = scores(t + 1, bias_fn)
            values(t - 1)
            softmax(t)
            store_scores(nxt)
            return carry

        lax.fori_loop(max(first - 1, 0), first + count - 1, body, 0)
        first += count
    values(n_steps - 2)
    softmax(n_steps - 1)
    values(n_steps - 1)


def _flash_finish(n_heads, sc, g_ref, o_ref):
    def body(iq, carry):
        heads = []
        for h in range(n_heads):
            acc = sc.acc[iq * n_heads + h]
            heads.append(acc[0:HEAD_LANES] * (1.0 / acc[HEAD_LANES:HEAD_LANES + 1]))
        o_t = jnp.concatenate(heads, axis=0)
        o_ref[0, _tile_rows(iq), :] = _rms(o_t.T, g_ref[...]).astype(o_ref.dtype)
        return carry

    lax.fori_loop(0, o_ref.shape[1] // TQ, body, 0, unroll=SETUP_UNROLL)


def _causal_segments(n_tiles):
    return [(n_tiles * (n_tiles - 1) // 2, lambda i, j, h: None), (n_tiles, lambda i, j, h: _causal_bias_t())]


def _fox_kernel(tab_ref, q_ref, eq_ref, k_ref, ek_ref, v_ref, g_ref, o_ref, *scratch):
    sc = _FlashScratch(*scratch)
    n_tiles = q_ref.shape[1] // TQ
    _flash_begin(v_ref, sc)

    def build_rhs(iq, carry):
        rows = _tile_rows(iq)
        q_t = q_ref[0, rows, :].astype(jnp.float32).T
        eq_t = eq_ref[0, rows, :].astype(jnp.float32).T
        slot_head = lax.broadcasted_iota(jnp.int32, eq_t.shape, 0) // FGATE_SLOT
        for h in range(FOX_HEADS):
            pair, e = divmod(h, 2)
            q_h = _head_rows(q_t[pair * V7X_LANES:(pair + 1) * V7X_LANES], e)
            e_h = jnp.where(slot_head == h, eq_t, 0.0)
            sc.rhs[iq * FOX_HEADS + h] = jnp.concatenate([q_h, e_h], axis=0).astype(jnp.bfloat16)
        return carry

    lax.fori_loop(0, n_tiles, build_rhs, 0, unroll=SETUP_UNROLL)

    def lhs_tile(j, pair):
        rows = _tile_rows(j)
        return jnp.concatenate([k_ref[0, rows, pair * V7X_LANES:(pair + 1) * V7X_LANES], ek_ref[0, rows, :]], axis=1)

    _flash_pipeline(tab_ref, _causal_segments(n_tiles), range(FOX_HEADS), FOX_HEADS, lhs_tile, sc)
    _flash_finish(FOX_HEADS, sc, g_ref, o_ref)


def _moba_kernel(tab_ref, q_ref, k_ref, v_ref, kmean_ref, bias_ref, g_ref, o_ref, *scratch):
    sc = _FlashScratch(*scratch)
    f32, bf16 = jnp.float32, jnp.bfloat16
    n_tiles = q_ref.shape[1] // TQ
    n_blocks = kmean_ref.shape[1]
    _flash_begin(v_ref, sc)
    sel_rows = V7X_BF16_SUBLANES
    blk = lax.broadcasted_iota(jnp.int32, (sel_rows, TQ), 0)
    lane = lax.broadcasted_iota(jnp.int32, (1, V7X_LANES), 1)

    def build_rhs(iq, carry):
        q_t = q_ref[0, _tile_rows(iq), :].astype(f32).T
        for h in range(MOBA_HEADS):
            pair, e = divmod(h, 2)
            q_h = _head_rows(q_t[pair * V7X_LANES:(pair + 1) * V7X_LANES], e)
            q_hb = q_h.astype(bf16)
            kmean = kmean_ref[0, :, pair * V7X_LANES:(pair + 1) * V7X_LANES]
            km_hi = kmean.astype(bf16).astype(f32)
            gate_lhs = jnp.concatenate([jnp.concatenate([km_hi, kmean - km_hi], axis=1),
                                        jnp.zeros((sel_rows - n_blocks, SCORE_K), f32)], axis=0).astype(bf16)
            gate = _dot(gate_lhs, jnp.concatenate([q_hb, q_hb], axis=0))
            beaten = jnp.zeros((sel_rows, TQ), f32)
            for mblk in range(n_blocks - 1):
                gm = gate[mblk:mblk + 1, :]
                wins = (gm > gate) | ((gm == gate) & (mblk < blk))
                beaten = beaten + jnp.where(wins, jnp.where(mblk < iq, 1.0, 0.0), 0.0)
            keep = ((blk < iq) & (beaten < MOBA_TOPK)) | (blk == iq)
            sel = jnp.where(keep, 0.0, MASK_NEG)[0:n_blocks]
            far = lax.broadcasted_iota(jnp.int32, (n_blocks, TQ), 0) <= iq - 2
            c = bias_ref[h, 2, 0:1, :]
            c_hi = c.astype(bf16).astype(f32)
            extras = [sel, jnp.where(far, c_hi, 0.0), jnp.where(far, c - c_hi, 0.0)]
            pad = jnp.zeros((V7X_LANES - len(extras) * n_blocks, TQ), f32)
            sc.rhs[iq * MOBA_HEADS + h] = jnp.concatenate([q_h] + extras + [pad], axis=0).astype(bf16)
        return carry

    lax.fori_loop(0, n_tiles, build_rhs, 0, unroll=SETUP_UNROLL)

    def lhs_tile(j, pair):
        hit = (lane % n_blocks == j) & (lane < 3 * n_blocks)
        onehot = jnp.broadcast_to(jnp.where(hit, 1.0, 0.0).astype(bf16), (TQ, V7X_LANES))
        return jnp.concatenate([k_ref[0, _tile_rows(j), pair * V7X_LANES:(pair + 1) * V7X_LANES], onehot], axis=1)

    n_far = (n_tiles - 1) * (n_tiles - 2) // 2
    segments = [(n_far, lambda i, j, h: None), (n_tiles - 1, lambda i, j, h: bias_ref[h, 1]),
                (n_tiles, lambda i, j, h: bias_ref[h, 0])]
    _flash_pipeline(tab_ref, segments, range(MOBA_HEADS), MOBA_HEADS, lhs_tile, sc)
    _flash_finish(MOBA_HEADS, sc, g_ref, o_ref)


def _mla_kernel(tab_ref, qn_ref, qr_ref, kn_ref, kr_ref, v_ref, g_ref, o_ref, *scratch):
    sc = _FlashScratch(*scratch)
    n_tiles = qn_ref.shape[1] // TQ
    _flash_begin(v_ref, sc)
    heads_per_rot = V7X_LANES // MLA_ROPE_DIM
    rot_slot = lax.broadcasted_iota(jnp.int32, (V7X_LANES, TQ), 0) // MLA_ROPE_DIM

    def build_rhs(iq, carry):
        rows = _tile_rows(iq)
        qn_t = qn_ref[0, rows, :].astype(jnp.float32).T
        qr_t = qr_ref[0, rows, :].astype(jnp.float32).T
        for h in range(MLA_HEADS):
            pair, e = divmod(h, 2)
            quad, slot = divmod(h, heads_per_rot)
            q_h = _head_rows(qn_t[pair * V7X_LANES:(pair + 1) * V7X_LANES], e)
            r_h = jnp.where(rot_slot == slot, qr_t[quad * V7X_LANES:(quad + 1) * V7X_LANES], 0.0)
            sc.rhs[iq * MLA_HEADS + h] = jnp.concatenate([q_h, r_h], axis=0).astype(jnp.bfloat16)
        return carry

    lax.fori_loop(0, n_tiles, build_rhs, 0, unroll=SETUP_UNROLL)

    def lhs_tile(j, pair):
        rows = _tile_rows(j)
        return jnp.concatenate([kn_ref[0, rows, pair * V7X_LANES:(pair + 1) * V7X_LANES], kr_ref[0, rows, :]], axis=1)

    for h0 in range(0, MLA_HEADS, PIPE_HEADS):
        _flash_pipeline(tab_ref, _causal_segments(n_tiles), range(h0, h0 + PIPE_HEADS), MLA_HEADS, lhs_tile, sc)
    _flash_finish(MLA_HEADS, sc, g_ref, o_ref)


def _tile_pairs(n_tiles, split_previous):
    near = 2 if split_previous else 1
    pairs = [(i, j) for i in range(n_tiles) for j in range(i - near + 1)]
    for d in range(near - 1, -1, -1):
        pairs += [(i, i - d) for i in range(d, n_tiles)]
    return jnp.asarray(np.array(pairs + [(1, 1)], np.int32).T)


def _attn_call(kernel, name, n_heads, split_previous, arrays, const_arrays):
    batch, seq, _ = arrays[0].shape
    n_tiles = seq // TQ
    out_width = n_heads * HEAD_LANES
    row = lambda a: pl.BlockSpec((1,) + a.shape[1:], lambda b: (b, 0, 0))
    return pl.pallas_call(
        kernel,
        out_shape=jax.ShapeDtypeStruct((batch, seq, out_width), jnp.bfloat16),
        grid=(batch,),
        in_specs=([pl.BlockSpec(memory_space=pltpu.SMEM)] + [row(a) for a in arrays]
                  + [_const_spec(a.shape) for a in const_arrays]),
        out_specs=pl.BlockSpec((1, seq, out_width), lambda b: (b, 0, 0)),
        scratch_shapes=[
            pltpu.VMEM((n_tiles, n_heads * V_ROWS, TQ), jnp.bfloat16),
            pltpu.VMEM((n_tiles * n_heads, SCORE_K, TQ), jnp.bfloat16),
            pltpu.VMEM((PIPE_HEADS, TQ, TQ), jnp.float32),
            pltpu.VMEM((PIPE_HEADS, TQ, TQ), jnp.bfloat16),
            pltpu.VMEM((PIPE_HEADS, 1, TQ), jnp.float32),
            pltpu.VMEM((n_tiles * n_heads, 1, TQ), jnp.float32),
            pltpu.VMEM((n_tiles * n_heads, V_ROWS, TQ), jnp.float32),
        ],
        compiler_params=_params("arbitrary"),
        name=name,
    )(_tile_pairs(n_tiles, split_previous), *arrays, *const_arrays)


def _bias_tile_kernel(table_ref, bucket_ref, o_ref):
    h = pl.program_id(0)
    for k in range(bucket_ref.shape[0]):
        bkt = bucket_ref[k]
        tile = jnp.full(bkt.shape, MASK_NEG, jnp.float32)
        for b in range(T5_BUCKETS):
            tile = jnp.where(bkt == b, table_ref[h, b] * LOG2E, tile)
        o_ref[0, k] = tile


def _moba_bias_tiles(t5_table):
    bucket = _t5_bucket_table(3 * TQ)
    key = np.arange(TQ)[:, None]
    qry = np.arange(TQ)[None, :]
    kinds = []
    for k in range(3):
        dist = qry - key + k * TQ
        kinds.append(np.where(dist >= 0, bucket[np.maximum(dist, 0)], -1))
    buckets = jnp.asarray(np.stack(kinds), jnp.int32)
    return pl.pallas_call(
        _bias_tile_kernel,
        out_shape=jax.ShapeDtypeStruct((MOBA_HEADS, 3, TQ, TQ), jnp.float32),
        grid=(MOBA_HEADS,),
        in_specs=[pl.BlockSpec(memory_space=pltpu.SMEM), _const_spec(buckets.shape)],
        out_specs=pl.BlockSpec((1, 3, TQ, TQ), lambda h: (h, 0, 0, 0)),
        compiler_params=_params("arbitrary"),
        name="t5_bias_tiles",
    )(t5_table.T, buckets)


def _mix_ffn_kernel(x_ref, of_ref, om_ref, oc_ref, mod_ref, gmix_ref, gpre_ref, gpost_ref,
                    wout_ref, wgu_ref, wd_ref, o_ref):
    bf16 = jnp.bfloat16
    gate_a = mod_ref[0, :, 2 * D_MODEL:3 * D_MODEL]
    shift = mod_ref[0, :, 3 * D_MODEL:4 * D_MODEL]
    scale = mod_ref[0, :, 4 * D_MODEL:5 * D_MODEL]
    gate_f = mod_ref[0, :, 5 * D_MODEL:6 * D_MODEL]
    half = TM // 2

    def mix(r):
        rows = slice(r * half, (r + 1) * half)
        o = jnp.concatenate([of_ref[0, rows, :], om_ref[0, rows, :], oc_ref[0, rows, :]], axis=1)
        x = x_ref[0, rows, :] + gate_a * _rms(_dot(o, wout_ref[...]), gmix_ref[...])
        return x, (_rms(x, gpre_ref[...]) * (1.0 + scale) + shift).astype(bf16)

    def chunk(h, c0, c1):
        g = _dot(h, wgu_ref[:, c0:c1])
        u = _dot(h, wgu_ref[:, D_FF + c0:D_FF + c1])
        return _dot((g * jax.nn.sigmoid(g) * u).astype(bf16), wd_ref[c0:c1, :])

    def finish(r, x, acc):
        o_ref[0, r * half:(r + 1) * half, :] = x + gate_f * _rms(acc, gpost_ref[...])

    x0, h0 = mix(0)
    acc0 = chunk(h0, *FFN_CHUNKS[0])
    x1, h1 = mix(1)
    acc0 = acc0 + chunk(h0, *FFN_CHUNKS[1])
    acc0 = acc0 + chunk(h0, *FFN_CHUNKS[2])
    acc1 = chunk(h1, *FFN_CHUNKS[0])
    finish(0, x0, acc0)
    acc1 = acc1 + chunk(h1, *FFN_CHUNKS[1])
    acc1 = acc1 + chunk(h1, *FFN_CHUNKS[2])
    finish(1, x1, acc1)


def _mix_ffn(x, o_f, o_m, o_c, mod_l, gmix, gpre, gpost, w_out, wgu, wd):
    batch, seq, d = x.shape

    def tok(width):
        return pl.BlockSpec((1, TM, width), lambda b, t: (b, t, 0))

    return pl.pallas_call(
        _mix_ffn_kernel,
        out_shape=jax.ShapeDtypeStruct(x.shape, x.dtype),
        grid=(batch, seq // TM),
        in_specs=[tok(d), tok(FOX_W), tok(MOBA_W), tok(MLA_W),
                  pl.BlockSpec((1, 1, 6 * d), lambda b, t: (b, 0, 0)),
                  _const_spec((1, d)), _const_spec((1, d)), _const_spec((1, d)),
                  _const_spec(w_out.shape), _const_spec(wgu.shape), _const_spec(wd.shape)],
        out_specs=tok(d),
        compiler_params=_params("arbitrary", "arbitrary"),
        name="mix_ffn",
    )(x, o_f, o_m, o_c, mod_l, gmix, gpre, gpost, w_out, wgu, wd)


def _rope_tables(seq):
    half = MLA_ROPE_DIM // 2
    inv_freq = 1.0 / (ROPE_THETA ** (jnp.arange(half, dtype=jnp.float32) / half))
    ang = jnp.arange(seq).astype(jnp.float32)[:, None] * inv_freq[None, :]
    reps = V7X_LANES // MLA_ROPE_DIM
    cos = jnp.tile(jnp.concatenate([jnp.cos(ang), jnp.cos(ang)], axis=1), (1, reps))
    sin = jnp.tile(jnp.concatenate([-jnp.sin(ang), jnp.sin(ang)], axis=1), (1, reps))
    return cos, sin


def kernel(x, c, t5_table, w_ada, b_ada, g_mix_pre, g_mix_post, w_in, b_forget, g_q_lat, w_uq, g_kv_lat, w_ukv,
           g_group, w_out, g_ffn_pre, g_ffn_post, w_gate_up, w_down):
    batch, seq, d = x.shape
    assert d == D_MODEL and seq % TM == 0 and TM % MOBA_BLOCK == 0 and TQ == MOBA_BLOCK
    bf16 = jnp.bfloat16
    win, wuq, wukv = (f(w).astype(bf16) for f, w in ((_reorder_in_proj, w_in), (_reorder_uq, w_uq), (_reorder_ukv, w_ukv)))
    wout, wgu, wd = w_out.astype(bf16), w_gate_up.astype(bf16), w_down.astype(bf16)
    cos_t, sin_t = _rope_tables(seq)
    tril = np.tril(np.ones((TM, TM), np.float32))
    tril = jnp.asarray(np.concatenate([tril] * N_SPLIT, axis=1), bf16)
    moba_bias = _moba_bias_tiles(t5_table)
    fg_lane = np.arange(V7X_LANES)
    fg_used = (fg_lane < FGATE_SLOT * FOX_HEADS) & (fg_lane % FGATE_SLOT < 2 * N_SPLIT)
    fg_head = np.minimum(fg_lane // FGATE_SLOT, FOX_HEADS - 1)

    mod = _ada_mod(c, w_ada, b_ada)
    for l in range(DEPTH):
        mod_l = mod[l].reshape(batch, 1, 6 * d)
        fbias = jnp.where(jnp.asarray(fg_used), b_forget[l][fg_head], 0.0).reshape(1, V7X_LANES)
        pr = _proj(x, mod_l, g_mix_pre[l].reshape(1, d),
                   win[l], fbias, g_q_lat[l].reshape(1, -1), wuq[l], g_kv_lat[l].reshape(1, -1), wukv[l],
                   cos_t, sin_t, tril)
        g_a = g_group[l, :FOX_W].reshape(1, -1)
        g_b = g_group[l, FOX_W:FOX_W + MOBA_W].reshape(1, -1)
        g_c = g_group[l, FOX_W + MOBA_W:].reshape(1, -1)
        o_f = _attn_call(_fox_kernel, "fox_attn", FOX_HEADS, False,
                         [pr["qf"], pr["eq"], pr["kf"], pr["ek"], pr["vf"]], [g_a])
        o_m = _attn_call(_moba_kernel, "moba_attn", MOBA_HEADS, True,
                         [pr["qm"], pr["km"], pr["vm"], pr["kmean"]], [moba_bias, g_b])
        o_c = _attn_call(_mla_kernel, "mla_attn", MLA_HEADS, False,
                         [pr["qn"], pr["qr"], pr["kn"], pr["kr"], pr["vc"]], [g_c])
        x = _mix_ffn(x, o_f, o_m, o_c, mod_l, g_mix_post[l].reshape(1, d), g_ffn_pre[l].reshape(1, d),
                     g_ffn_post[l].reshape(1, d), wout[l], wgu[l], wd[l])
    return x
```

```python
import math
from typing import NamedTuple

import jax
import jax.numpy as jnp
import numpy as np
from jax import lax
from jax.experimental import pallas as pl
from jax.experimental.pallas import tpu as pltpu

D_MODEL = 1024
DEPTH = 2
FOX_HEADS = 4
FOX_HEAD_DIM = 64
MOBA_HEADS = 4
MOBA_HEAD_DIM = 64
MOBA_BLOCK = 256
MOBA_TOPK = 3
MLA_HEADS = 8
MLA_NOPE_DIM = 64
MLA_ROPE_DIM = 32
MLA_V_DIM = 64
MLA_Q_RANK = 256
MLA_KV_RANK = 128
ROPE_THETA = 10000.0
T5_BUCKETS = 32
T5_MAX_DISTANCE = 128
D_FF = -(-8 * D_MODEL // (3 * 256)) * 256
RMS_EPS = 1e-6
FOX_W = FOX_HEADS * FOX_HEAD_DIM
MOBA_W = MOBA_HEADS * MOBA_HEAD_DIM
MLA_W = MLA_HEADS * MLA_V_DIM
MIX_WIDTH = FOX_W + MOBA_W + MLA_W
IN_SIZES = (FOX_W, FOX_W, FOX_W, FOX_HEADS, MOBA_W, MOBA_W, MOBA_W, MLA_Q_RANK, MLA_KV_RANK, MLA_ROPE_DIM)

V7X_LANES = 128
V7X_BF16_SUBLANES = 16
V7X_VMEM_LIMIT_BYTES = 56 * 1024 * 1024

TM = 512
TQ = 256
PIPE_HEADS = 4
SETUP_UNROLL = 4
FFN_CHUNKS = ((0, 1024), (1024, 2048), (2048, D_FF))

HEAD_LANES = 64
MASK_NEG = -1e30
LOG2E = math.log2(math.e)
V_ROWS = HEAD_LANES + V7X_BF16_SUBLANES
SCORE_K = 2 * V7X_LANES
FGATE_SLOT = 8
N_SPLIT = 3

_IN_SEGMENTS = (FOX_W, FOX_W, FOX_W, MOBA_W, MOBA_W, MOBA_W, MLA_Q_RANK, MLA_KV_RANK, V7X_LANES, V7X_LANES, V7X_LANES)
(_C_QF, _C_KF, _C_VF, _C_QM, _C_KM, _C_VM, _C_CQ, _C_CKV, _C_FG, _C_KR, _C_KRS,
 IN_WIDTH_PADDED) = (int(c) for c in np.cumsum((0,) + _IN_SEGMENTS))


def _reorder_in_proj(w_in):
    off = np.cumsum((0,) + IN_SIZES)
    depth, d, _ = w_in.shape
    f_g = w_in[..., off[3]:off[4]]
    k_r = w_in[..., off[9]:off[10]]
    half = MLA_ROPE_DIM // 2
    k_r_sw = jnp.concatenate([k_r[..., half:], k_r[..., :half]], axis=-1)
    fg = jnp.broadcast_to(f_g[..., None], f_g.shape + (2 * N_SPLIT,))
    fg = jnp.pad(fg, ((0, 0), (0, 0), (0, 0), (0, FGATE_SLOT - 2 * N_SPLIT))).reshape(depth, d, -1)
    fg = jnp.pad(fg, ((0, 0), (0, 0), (0, V7X_LANES - fg.shape[-1])))
    reps = V7X_LANES // MLA_ROPE_DIM
    out = jnp.concatenate([w_in[..., :off[3]], w_in[..., off[4]:off[9]], fg,
                           jnp.tile(k_r, (1, 1, reps)), jnp.tile(k_r_sw, (1, 1, reps))], axis=-1)
    assert out.shape[-1] == IN_WIDTH_PADDED
    return out


def _reorder_uq(w_uq):
    depth, r, _ = w_uq.shape
    half = MLA_ROPE_DIM // 2
    w = w_uq.reshape(depth, r, MLA_HEADS, MLA_NOPE_DIM + MLA_ROPE_DIM)
    rot = w[..., MLA_NOPE_DIM:]
    rot_sw = jnp.concatenate([rot[..., half:], rot[..., :half]], axis=-1)
    return jnp.concatenate([w[..., :MLA_NOPE_DIM].reshape(depth, r, -1), rot.reshape(depth, r, -1),
                            rot_sw.reshape(depth, r, -1)], axis=-1)


def _reorder_ukv(w_ukv):
    depth, r, _ = w_ukv.shape
    w = w_ukv.reshape(depth, r, MLA_HEADS, MLA_NOPE_DIM + MLA_V_DIM)
    return jnp.concatenate([w[..., :MLA_NOPE_DIM].reshape(depth, r, -1), w[..., MLA_NOPE_DIM:].reshape(depth, r, -1)],
                           axis=-1)


def _t5_bucket_table(n):
    d = np.arange(n, dtype=np.int32)
    max_exact = T5_BUCKETS // 2
    nf = np.maximum(d, max_exact).astype(np.float32)
    ratio = np.log(nf / np.float32(max_exact)) / np.float32(math.log(T5_MAX_DISTANCE / max_exact))
    large = max_exact + (ratio.astype(np.float32) * np.float32(T5_BUCKETS - max_exact)).astype(np.int32)
    large = np.minimum(large, T5_BUCKETS - 1)
    return np.where(d < max_exact, d, large).astype(np.int32)


def _const_spec(shape):
    nd = len(shape)
    return pl.BlockSpec(shape, lambda *_: (0,) * nd, pipeline_mode=pl.Buffered(1))


def _layer_spec(stacked, layer):
    nd = stacked.ndim - 1
    return pl.BlockSpec((None,) + stacked.shape[1:], lambda *_: (layer,) + (0,) * nd, pipeline_mode=pl.Buffered(1))


def _params(*sem):
    return pltpu.CompilerParams(dimension_semantics=sem, vmem_limit_bytes=V7X_VMEM_LIMIT_BYTES)


def _rms(x, g):
    return x * lax.rsqrt(jnp.mean(x * x, axis=-1, keepdims=True) + RMS_EPS) * g


def _split3(v):
    hi = v.astype(jnp.bfloat16)
    r1 = v - hi.astype(jnp.float32)
    mid = r1.astype(jnp.bfloat16)
    lo = (r1 - mid.astype(jnp.float32)).astype(jnp.bfloat16)
    return hi, mid, lo


def _dot(a, b):
    return jnp.dot(a, b, preferred_element_type=jnp.float32)


def _ada_kernel(c_ref, w_ref, b_ref, o_ref):
    c = c_ref[...]
    act = (c * jax.nn.sigmoid(c)).astype(jnp.bfloat16)
    o_ref[0] = _dot(act, w_ref[0].astype(jnp.bfloat16)) + b_ref[0]


def _ada_mod(c, w_ada, b_ada):
    depth, d, six_d = w_ada.shape
    batch = c.shape[0]
    n_col = six_d // d
    return pl.pallas_call(
        _ada_kernel,
        out_shape=jax.ShapeDtypeStruct((depth, batch, six_d), jnp.float32),
        grid=(depth, n_col),
        in_specs=[
            pl.BlockSpec((batch, d), lambda l, j: (0, 0)),
            pl.BlockSpec((1, d, d), lambda l, j: (l, 0, j)),
            pl.BlockSpec((1, 1, d), lambda l, j: (l, 0, j)),
        ],
        out_specs=pl.BlockSpec((1, batch, d), lambda l, j: (l, 0, j)),
        compiler_params=_params("arbitrary", "arbitrary"),
        name="ada_mod",
    )(c, w_ada, b_ada.reshape(depth, 1, six_d))


def _proj_kernel(x_ref, mod_ref, gpre_ref, win_ref, fb_ref, gq_ref, wuq_ref, gkv_ref, wukv_ref,
                 cos_ref, sin_ref, tril_ref,
                 qf_ref, eq_ref, kf_ref, ek_ref, vf_ref, qm_ref, km_ref, vm_ref, kmean_ref,
                 qn_ref, qr_ref, kn_ref, kr_ref, vc_ref, carry_ref):
    t = pl.program_id(1)
    bf16 = jnp.bfloat16
    x = x_ref[0]
    shift = mod_ref[0, :, 0:D_MODEL]
    scale = mod_ref[0, :, D_MODEL:2 * D_MODEL]
    h = (_rms(x, gpre_ref[...]) * (1.0 + scale) + shift).astype(bf16)

    def seg(c0, width):
        return _dot(h, win_ref[:, c0:c0 + width])

    qf_ref[0] = (seg(_C_QF, FOX_W) * (FOX_HEAD_DIM ** -0.5 * LOG2E)).astype(bf16)
    kf_ref[0] = seg(_C_KF, FOX_W).astype(bf16)
    vf_ref[0] = seg(_C_VF, FOX_W).astype(bf16)
    qm_ref[0] = (seg(_C_QM, MOBA_W) * (MOBA_HEAD_DIM ** -0.5 * LOG2E)).astype(bf16)
    km = seg(_C_KM, MOBA_W)
    km_ref[0] = km.astype(bf16)
    vm_ref[0] = seg(_C_VM, MOBA_W).astype(bf16)
    kmean_ref[0, 0] = jnp.mean(km.reshape(TM // MOBA_BLOCK, MOBA_BLOCK, MOBA_W), axis=1)

    lane = lax.broadcasted_iota(jnp.int32, (1, V7X_LANES), 1)
    slot = lane % FGATE_SLOT
    used = (lane < FGATE_SLOT * FOX_HEADS) & (slot < 2 * N_SPLIT)
    fl = seg(_C_FG, V7X_LANES) + fb_ref[...]
    logf = jnp.where(used, jnp.minimum(fl, 0.0) - jnp.log1p(jnp.exp(-jnp.abs(fl))), 0.0)

    @pl.when(t == 0)
    def _():
        carry_ref[...] = jnp.zeros_like(carry_ref)

    fcum = _dot(tril_ref[...], jnp.concatenate(_split3(logf), axis=0)) + carry_ref[0:1, :]
    carry_ref[0:1, :] = fcum[TM - 1:TM, :]
    hi, mid, lo = (p.astype(jnp.float32) for p in _split3(fcum * LOG2E))
    parts = jnp.where(slot % N_SPLIT == 0, hi, jnp.where(slot % N_SPLIT == 1, mid, lo))
    eq_ref[0] = jnp.where(used, jnp.where(slot < N_SPLIT, parts, 1.0), 0.0).astype(bf16)
    ek_ref[0] = jnp.where(used, jnp.where(slot < N_SPLIT, 1.0, -parts), 0.0).astype(bf16)

    cos = cos_ref[...]
    sin = sin_ref[...]
    mla_scale = (MLA_NOPE_DIM + MLA_ROPE_DIM) ** -0.5 * LOG2E
    cq = _rms(seg(_C_CQ, MLA_Q_RANK), gq_ref[...]).astype(bf16)
    n_nope = MLA_HEADS * MLA_NOPE_DIM
    n_rot = MLA_HEADS * MLA_ROPE_DIM
    qn_ref[0] = (_dot(cq, wuq_ref[:, 0:n_nope]) * mla_scale).astype(bf16)
    q_rot = _dot(cq, wuq_ref[:, n_nope:n_nope + n_rot])
    q_rot_sw = _dot(cq, wuq_ref[:, n_nope + n_rot:n_nope + 2 * n_rot])
    cos2 = jnp.concatenate([cos] * (n_rot // V7X_LANES), axis=1)
    sin2 = jnp.concatenate([sin] * (n_rot // V7X_LANES), axis=1)
    qr_ref[0] = ((q_rot * cos2 + q_rot_sw * sin2) * mla_scale).astype(bf16)
    ckv = _rms(seg(_C_CKV, MLA_KV_RANK), gkv_ref[...]).astype(bf16)
    kn_ref[0] = _dot(ckv, wukv_ref[:, 0:n_nope]).astype(bf16)
    vc_ref[0] = _dot(ckv, wukv_ref[:, n_nope:n_nope + MLA_W]).astype(bf16)
    kr_ref[0] = (seg(_C_KR, V7X_LANES) * cos + seg(_C_KRS, V7X_LANES) * sin).astype(bf16)


def _proj(x, layer, mod_l, gpre, win, fbias, gq, wuq, gkv, wukv, cos_t, sin_t, tril):
    batch, seq, d = x.shape
    nt = seq // TM
    bf16 = jnp.bfloat16

    def tok(width):
        return pl.BlockSpec((1, TM, width), lambda b, t: (b, t, 0))

    def out(width):
        return jax.ShapeDtypeStruct((batch, seq, width), bf16)

    widths = dict(qf=FOX_W, eq=V7X_LANES, kf=FOX_W, ek=V7X_LANES, vf=FOX_W, qm=MOBA_W, km=MOBA_W, vm=MOBA_W)
    mla_widths = dict(qn=MLA_HEADS * MLA_NOPE_DIM, qr=MLA_HEADS * MLA_ROPE_DIM, kn=MLA_HEADS * MLA_NOPE_DIM,
                      kr=V7X_LANES, vc=MLA_W)
    nb = TM // MOBA_BLOCK
    out_shape = ([out(w) for w in widths.values()]
                 + [jax.ShapeDtypeStruct((batch, nt, nb, MOBA_W), jnp.float32)]
                 + [out(w) for w in mla_widths.values()])
    out_specs = ([tok(w) for w in widths.values()]
                 + [pl.BlockSpec((1, 1, nb, MOBA_W), lambda b, t: (b, t, 0, 0))]
                 + [tok(w) for w in mla_widths.values()])
    res = pl.pallas_call(
        _proj_kernel,
        out_shape=out_shape,
        grid=(batch, nt),
        in_specs=[
            tok(d),
            pl.BlockSpec((1, 1, 6 * d), lambda b, t: (b, 0, 0)),
            _const_spec((1, d)),
            _layer_spec(win, layer),
            _const_spec((1, V7X_LANES)),
            _const_spec((1, MLA_Q_RANK)),
            _layer_spec(wuq, layer),
            _const_spec((1, MLA_KV_RANK)),
            _layer_spec(wukv, layer),
            pl.BlockSpec((TM, V7X_LANES), lambda b, t: (t, 0)),
            pl.BlockSpec((TM, V7X_LANES), lambda b, t: (t, 0)),
            _const_spec(tril.shape),
        ],
        out_specs=out_specs,
        scratch_shapes=[pltpu.VMEM((8, V7X_LANES), jnp.float32)],
        compiler_params=_params("arbitrary", "arbitrary"),
        name="in_proj",
    )(x, mod_l, gpre, win, fbias, gq, wuq, gkv, wukv, cos_t, sin_t, tril)
    names = list(widths) + ["kmean"] + list(mla_widths)
    r = dict(zip(names, res))
    r["kmean"] = r["kmean"].reshape(batch, seq // MOBA_BLOCK, MOBA_W)
    return r


def _transpose_to_bf16(x):
    return x.astype(jnp.float32).T.astype(jnp.bfloat16)


def _head_rows(x_pair, e):
    row = lax.broadcasted_iota(jnp.int32, x_pair.shape, 0)
    keep = (row >= HEAD_LANES) if e else (row < HEAD_LANES)
    return jnp.where(keep, x_pair, 0.0)


def _causal_bias_t():
    key = lax.broadcasted_iota(jnp.int32, (TQ, TQ), 0)
    qry = lax.broadcasted_iota(jnp.int32, (TQ, TQ), 1)
    return jnp.where(key <= qry, 0.0, MASK_NEG)


def _tile_rows(t):
    return pl.ds(pl.multiple_of(t * TQ, TQ), TQ)


class _FlashScratch(NamedTuple):
    vt: object
    rhs: object
    s: object
    p: object
    alpha: object
    m: object
    acc: object


def _flash_begin(v_ref, sc):
    n_heads = sc.vt.shape[1] // V_ROWS
    extra = lax.broadcasted_iota(jnp.int32, (V_ROWS - HEAD_LANES, TQ), 0)
    ones_row = jnp.where(extra == 0, 1.0, 0.0).astype(jnp.bfloat16)
    for c in range(sc.vt.shape[0]):
        v_t = _transpose_to_bf16(v_ref[0, c * TQ:(c + 1) * TQ, :])
        for h in range(n_heads):
            sc.vt[c, h * V_ROWS:h * V_ROWS + HEAD_LANES, :] = v_t[h * HEAD_LANES:(h + 1) * HEAD_LANES]
            sc.vt[c, h * V_ROWS + HEAD_LANES:(h + 1) * V_ROWS, :] = ones_row
    sc.m[...] = jnp.full(sc.m.shape, MASK_NEG, jnp.float32)
    sc.acc[...] = jnp.zeros(sc.acc.shape, jnp.float32)


def _flash_pipeline(tab_ref, segments, heads, n_heads, lhs_tile, sc):
    slot = {h: h - heads[0] for h in heads}
    n_steps = sum(count for count, _ in segments)

    def ij(t):
        t = jnp.clip(t, 0, n_steps - 1)
        return tab_ref[0, t], tab_ref[1, t]

    def scores(t, bias_fn):
        i, j = ij(t)
        lhs = {h // 2: lhs_tile(j, h // 2) for h in heads if h % 2 == 0}
        out = {}
        for h in heads:
            s = _dot(lhs[h // 2], sc.rhs[i * n_heads + h])
            b = bias_fn(i, j, h)
            out[h] = s if b is None else s + b
        return out

    def store_scores(vals):
        for h in heads:
            sc.s[slot[h]] = vals[h]

    def softmax(t):
        i, _ = ij(t)
        s_val = {h: sc.s[slot[h]] for h in heads}
        m_old = {h: sc.m[i * n_heads + h] for h in heads}
        m_new = {h: jnp.maximum(m_old[h], jnp.max(s_val[h], axis=0, keepdims=True)) for h in heads}
        alpha = {h: jnp.exp2(m_old[h] - m_new[h]) for h in heads}
        probs = {h: jnp.exp2(s_val[h] - m_new[h]) for h in heads}
        for h in heads:
            st = i * n_heads + h
            sc.m[st] = m_new[h]
            sc.alpha[slot[h]] = alpha[h]
            sc.p[slot[h]] = probs[h].astype(jnp.bfloat16)

    def values(t):
        i, j = ij(t)
        pv = {h: _dot(sc.vt[j, h * V_ROWS:(h + 1) * V_ROWS, :], sc.p[slot[h]]) for h in heads}
        for h in heads:
            st = i * n_heads + h
            sc.acc[st] = sc.alpha[slot[h]] * sc.acc[st] + pv[h]

    def prologue(_, carry):
        for h in heads:
            sc.alpha[slot[h]] = jnp.ones(sc.alpha.shape[1:], jnp.float32)
            sc.p[slot[h]] = jnp.zeros(sc.p.shape[1:], jnp.bfloat16)
        store_scores(scores(0, segments[0][1]))
        return carry

    lax.fori_loop(0, tab_ref[0, n_steps], prologue, 0)

    first = 0
    for count, bias_fn in segments:

        def body(t, carry, bias_fn=bias_fn):
            nxt = scores(t + 1, bias_fn)
            values(t - 1)
            softmax(t)
            store_scores(nxt)
            return carry

        lax.fori_loop(max(first - 1, 0), first + count - 1, body, 0)
        first += count
    values(n_steps - 2)
    softmax(n_steps - 1)
    values(n_steps - 1)


def _flash_finish(n_heads, sc, g_ref, o_ref):
    def body(iq, carry):
        heads = []
        for h in range(n_heads):
            acc = sc.acc[iq * n_heads + h]
            heads.append(acc[0:HEAD_LANES] * (1.0 / acc[HEAD_LANES:HEAD_LANES + 1]))
        o_t = jnp.concatenate(heads, axis=0)
        o_ref[0, _tile_rows(iq), :] = _rms(o_t.T, g_ref[...]).astype(o_ref.dtype)
        return carry

    lax.fori_loop(0, o_ref.shape[1] // TQ, body, 0, unroll=SETUP_UNROLL)


def _causal_segments(n_tiles):
    return [(n_tiles * (n_tiles - 1) // 2, lambda i, j, h: None), (n_tiles, lambda i, j, h: _causal_bias_t())]


def _fox_kernel(tab_ref, q_ref, eq_ref, k_ref, ek_ref, v_ref, g_ref, o_ref, *scratch):
    sc = _FlashScratch(*scratch)
    n_tiles = q_ref.shape[1] // TQ
    _flash_begin(v_ref, sc)

    def build_rhs(iq, carry):
        rows = _tile_rows(iq)
        q_t = q_ref[0, rows, :].astype(jnp.float32).T
        eq_t = eq_ref[0, rows, :].astype(jnp.float32).T
        slot_head = lax.broadcasted_iota(jnp.int32, eq_t.shape, 0) // FGATE_SLOT
        for h in range(FOX_HEADS):
            pair, e = divmod(h, 2)
            q_h = _head_rows(q_t[pair * V7X_LANES:(pair + 1) * V7X_LANES], e)
            e_h = jnp.where(slot_head == h, eq_t, 0.0)
            sc.rhs[iq * FOX_HEADS + h] = jnp.concatenate([q_h, e_h], axis=0).astype(jnp.bfloat16)
        return carry

    lax.fori_loop(0, n_tiles, build_rhs, 0, unroll=SETUP_UNROLL)

    def lhs_tile(j, pair):
        rows = _tile_rows(j)
        return jnp.concatenate([k_ref[0, rows, pair * V7X_LANES:(pair + 1) * V7X_LANES], ek_ref[0, rows, :]], axis=1)

    _flash_pipeline(tab_ref, _causal_segments(n_tiles), range(FOX_HEADS), FOX_HEADS, lhs_tile, sc)
    _flash_finish(FOX_HEADS, sc, g_ref, o_ref)


def _moba_kernel(tab_ref, q_ref, k_ref, v_ref, kmean_ref, bias_ref, g_ref, o_ref, *scratch):
    sc = _FlashScratch(*scratch)
    f32, bf16 = jnp.float32, jnp.bfloat16
    n_tiles = q_ref.shape[1] // TQ
    n_blocks = kmean_ref.shape[1]
    _flash_begin(v_ref, sc)
    sel_rows = V7X_BF16_SUBLANES
    blk = lax.broadcasted_iota(jnp.int32, (sel_rows, TQ), 0)
    lane = lax.broadcasted_iota(jnp.int32, (1, V7X_LANES), 1)

    def build_rhs(iq, carry):
        q_t = q_ref[0, _tile_rows(iq), :].astype(f32).T
        for h in range(MOBA_HEADS):
            pair, e = divmod(h, 2)
            q_h = _head_rows(q_t[pair * V7X_LANES:(pair + 1) * V7X_LANES], e)
            q_hb = q_h.astype(bf16)
            kmean = kmean_ref[0, :, pair * V7X_LANES:(pair + 1) * V7X_LANES]
            km_hi = kmean.astype(bf16).astype(f32)
            gate_lhs = jnp.concatenate([jnp.concatenate([km_hi, kmean - km_hi], axis=1),
                                        jnp.zeros((sel_rows - n_blocks, SCORE_K), f32)], axis=0).astype(bf16)
            gate = _dot(gate_lhs, jnp.concatenate([q_hb, q_hb], axis=0))
            beaten = jnp.zeros((sel_rows, TQ), f32)
            for mblk in range(n_blocks - 1):
                gm = gate[mblk:mblk + 1, :]
                wins = (gm > gate) | ((gm == gate) & (mblk < blk))
                beaten = beaten + jnp.where(wins, jnp.where(mblk < iq, 1.0, 0.0), 0.0)
            keep = ((blk < iq) & (beaten < MOBA_TOPK)) | (blk == iq)
            sel = jnp.where(keep, 0.0, MASK_NEG)[0:n_blocks]
            far = lax.broadcasted_iota(jnp.int32, (n_blocks, TQ), 0) <= iq - 2
            c = bias_ref[h, 2, 0:1, :]
            c_hi = c.astype(bf16).astype(f32)
            extras = [sel, jnp.where(far, c_hi, 0.0), jnp.where(far, c - c_hi, 0.0)]
            pad = jnp.zeros((V7X_LANES - len(extras) * n_blocks, TQ), f32)
            sc.rhs[iq * MOBA_HEADS + h] = jnp.concatenate([q_h] + extras + [pad], axis=0).astype(bf16)
        return carry

    lax.fori_loop(0, n_tiles, build_rhs, 0, unroll=SETUP_UNROLL)

    def lhs_tile(j, pair):
        hit = (lane % n_blocks == j) & (lane < 3 * n_blocks)
        onehot = jnp.broadcast_to(jnp.where(hit, 1.0, 0.0).astype(bf16), (TQ, V7X_LANES))
        return jnp.concatenate([k_ref[0, _tile_rows(j), pair * V7X_LANES:(pair + 1) * V7X_LANES], onehot], axis=1)

    n_far = (n_tiles - 1) * (n_tiles - 2) // 2
    segments = [(n_far, lambda i, j, h: None), (n_tiles - 1, lambda i, j, h: bias_ref[h, 1]),
                (n_tiles, lambda i, j, h: bias_ref[h, 0])]
    _flash_pipeline(tab_ref, segments, range(MOBA_HEADS), MOBA_HEADS, lhs_tile, sc)
    _flash_finish(MOBA_HEADS, sc, g_ref, o_ref)


def _mla_kernel(tab_ref, qn_ref, qr_ref, kn_ref, kr_ref, v_ref, g_ref, o_ref, *scratch):
    sc = _FlashScratch(*scratch)
    n_tiles = qn_ref.shape[1] // TQ
    _flash_begin(v_ref, sc)
    heads_per_rot = V7X_LANES // MLA_ROPE_DIM
    rot_slot = lax.broadcasted_iota(jnp.int32, (V7X_LANES, TQ), 0) // MLA_ROPE_DIM

    def build_rhs(iq, carry):
        rows = _tile_rows(iq)
        qn_t = qn_ref[0, rows, :].astype(jnp.float32).T
        qr_t = qr_ref[0, rows, :].astype(jnp.float32).T
        for h in range(MLA_HEADS):
            pair, e = divmod(h, 2)
            quad, slot = divmod(h, heads_per_rot)
            q_h = _head_rows(qn_t[pair * V7X_LANES:(pair + 1) * V7X_LANES], e)
            r_h = jnp.where(rot_slot == slot, qr_t[quad * V7X_LANES:(quad + 1) * V7X_LANES], 0.0)
            sc.rhs[iq * MLA_HEADS + h] = jnp.concatenate([q_h, r_h], axis=0).astype(jnp.bfloat16)
        return carry

    lax.fori_loop(0, n_tiles, build_rhs, 0, unroll=SETUP_UNROLL)

    def lhs_tile(j, pair):
        rows = _tile_rows(j)
        return jnp.concatenate([kn_ref[0, rows, pair * V7X_LANES:(pair + 1) * V7X_LANES], kr_ref[0, rows, :]], axis=1)

    for h0 in range(0, MLA_HEADS, PIPE_HEADS):
        _flash_pipeline(tab_ref, _causal_segments(n_tiles), range(h0, h0 + PIPE_HEADS), MLA_HEADS, lhs_tile, sc)
    _flash_finish(MLA_HEADS, sc, g_ref, o_ref)


def _tile_pairs(n_tiles, split_previous):
    near = 2 if split_previous else 1
    pairs = [(i, j) for i in range(n_tiles) for j in range(i - near + 1)]
    for d in range(near - 1, -1, -1):
        pairs += [(i, i - d) for i in range(d, n_tiles)]
    return jnp.asarray(np.array(pairs + [(1, 1)], np.int32).T)


def _attn_call(kernel, name, n_heads, split_previous, arrays, const_arrays):
    batch, seq, _ = arrays[0].shape
    n_tiles = seq // TQ
    out_width = n_heads * HEAD_LANES
    row = lambda a: pl.BlockSpec((1,) + a.shape[1:], lambda b: (b, 0, 0))
    return pl.pallas_call(
        kernel,
        out_shape=jax.ShapeDtypeStruct((batch, seq, out_width), jnp.bfloat16),
        grid=(batch,),
        in_specs=([pl.BlockSpec(memory_space=pltpu.SMEM)] + [row(a) for a in arrays]
                  + [_const_spec(a.shape) for a in const_arrays]),
        out_specs=pl.BlockSpec((1, seq, out_width), lambda b: (b, 0, 0)),
        scratch_shapes=[
            pltpu.VMEM((n_tiles, n_heads * V_ROWS, TQ), jnp.bfloat16),
            pltpu.VMEM((n_tiles * n_heads, SCORE_K, TQ), jnp.bfloat16),
            pltpu.VMEM((PIPE_HEADS, TQ, TQ), jnp.float32),
            pltpu.VMEM((PIPE_HEADS, TQ, TQ), jnp.bfloat16),
            pltpu.VMEM((PIPE_HEADS, 1, TQ), jnp.float32),
            pltpu.VMEM((n_tiles * n_heads, 1, TQ), jnp.float32),
            pltpu.VMEM((n_tiles * n_heads, V_ROWS, TQ), jnp.float32),
        ],
        compiler_params=_params("arbitrary"),
        name=name,
    )(_tile_pairs(n_tiles, split_previous), *arrays, *const_arrays)


def _bias_tile_kernel(table_ref, bucket_ref, o_ref):
    h = pl.program_id(0)
    for k in range(bucket_ref.shape[0]):
        bkt = bucket_ref[k]
        tile = jnp.full(bkt.shape, MASK_NEG, jnp.float32)
        for b in range(T5_BUCKETS):
            tile = jnp.where(bkt == b, table_ref[h, b] * LOG2E, tile)
        o_ref[0, k] = tile


def _moba_bias_tiles(t5_table):
    bucket = _t5_bucket_table(3 * TQ)
    key = np.arange(TQ)[:, None]
    qry = np.arange(TQ)[None, :]
    kinds = []
    for k in range(3):
        dist = qry - key + k * TQ
        kinds.append(np.where(dist >= 0, bucket[np.maximum(dist, 0)], -1))
    buckets = jnp.asarray(np.stack(kinds), jnp.int32)
    return pl.pallas_call(
        _bias_tile_kernel,
        out_shape=jax.ShapeDtypeStruct((MOBA_HEADS, 3, TQ, TQ), jnp.float32),
        grid=(MOBA_HEADS,),
        in_specs=[pl.BlockSpec(memory_space=pltpu.SMEM), _const_spec(buckets.shape)],
        out_specs=pl.BlockSpec((1, 3, TQ, TQ), lambda h: (h, 0, 0, 0)),
        compiler_params=_params("arbitrary"),
        name="t5_bias_tiles",
    )(t5_table.T, buckets)


def _mix_ffn_kernel(x_ref, of_ref, om_ref, oc_ref, mod_ref, gmix_ref, gpre_ref, gpost_ref,
                    wout_ref, wgu_ref, wd_ref, o_ref):
    bf16 = jnp.bfloat16
    gate_a = mod_ref[0, :, 2 * D_MODEL:3 * D_MODEL]
    shift = mod_ref[0, :, 3 * D_MODEL:4 * D_MODEL]
    scale = mod_ref[0, :, 4 * D_MODEL:5 * D_MODEL]
    gate_f = mod_ref[0, :, 5 * D_MODEL:6 * D_MODEL]
    half = TM // 2

    def mix(r):
        rows = slice(r * half, (r + 1) * half)
        o = jnp.concatenate([of_ref[0, rows, :], om_ref[0, rows, :], oc_ref[0, rows, :]], axis=1)
        x = x_ref[0, rows, :] + gate_a * _rms(_dot(o, wout_ref[...]), gmix_ref[...])
        return x, (_rms(x, gpre_ref[...]) * (1.0 + scale) + shift).astype(bf16)

    def chunk(h, c0, c1):
        g = _dot(h, wgu_ref[:, c0:c1])
        u = _dot(h, wgu_ref[:, D_FF + c0:D_FF + c1])
        return _dot((g * jax.nn.sigmoid(g) * u).astype(bf16), wd_ref[c0:c1, :])

    def finish(r, x, acc):
        o_ref[0, r * half:(r + 1) * half, :] = x + gate_f * _rms(acc, gpost_ref[...])

    x0, h0 = mix(0)
    acc0 = chunk(h0, *FFN_CHUNKS[0])
    x1, h1 = mix(1)
    acc0 = acc0 + chunk(h0, *FFN_CHUNKS[1])
    acc0 = acc0 + chunk(h0, *FFN_CHUNKS[2])
    acc1 = chunk(h1, *FFN_CHUNKS[0])
    finish(0, x0, acc0)
    acc1 = acc1 + chunk(h1, *FFN_CHUNKS[1])
    acc1 = acc1 + chunk(h1, *FFN_CHUNKS[2])
    finish(1, x1, acc1)


def _mix_ffn(x, layer, o_f, o_m, o_c, mod_l, gmix, gpre, gpost, w_out, wgu, wd):
    batch, seq, d = x.shape

    def tok(width):
        return pl.BlockSpec((1, TM, width), lambda b, t: (b, t, 0))

    return pl.pallas_call(
        _mix_ffn_kernel,
        out_shape=jax.ShapeDtypeStruct(x.shape, x.dtype),
        grid=(batch, seq // TM),
        in_specs=[tok(d), tok(FOX_W), tok(MOBA_W), tok(MLA_W),
                  pl.BlockSpec((1, 1, 6 * d), lambda b, t: (b, 0, 0)),
                  _const_spec((1, d)), _const_spec((1, d)), _const_spec((1, d)),
                  _layer_spec(w_out, layer), _layer_spec(wgu, layer), _layer_spec(wd, layer)],
        out_specs=tok(d),
        compiler_params=_params("arbitrary", "arbitrary"),
        name="mix_ffn",
    )(x, o_f, o_m, o_c, mod_l, gmix, gpre, gpost, w_out, wgu, wd)


def _rope_tables(seq):
    half = MLA_ROPE_DIM // 2
    inv_freq = 1.0 / (ROPE_THETA ** (jnp.arange(half, dtype=jnp.float32) / half))
    ang = jnp.arange(seq).astype(jnp.float32)[:, None] * inv_freq[None, :]
    reps = V7X_LANES // MLA_ROPE_DIM
    cos = jnp.tile(jnp.concatenate([jnp.cos(ang), jnp.cos(ang)], axis=1), (1, reps))
    sin = jnp.tile(jnp.concatenate([-jnp.sin(ang), jnp.sin(ang)], axis=1), (1, reps))
    return cos, sin


def kernel(x, c, t5_table, w_ada, b_ada, g_mix_pre, g_mix_post, w_in, b_forget, g_q_lat, w_uq, g_kv_lat, w_ukv,
           g_group, w_out, g_ffn_pre, g_ffn_post, w_gate_up, w_down):
    batch, seq, d = x.shape
    assert d == D_MODEL and seq % TM == 0 and TM % MOBA_BLOCK == 0 and TQ == MOBA_BLOCK
    bf16 = jnp.bfloat16
    win, wuq, wukv = (f(w).astype(bf16) for f, w in ((_reorder_in_proj, w_in), (_reorder_uq, w_uq), (_reorder_ukv, w_ukv)))
    wout, wgu, wd = w_out.astype(bf16), w_gate_up.astype(bf16), w_down.astype(bf16)
    cos_t, sin_t = _rope_tables(seq)
    tril = np.tril(np.ones((TM, TM), np.float32))
    tril = jnp.asarray(np.concatenate([tril] * N_SPLIT, axis=1), bf16)
    moba_bias = _moba_bias_tiles(t5_table)
    fg_lane = np.arange(V7X_LANES)
    fg_used = (fg_lane < FGATE_SLOT * FOX_HEADS) & (fg_lane % FGATE_SLOT < 2 * N_SPLIT)
    fg_head = np.minimum(fg_lane // FGATE_SLOT, FOX_HEADS - 1)

    mod = _ada_mod(c, w_ada, b_ada)
    for l in range(DEPTH):
        mod_l = mod[l].reshape(batch, 1, 6 * d)
        fbias = jnp.where(jnp.asarray(fg_used), b_forget[l][fg_head], 0.0).reshape(1, V7X_LANES)
        pr = _proj(x, l, mod_l, g_mix_pre[l].reshape(1, d),
                   win, fbias, g_q_lat[l].reshape(1, -1), wuq, g_kv_lat[l].reshape(1, -1), wukv,
                   cos_t, sin_t, tril)
        g_a = g_group[l, :FOX_W].reshape(1, -1)
        g_b = g_group[l, FOX_W:FOX_W + MOBA_W].reshape(1, -1)
        g_c = g_group[l, FOX_W + MOBA_W:].reshape(1, -1)
        o_f = _attn_call(_fox_kernel, "fox_attn", FOX_HEADS, False,
                         [pr["qf"], pr["eq"], pr["kf"], pr["ek"], pr["vf"]], [g_a])
        o_m = _attn_call(_moba_kernel, "moba_attn", MOBA_HEADS, True,
                         [pr["qm"], pr["km"], pr["vm"], pr["kmean"]], [moba_bias, g_b])
        o_c = _attn_call(_mla_kernel, "mla_attn", MLA_HEADS, False,
                         [pr["qn"], pr["qr"], pr["kn"], pr["kr"], pr["vc"]], [g_c])
        x = _mix_ffn(x, l, o_f, o_m, o_c, mod_l, g_mix_post[l].reshape(1, d), g_ffn_pre[l].reshape(1, d),
                     g_ffn_post[l].reshape(1, d), wout, wgu, wd)
    return x
```

```python
import math
from typing import NamedTuple

import jax
import jax.numpy as jnp
import numpy as np
from jax import lax
from jax.experimental import pallas as pl
from jax.experimental.pallas import tpu as pltpu

D_MODEL = 1024
DEPTH = 2
FOX_HEADS = 4
FOX_HEAD_DIM = 64
MOBA_HEADS = 4
MOBA_HEAD_DIM = 64
MOBA_BLOCK = 256
MOBA_TOPK = 3
MLA_HEADS = 8
MLA_NOPE_DIM = 64
MLA_ROPE_DIM = 32
MLA_V_DIM = 64
MLA_Q_RANK = 256
MLA_KV_RANK = 128
ROPE_THETA = 10000.0
T5_BUCKETS = 32
T5_MAX_DISTANCE = 128
D_FF = -(-8 * D_MODEL // (3 * 256)) * 256
RMS_EPS = 1e-6
FOX_W = FOX_HEADS * FOX_HEAD_DIM
MOBA_W = MOBA_HEADS * MOBA_HEAD_DIM
MLA_W = MLA_HEADS * MLA_V_DIM
MIX_WIDTH = FOX_W + MOBA_W + MLA_W
IN_SIZES = (FOX_W, FOX_W, FOX_W, FOX_HEADS, MOBA_W, MOBA_W, MOBA_W, MLA_Q_RANK, MLA_KV_RANK, MLA_ROPE_DIM)

V7X_LANES = 128
V7X_BF16_SUBLANES = 16
V7X_VMEM_LIMIT_BYTES = 56 * 1024 * 1024

TM = 512
TQ = 256
PIPE_HEADS = 4
SETUP_UNROLL = 4
FFN_CHUNKS = ((0, 1024), (1024, 2048), (2048, D_FF))

HEAD_LANES = 64
MASK_NEG = -1e30
LOG2E = math.log2(math.e)
V_ROWS = HEAD_LANES + V7X_BF16_SUBLANES
SCORE_K = 2 * V7X_LANES
FGATE_SLOT = 8
N_SPLIT = 3

_IN_SEGMENTS = (FOX_W, FOX_W, FOX_W, MOBA_W, MOBA_W, MOBA_W, MLA_Q_RANK, MLA_KV_RANK, V7X_LANES, V7X_LANES, V7X_LANES)
(_C_QF, _C_KF, _C_VF, _C_QM, _C_KM, _C_VM, _C_CQ, _C_CKV, _C_FG, _C_KR, _C_KRS,
 IN_WIDTH_PADDED) = (int(c) for c in np.cumsum((0,) + _IN_SEGMENTS))


def _reorder_in_proj(w_in):
    off = np.cumsum((0,) + IN_SIZES)
    depth, d, _ = w_in.shape
    f_g = w_in[..., off[3]:off[4]]
    k_r = w_in[..., off[9]:off[10]]
    half = MLA_ROPE_DIM // 2
    k_r_sw = jnp.concatenate([k_r[..., half:], k_r[..., :half]], axis=-1)
    fg = jnp.broadcast_to(f_g[..., None], f_g.shape + (2 * N_SPLIT,))
    fg = jnp.pad(fg, ((0, 0), (0, 0), (0, 0), (0, FGATE_SLOT - 2 * N_SPLIT))).reshape(depth, d, -1)
    fg = jnp.pad(fg, ((0, 0), (0, 0), (0, V7X_LANES - fg.shape[-1])))
    reps = V7X_LANES // MLA_ROPE_DIM
    out = jnp.concatenate([w_in[..., :off[3]], w_in[..., off[4]:off[9]], fg,
                           jnp.tile(k_r, (1, 1, reps)), jnp.tile(k_r_sw, (1, 1, reps))], axis=-1)
    assert out.shape[-1] == IN_WIDTH_PADDED
    return out


def _reorder_uq(w_uq):
    depth, r, _ = w_uq.shape
    half = MLA_ROPE_DIM // 2
    w = w_uq.reshape(depth, r, MLA_HEADS, MLA_NOPE_DIM + MLA_ROPE_DIM)
    rot = w[..., MLA_NOPE_DIM:]
    rot_sw = jnp.concatenate([rot[..., half:], rot[..., :half]], axis=-1)
    return jnp.concatenate([w[..., :MLA_NOPE_DIM].reshape(depth, r, -1), rot.reshape(depth, r, -1),
                            rot_sw.reshape(depth, r, -1)], axis=-1)


def _reorder_ukv(w_ukv):
    depth, r, _ = w_ukv.shape
    w = w_ukv.reshape(depth, r, MLA_HEADS, MLA_NOPE_DIM + MLA_V_DIM)
    return jnp.concatenate([w[..., :MLA_NOPE_DIM].reshape(depth, r, -1), w[..., MLA_NOPE_DIM:].reshape(depth, r, -1)],
                           axis=-1)


def _t5_bucket_table(n):
    d = np.arange(n, dtype=np.int32)
    max_exact = T5_BUCKETS // 2
    nf = np.maximum(d, max_exact).astype(np.float32)
    ratio = np.log(nf / np.float32(max_exact)) / np.float32(math.log(T5_MAX_DISTANCE / max_exact))
    large = max_exact + (ratio.astype(np.float32) * np.float32(T5_BUCKETS - max_exact)).astype(np.int32)
    large = np.minimum(large, T5_BUCKETS - 1)
    return np.where(d < max_exact, d, large).astype(np.int32)


def _const_spec(shape):
    nd = len(shape)
    return pl.BlockSpec(shape, lambda *_: (0,) * nd, pipeline_mode=pl.Buffered(1))


def _params(*sem):
    return pltpu.CompilerParams(dimension_semantics=sem, vmem_limit_bytes=V7X_VMEM_LIMIT_BYTES)


def _rms(x, g):
    return x * lax.rsqrt(jnp.mean(x * x, axis=-1, keepdims=True) + RMS_EPS) * g


def _split3(v):
    hi = v.astype(jnp.bfloat16)
    r1 = v - hi.astype(jnp.float32)
    mid = r1.astype(jnp.bfloat16)
    lo = (r1 - mid.astype(jnp.float32)).astype(jnp.bfloat16)
    return hi, mid, lo


def _dot(a, b):
    return jnp.dot(a, b, preferred_element_type=jnp.float32)


def _ada_kernel(c_ref, w_ref, b_ref, o_ref):
    c = c_ref[...]
    act = (c * jax.nn.sigmoid(c)).astype(jnp.bfloat16)
    o_ref[0] = _dot(act, w_ref[0].astype(jnp.bfloat16)) + b_ref[0]


def _ada_mod(c, w_ada, b_ada):
    depth, d, six_d = w_ada.shape
    batch = c.shape[0]
    n_col = six_d // d
    return pl.pallas_call(
        _ada_kernel,
        out_shape=jax.ShapeDtypeStruct((depth, batch, six_d), jnp.float32),
        grid=(depth, n_col),
        in_specs=[
            pl.BlockSpec((batch, d), lambda l, j: (0, 0)),
            pl.BlockSpec((1, d, d), lambda l, j: (l, 0, j)),
            pl.BlockSpec((1, 1, d), lambda l, j: (l, 0, j)),
        ],
        out_specs=pl.BlockSpec((1, batch, d), lambda l, j: (l, 0, j)),
        compiler_params=_params("arbitrary", "arbitrary"),
        name="ada_mod",
    )(c, w_ada, b_ada.reshape(depth, 1, six_d))


def _proj_kernel(x_ref, mod_ref, gpre_ref, win_ref, fb_ref, gq_ref, wuq_ref, gkv_ref, wukv_ref,
                 cos_ref, sin_ref, tril_ref,
                 qf_ref, eq_ref, kf_ref, ek_ref, vf_ref, qm_ref, km_ref, vm_ref, kmean_ref,
                 qn_ref, qr_ref, kn_ref, kr_ref, vc_ref, carry_ref):
    t = pl.program_id(1)
    bf16 = jnp.bfloat16
    x = x_ref[0]
    shift = mod_ref[0, :, 0:D_MODEL]
    scale = mod_ref[0, :, D_MODEL:2 * D_MODEL]
    h = (_rms(x, gpre_ref[...]) * (1.0 + scale) + shift).astype(bf16)

    def seg(c0, width):
        return _dot(h, win_ref[:, c0:c0 + width])

    qf_ref[0] = (seg(_C_QF, FOX_W) * (FOX_HEAD_DIM ** -0.5 * LOG2E)).astype(bf16)
    kf_ref[0] = seg(_C_KF, FOX_W).astype(bf16)
    vf_ref[0] = seg(_C_VF, FOX_W).astype(bf16)
    qm_ref[0] = (seg(_C_QM, MOBA_W) * (MOBA_HEAD_DIM ** -0.5 * LOG2E)).astype(bf16)
    km = seg(_C_KM, MOBA_W)
    km_ref[0] = km.astype(bf16)
    vm_ref[0] = seg(_C_VM, MOBA_W).astype(bf16)
    kmean_ref[0, 0] = jnp.mean(km.reshape(TM // MOBA_BLOCK, MOBA_BLOCK, MOBA_W), axis=1)

    lane = lax.broadcasted_iota(jnp.int32, (1, V7X_LANES), 1)
    slot = lane % FGATE_SLOT
    used = (lane < FGATE_SLOT * FOX_HEADS) & (slot < 2 * N_SPLIT)
    fl = seg(_C_FG, V7X_LANES) + fb_ref[...]
    logf = jnp.where(used, jnp.minimum(fl, 0.0) - jnp.log1p(jnp.exp(-jnp.abs(fl))), 0.0)

    @pl.when(t == 0)
    def _():
        carry_ref[...] = jnp.zeros_like(carry_ref)

    fcum = _dot(tril_ref[...], jnp.concatenate(_split3(logf), axis=0)) + carry_ref[0:1, :]
    carry_ref[0:1, :] = fcum[TM - 1:TM, :]
    hi, mid, lo = (p.astype(jnp.float32) for p in _split3(fcum * LOG2E))
    parts = jnp.where(slot % N_SPLIT == 0, hi, jnp.where(slot % N_SPLIT == 1, mid, lo))
    eq_ref[0] = jnp.where(used, jnp.where(slot < N_SPLIT, parts, 1.0), 0.0).astype(bf16)
    ek_ref[0] = jnp.where(used, jnp.where(slot < N_SPLIT, 1.0, -parts), 0.0).astype(bf16)

    cos = cos_ref[...]
    sin = sin_ref[...]
    mla_scale = (MLA_NOPE_DIM + MLA_ROPE_DIM) ** -0.5 * LOG2E
    cq = _rms(seg(_C_CQ, MLA_Q_RANK), gq_ref[...]).astype(bf16)
    n_nope = MLA_HEADS * MLA_NOPE_DIM
    n_rot = MLA_HEADS * MLA_ROPE_DIM
    qn_ref[0] = (_dot(cq, wuq_ref[:, 0:n_nope]) * mla_scale).astype(bf16)
    q_rot = _dot(cq, wuq_ref[:, n_nope:n_nope + n_rot])
    q_rot_sw = _dot(cq, wuq_ref[:, n_nope + n_rot:n_nope + 2 * n_rot])
    cos2 = jnp.concatenate([cos] * (n_rot // V7X_LANES), axis=1)
    sin2 = jnp.concatenate([sin] * (n_rot // V7X_LANES), axis=1)
    qr_ref[0] = ((q_rot * cos2 + q_rot_sw * sin2) * mla_scale).astype(bf16)
    ckv = _rms(seg(_C_CKV, MLA_KV_RANK), gkv_ref[...]).astype(bf16)
    kn_ref[0] = _dot(ckv, wukv_ref[:, 0:n_nope]).astype(bf16)
    vc_ref[0] = _dot(ckv, wukv_ref[:, n_nope:n_nope + MLA_W]).astype(bf16)
    kr_ref[0] = (seg(_C_KR, V7X_LANES) * cos + seg(_C_KRS, V7X_LANES) * sin).astype(bf16)


def _proj(x, mod_l, gpre, win, fbias, gq, wuq, gkv, wukv, cos_t, sin_t, tril):
    batch, seq, d = x.shape
    nt = seq // TM
    bf16 = jnp.bfloat16

    def tok(width):
        return pl.BlockSpec((1, TM, width), lambda b, t: (b, t, 0))

    def out(width):
        return jax.ShapeDtypeStruct((batch, seq, width), bf16)

    widths = dict(qf=FOX_W, eq=V7X_LANES, kf=FOX_W, ek=V7X_LANES, vf=FOX_W, qm=MOBA_W, km=MOBA_W, vm=MOBA_W)
    mla_widths = dict(qn=MLA_HEADS * MLA_NOPE_DIM, qr=MLA_HEADS * MLA_ROPE_DIM, kn=MLA_HEADS * MLA_NOPE_DIM,
                      kr=V7X_LANES, vc=MLA_W)
    nb = TM // MOBA_BLOCK
    out_shape = ([out(w) for w in widths.values()]
                 + [jax.ShapeDtypeStruct((batch, nt, nb, MOBA_W), jnp.float32)]
                 + [out(w) for w in mla_widths.values()])
    out_specs = ([tok(w) for w in widths.values()]
                 + [pl.BlockSpec((1, 1, nb, MOBA_W), lambda b, t: (b, t, 0, 0))]
                 + [tok(w) for w in mla_widths.values()])
    res = pl.pallas_call(
        _proj_kernel,
        out_shape=out_shape,
        grid=(batch, nt),
        in_specs=[
            tok(d),
            pl.BlockSpec((1, 1, 6 * d), lambda b, t: (b, 0, 0)),
            _const_spec((1, d)),
            _const_spec(win.shape),
            _const_spec((1, V7X_LANES)),
            _const_spec((1, MLA_Q_RANK)),
            _const_spec(wuq.shape),
            _const_spec((1, MLA_KV_RANK)),
            _const_spec(wukv.shape),
            pl.BlockSpec((TM, V7X_LANES), lambda b, t: (t, 0)),
            pl.BlockSpec((TM, V7X_LANES), lambda b, t: (t, 0)),
            _const_spec(tril.shape),
        ],
        out_specs=out_specs,
        scratch_shapes=[pltpu.VMEM((8, V7X_LANES), jnp.float32)],
        compiler_params=_params("arbitrary", "arbitrary"),
        name="in_proj",
    )(x, mod_l, gpre, win, fbias, gq, wuq, gkv, wukv, cos_t, sin_t, tril)
    names = list(widths) + ["kmean"] + list(mla_widths)
    r = dict(zip(names, res))
    r["kmean"] = r["kmean"].reshape(batch, seq // MOBA_BLOCK, MOBA_W)
    return r


def _transpose_to_bf16(x):
    return x.astype(jnp.float32).T.astype(jnp.bfloat16)


def _head_rows(x_pair, e):
    row = lax.broadcasted_iota(jnp.int32, x_pair.shape, 0)
    keep = (row >= HEAD_LANES) if e else (row < HEAD_LANES)
    return jnp.where(keep, x_pair, 0.0)


def _causal_bias_t():
    key = lax.broadcasted_iota(jnp.int32, (TQ, TQ), 0)
    qry = lax.broadcasted_iota(jnp.int32, (TQ, TQ), 1)
    return jnp.where(key <= qry, 0.0, MASK_NEG)


def _tile_rows(t):
    return pl.ds(pl.multiple_of(t * TQ, TQ), TQ)


class _FlashScratch(NamedTuple):
    vt: object
    rhs: object
    s: object
    p: object
    alpha: object
    m: object
    acc: object


def _flash_begin(v_ref, sc):
    n_heads = sc.vt.shape[1] // V_ROWS
    extra = lax.broadcasted_iota(jnp.int32, (V_ROWS - HEAD_LANES, TQ), 0)
    ones_row = jnp.where(extra == 0, 1.0, 0.0).astype(jnp.bfloat16)
    for c in range(sc.vt.shape[0]):
        v_t = _transpose_to_bf16(v_ref[0, c * TQ:(c + 1) * TQ, :])
        for h in range(n_heads):
            sc.vt[c, h * V_ROWS:h * V_ROWS + HEAD_LANES, :] = v_t[h * HEAD_LANES:(h + 1) * HEAD_LANES]
            sc.vt[c, h * V_ROWS + HEAD_LANES:(h + 1) * V_ROWS, :] = ones_row
    sc.m[...] = jnp.full(sc.m.shape, MASK_NEG, jnp.float32)
    sc.acc[...] = jnp.zeros(sc.acc.shape, jnp.float32)


def _flash_pipeline(tab_ref, segments, heads, n_heads, lhs_tile, sc):
    slot = {h: h - heads[0] for h in heads}
    n_steps = sum(count for count, _ in segments)

    def ij(t):
        t = jnp.clip(t, 0, n_steps - 1)
        return tab_ref[0, t], tab_ref[1, t]

    def scores(t, bias_fn):
        i, j = ij(t)
        lhs = {h // 2: lhs_tile(j, h // 2) for h in heads if h % 2 == 0}
        out = {}
        for h in heads:
            s = _dot(lhs[h // 2], sc.rhs[i * n_heads + h])
            b = bias_fn(i, j, h)
            out[h] = s if b is None else s + b
        return out

    def store_scores(vals):
        for h in heads:
            sc.s[slot[h]] = vals[h]

    def softmax(t):
        i, _ = ij(t)
        blocks = [(h, slice(c * V7X_LANES, (c + 1) * V7X_LANES)) for h in heads for c in range(TQ // V7X_LANES)]
        s_val = {k: sc.s[slot[k[0]], :, k[1]] for k in blocks}
        m_old = {k: sc.m[i * n_heads + k[0], :, k[1]] for k in blocks}
        m_new = {k: jnp.maximum(m_old[k], jnp.max(s_val[k], axis=0, keepdims=True)) for k in blocks}
        alpha = {k: jnp.exp2(m_old[k] - m_new[k]) for k in blocks}
        probs = {k: jnp.exp2(s_val[k] - m_new[k]) for k in blocks}
        for k in blocks:
            h, lanes = k
            sc.m[i * n_heads + h, :, lanes] = m_new[k]
            sc.alpha[slot[h], :, lanes] = alpha[k]
            sc.p[slot[h], :, lanes] = probs[k].astype(jnp.bfloat16)

    def values(t):
        i, j = ij(t)
        pv = {h: _dot(sc.vt[j, h * V_ROWS:(h + 1) * V_ROWS, :], sc.p[slot[h]]) for h in heads}
        for h in heads:
            st = i * n_heads + h
            sc.acc[st] = sc.alpha[slot[h]] * sc.acc[st] + pv[h]

    def prologue(_, carry):
        for h in heads:
            sc.alpha[slot[h]] = jnp.ones(sc.alpha.shape[1:], jnp.float32)
            sc.p[slot[h]] = jnp.zeros(sc.p.shape[1:], jnp.bfloat16)
        store_scores(scores(0, segments[0][1]))
        return carry

    lax.fori_loop(0, tab_ref[0, n_steps], prologue, 0)

    first = 0
    for count, bias_fn in segments:

        def body(t, carry, bias_fn=bias_fn):
            nxt = scores(t + 1, bias_fn)
            values(t - 1)
            softmax(t)
            store_scores(nxt)
            return carry

        lax.fori_loop(max(first - 1, 0), first + count - 1, body, 0)
        first += count
    values(n_steps - 2)
    softmax(n_steps - 1)
    values(n_steps - 1)


def _flash_finish(n_heads, sc, g_ref, o_ref):
    def body(iq, carry):
        heads = []
        for h in range(n_heads):
            acc = sc.acc[iq * n_heads + h]
            heads.append(acc[0:HEAD_LANES] * (1.0 / acc[HEAD_LANES:HEAD_LANES + 1]))
        o_t = jnp.concatenate(heads, axis=0)
        o_ref[0, _tile_rows(iq), :] = _rms(o_t.T, g_ref[...]).astype(o_ref.dtype)
        return carry

    lax.fori_loop(0, o_ref.shape[1] // TQ, body, 0, unroll=SETUP_UNROLL)


def _causal_segments(n_tiles):
    return [(n_tiles * (n_tiles - 1) // 2, lambda i, j, h: None), (n_tiles, lambda i, j, h: _causal_bias_t())]


def _fox_kernel(tab_ref, q_ref, eq_ref, k_ref, ek_ref, v_ref, g_ref, o_ref, *scratch):
    sc = _FlashScratch(*scratch)
    n_tiles = q_ref.shape[1] // TQ
    _flash_begin(v_ref, sc)

    def build_rhs(iq, carry):
        rows = _tile_rows(iq)
        q_t = q_ref[0, rows, :].astype(jnp.float32).T
        eq_t = eq_ref[0, rows, :].astype(jnp.float32).T
        slot_head = lax.broadcasted_iota(jnp.int32, eq_t.shape, 0) // FGATE_SLOT
        for h in range(FOX_HEADS):
            pair, e = divmod(h, 2)
            q_h = _head_rows(q_t[pair * V7X_LANES:(pair + 1) * V7X_LANES], e)
            e_h = jnp.where(slot_head == h, eq_t, 0.0)
            sc.rhs[iq * FOX_HEADS + h] = jnp.concatenate([q_h, e_h], axis=0).astype(jnp.bfloat16)
        return carry

    lax.fori_loop(0, n_tiles, build_rhs, 0, unroll=SETUP_UNROLL)

    def lhs_tile(j, pair):
        rows = _tile_rows(j)
        return jnp.concatenate([k_ref[0, rows, pair * V7X_LANES:(pair + 1) * V7X_LANES], ek_ref[0, rows, :]], axis=1)

    _flash_pipeline(tab_ref, _causal_segments(n_tiles), range(FOX_HEADS), FOX_HEADS, lhs_tile, sc)
    _flash_finish(FOX_HEADS, sc, g_ref, o_ref)


def _moba_kernel(tab_ref, q_ref, k_ref, v_ref, kmean_ref, bias_ref, g_ref, o_ref, *scratch):
    sc = _FlashScratch(*scratch)
    f32, bf16 = jnp.float32, jnp.bfloat16
    n_tiles = q_ref.shape[1] // TQ
    n_blocks = kmean_ref.shape[1]
    _flash_begin(v_ref, sc)
    sel_rows = V7X_BF16_SUBLANES
    blk = lax.broadcasted_iota(jnp.int32, (sel_rows, TQ), 0)
    lane = lax.broadcasted_iota(jnp.int32, (1, V7X_LANES), 1)

    def build_rhs(iq, carry):
        q_t = q_ref[0, _tile_rows(iq), :].astype(f32).T
        for h in range(MOBA_HEADS):
            pair, e = divmod(h, 2)
            q_h = _head_rows(q_t[pair * V7X_LANES:(pair + 1) * V7X_LANES], e)
            q_hb = q_h.astype(bf16)
            kmean = kmean_ref[0, :, pair * V7X_LANES:(pair + 1) * V7X_LANES]
            km_hi = kmean.astype(bf16).astype(f32)
            gate_lhs = jnp.concatenate([jnp.concatenate([km_hi, kmean - km_hi], axis=1),
                                        jnp.zeros((sel_rows - n_blocks, SCORE_K), f32)], axis=0).astype(bf16)
            gate = _dot(gate_lhs, jnp.concatenate([q_hb, q_hb], axis=0))
            beaten = jnp.zeros((sel_rows, TQ), f32)
            for mblk in range(n_blocks - 1):
                gm = gate[mblk:mblk + 1, :]
                wins = (gm > gate) | ((gm == gate) & (mblk < blk))
                beaten = beaten + jnp.where(wins, jnp.where(mblk < iq, 1.0, 0.0), 0.0)
            keep = ((blk < iq) & (beaten < MOBA_TOPK)) | (blk == iq)
            sel = jnp.where(keep, 0.0, MASK_NEG)[0:n_blocks]
            far = lax.broadcasted_iota(jnp.int32, (n_blocks, TQ), 0) <= iq - 2
            c = bias_ref[h, 2, 0:1, :]
            c_hi = c.astype(bf16).astype(f32)
            extras = [sel, jnp.where(far, c_hi, 0.0), jnp.where(far, c - c_hi, 0.0)]
            pad = jnp.zeros((V7X_LANES - len(extras) * n_blocks, TQ), f32)
            sc.rhs[iq * MOBA_HEADS + h] = jnp.concatenate([q_h] + extras + [pad], axis=0).astype(bf16)
        return carry

    lax.fori_loop(0, n_tiles, build_rhs, 0, unroll=SETUP_UNROLL)

    def lhs_tile(j, pair):
        hit = (lane % n_blocks == j) & (lane < 3 * n_blocks)
        onehot = jnp.broadcast_to(jnp.where(hit, 1.0, 0.0).astype(bf16), (TQ, V7X_LANES))
        return jnp.concatenate([k_ref[0, _tile_rows(j), pair * V7X_LANES:(pair + 1) * V7X_LANES], onehot], axis=1)

    n_far = (n_tiles - 1) * (n_tiles - 2) // 2
    segments = [(n_far, lambda i, j, h: None), (n_tiles - 1, lambda i, j, h: bias_ref[h, 1]),
                (n_tiles, lambda i, j, h: bias_ref[h, 0])]
    _flash_pipeline(tab_ref, segments, range(MOBA_HEADS), MOBA_HEADS, lhs_tile, sc)
    _flash_finish(MOBA_HEADS, sc, g_ref, o_ref)


def _mla_kernel(tab_ref, qn_ref, qr_ref, kn_ref, kr_ref, v_ref, g_ref, o_ref, *scratch):
    sc = _FlashScratch(*scratch)
    n_tiles = qn_ref.shape[1] // TQ
    _flash_begin(v_ref, sc)
    heads_per_rot = V7X_LANES // MLA_ROPE_DIM
    rot_slot = lax.broadcasted_iota(jnp.int32, (V7X_LANES, TQ), 0) // MLA_ROPE_DIM

    def build_rhs(iq, carry):
        rows = _tile_rows(iq)
        qn_t = qn_ref[0, rows, :].astype(jnp.float32).T
        qr_t = qr_ref[0, rows, :].astype(jnp.float32).T
        for h in range(MLA_HEADS):
            pair, e = divmod(h, 2)
            quad, slot = divmod(h, heads_per_rot)
            q_h = _head_rows(qn_t[pair * V7X_LANES:(pair + 1) * V7X_LANES], e)
            r_h = jnp.where(rot_slot == slot, qr_t[quad * V7X_LANES:(quad + 1) * V7X_LANES], 0.0)
            sc.rhs[iq * MLA_HEADS + h] = jnp.concatenate([q_h, r_h], axis=0).astype(jnp.bfloat16)
        return carry

    lax.fori_loop(0, n_tiles, build_rhs, 0, unroll=SETUP_UNROLL)

    def lhs_tile(j, pair):
        rows = _tile_rows(j)
        return jnp.concatenate([kn_ref[0, rows, pair * V7X_LANES:(pair + 1) * V7X_LANES], kr_ref[0, rows, :]], axis=1)

    for h0 in range(0, MLA_HEADS, PIPE_HEADS):
        _flash_pipeline(tab_ref, _causal_segments(n_tiles), range(h0, h0 + PIPE_HEADS), MLA_HEADS, lhs_tile, sc)
    _flash_finish(MLA_HEADS, sc, g_ref, o_ref)


def _tile_pairs(n_tiles, split_previous):
    near = 2 if split_previous else 1
    pairs = [(i, j) for i in range(n_tiles) for j in range(i - near + 1)]
    for d in range(near - 1, -1, -1):
        pairs += [(i, i - d) for i in range(d, n_tiles)]
    return jnp.asarray(np.array(pairs + [(1, 1)], np.int32).T)


def _attn_call(kernel, name, n_heads, split_previous, arrays, const_arrays):
    batch, seq, _ = arrays[0].shape
    n_tiles = seq // TQ
    out_width = n_heads * HEAD_LANES
    row = lambda a: pl.BlockSpec((1,) + a.shape[1:], lambda b: (b, 0, 0))
    return pl.pallas_call(
        kernel,
        out_shape=jax.ShapeDtypeStruct((batch, seq, out_width), jnp.bfloat16),
        grid=(batch,),
        in_specs=([pl.BlockSpec(memory_space=pltpu.SMEM)] + [row(a) for a in arrays]
                  + [_const_spec(a.shape) for a in const_arrays]),
        out_specs=pl.BlockSpec((1, seq, out_width), lambda b: (b, 0, 0)),
        scratch_shapes=[
            pltpu.VMEM((n_tiles, n_heads * V_ROWS, TQ), jnp.bfloat16),
            pltpu.VMEM((n_tiles * n_heads, SCORE_K, TQ), jnp.bfloat16),
            pltpu.VMEM((PIPE_HEADS, TQ, TQ), jnp.float32),
            pltpu.VMEM((PIPE_HEADS, TQ, TQ), jnp.bfloat16),
            pltpu.VMEM((PIPE_HEADS, 1, TQ), jnp.float32),
            pltpu.VMEM((n_tiles * n_heads, 1, TQ), jnp.float32),
            pltpu.VMEM((n_tiles * n_heads, V_ROWS, TQ), jnp.float32),
        ],
        compiler_params=_params("arbitrary"),
        name=name,
    )(_tile_pairs(n_tiles, split_previous), *arrays, *const_arrays)


def _bias_tile_kernel(table_ref, bucket_ref, o_ref):
    h = pl.program_id(0)
    for k in range(bucket_ref.shape[0]):
        bkt = bucket_ref[k]
        tile = jnp.full(bkt.shape, MASK_NEG, jnp.float32)
        for b in range(T5_BUCKETS):
            tile = jnp.where(bkt == b, table_ref[h, b] * LOG2E, tile)
        o_ref[0, k] = tile


def _moba_bias_tiles(t5_table):
    bucket = _t5_bucket_table(3 * TQ)
    key = np.arange(TQ)[:, None]
    qry = np.arange(TQ)[None, :]
    kinds = []
    for k in range(3):
        dist = qry - key + k * TQ
        kinds.append(np.where(dist >= 0, bucket[np.maximum(dist, 0)], -1))
    buckets = jnp.asarray(np.stack(kinds), jnp.int32)
    return pl.pallas_call(
        _bias_tile_kernel,
        out_shape=jax.ShapeDtypeStruct((MOBA_HEADS, 3, TQ, TQ), jnp.float32),
        grid=(MOBA_HEADS,),
        in_specs=[pl.BlockSpec(memory_space=pltpu.SMEM), _const_spec(buckets.shape)],
        out_specs=pl.BlockSpec((1, 3, TQ, TQ), lambda h: (h, 0, 0, 0)),
        compiler_params=_params("arbitrary"),
        name="t5_bias_tiles",
    )(t5_table.T, buckets)


def _mix_ffn_kernel(x_ref, of_ref, om_ref, oc_ref, mod_ref, gmix_ref, gpre_ref, gpost_ref,
                    wout_ref, wgu_ref, wd_ref, o_ref):
    bf16 = jnp.bfloat16
    gate_a = mod_ref[0, :, 2 * D_MODEL:3 * D_MODEL]
    shift = mod_ref[0, :, 3 * D_MODEL:4 * D_MODEL]
    scale = mod_ref[0, :, 4 * D_MODEL:5 * D_MODEL]
    gate_f = mod_ref[0, :, 5 * D_MODEL:6 * D_MODEL]
    half = TM // 2

    def mix(r):
        rows = slice(r * half, (r + 1) * half)
        o = jnp.concatenate([of_ref[0, rows, :], om_ref[0, rows, :], oc_ref[0, rows, :]], axis=1)
        x = x_ref[0, rows, :] + gate_a * _rms(_dot(o, wout_ref[...]), gmix_ref[...])
        return x, (_rms(x, gpre_ref[...]) * (1.0 + scale) + shift).astype(bf16)

    def chunk(h, c0, c1):
        g = _dot(h, wgu_ref[:, c0:c1])
        u = _dot(h, wgu_ref[:, D_FF + c0:D_FF + c1])
        return _dot((g * jax.nn.sigmoid(g) * u).astype(bf16), wd_ref[c0:c1, :])

    def finish(r, x, acc):
        o_ref[0, r * half:(r + 1) * half, :] = x + gate_f * _rms(acc, gpost_ref[...])

    x0, h0 = mix(0)
    acc0 = chunk(h0, *FFN_CHUNKS[0])
    x1, h1 = mix(1)
    acc0 = acc0 + chunk(h0, *FFN_CHUNKS[1])
    acc0 = acc0 + chunk(h0, *FFN_CHUNKS[2])
    acc1 = chunk(h1, *FFN_CHUNKS[0])
    finish(0, x0, acc0)
    acc1 = acc1 + chunk(h1, *FFN_CHUNKS[1])
    acc1 = acc1 + chunk(h1, *FFN_CHUNKS[2])
    finish(1, x1, acc1)


def _mix_ffn(x, o_f, o_m, o_c, mod_l, gmix, gpre, gpost, w_out, wgu, wd):
    batch, seq, d = x.shape

    def tok(width):
        return pl.BlockSpec((1, TM, width), lambda b, t: (b, t, 0))

    return pl.pallas_call(
        _mix_ffn_kernel,
        out_shape=jax.ShapeDtypeStruct(x.shape, x.dtype),
        grid=(batch, seq // TM),
        in_specs=[tok(d), tok(FOX_W), tok(MOBA_W), tok(MLA_W),
                  pl.BlockSpec((1, 1, 6 * d), lambda b, t: (b, 0, 0)),
                  _const_spec((1, d)), _const_spec((1, d)), _const_spec((1, d)),
                  _const_spec(w_out.shape), _const_spec(wgu.shape), _const_spec(wd.shape)],
        out_specs=tok(d),
        compiler_params=_params("arbitrary", "arbitrary"),
        name="mix_ffn",
    )(x, o_f, o_m, o_c, mod_l, gmix, gpre, gpost, w_out, wgu, wd)


def _rope_tables(seq):
    half = MLA_ROPE_DIM // 2
    inv_freq = 1.0 / (ROPE_THETA ** (jnp.arange(half, dtype=jnp.float32) / half))
    ang = jnp.arange(seq).astype(jnp.float32)[:, None] * inv_freq[None, :]
    reps = V7X_LANES // MLA_ROPE_DIM
    cos = jnp.tile(jnp.concatenate([jnp.cos(ang), jnp.cos(ang)], axis=1), (1, reps))
    sin = jnp.tile(jnp.concatenate([-jnp.sin(ang), jnp.sin(ang)], axis=1), (1, reps))
    return cos, sin


def kernel(x, c, t5_table, w_ada, b_ada, g_mix_pre, g_mix_post, w_in, b_forget, g_q_lat, w_uq, g_kv_lat, w_ukv,
           g_group, w_out, g_ffn_pre, g_ffn_post, w_gate_up, w_down):
    batch, seq, d = x.shape
    assert d == D_MODEL and seq % TM == 0 and TM % MOBA_BLOCK == 0 and TQ == MOBA_BLOCK
    bf16 = jnp.bfloat16
    win, wuq, wukv = (f(w).astype(bf16) for f, w in ((_reorder_in_proj, w_in), (_reorder_uq, w_uq), (_reorder_ukv, w_ukv)))
    wout, wgu, wd = w_out.astype(bf16), w_gate_up.astype(bf16), w_down.astype(bf16)
    cos_t, sin_t = _rope_tables(seq)
    tril = np.tril(np.ones((TM, TM), np.float32))
    tril = jnp.asarray(np.concatenate([tril] * N_SPLIT, axis=1), bf16)
    moba_bias = _moba_bias_tiles(t5_table)
    fg_lane = np.arange(V7X_LANES)
    fg_used = (fg_lane < FGATE_SLOT * FOX_HEADS) & (fg_lane % FGATE_SLOT < 2 * N_SPLIT)
    fg_head = np.minimum(fg_lane // FGATE_SLOT, FOX_HEADS - 1)

    mod = _ada_mod(c, w_ada, b_ada)
    for l in range(DEPTH):
        mod_l = mod[l].reshape(batch, 1, 6 * d)
        fbias = jnp.where(jnp.asarray(fg_used), b_forget[l][fg_head], 0.0).reshape(1, V7X_LANES)
        pr = _proj(x, mod_l, g_mix_pre[l].reshape(1, d),
                   win[l], fbias, g_q_lat[l].reshape(1, -1), wuq[l], g_kv_lat[l].reshape(1, -1), wukv[l],
                   cos_t, sin_t, tril)
        g_a = g_group[l, :FOX_W].reshape(1, -1)
        g_b = g_group[l, FOX_W:FOX_W + MOBA_W].reshape(1, -1)
        g_c = g_group[l, FOX_W + MOBA_W:].reshape(1, -1)
        o_f = _attn_call(_fox_kernel, "fox_attn", FOX_HEADS, False,
                         [pr["qf"], pr["eq"], pr["kf"], pr["ek"], pr["vf"]], [g_a])
        o_m = _attn_call(_moba_kernel, "moba_attn", MOBA_HEADS, True,
                         [pr["qm"], pr["km"], pr["vm"], pr["kmean"]], [moba_bias, g_b])
        o_c = _attn_call(_mla_kernel, "mla_attn", MLA_HEADS, False,
                         [pr["qn"], pr["qr"], pr["kn"], pr["kr"], pr["vc"]], [g_c])
        x = _mix_ffn(x, o_f, o_m, o_c, mod_l, g_mix_post[l].reshape(1, d), g_ffn_pre[l].reshape(1, d),
                     g_ffn_post[l].reshape(1, d), wout[l], wgu[l], wd[l])
    return x
```
